```python
import math
import jax, jax.numpy as jnp
from jax import lax
import numpy as np

D_MODEL = 2048
BATCH = 16
SEQ = 2048
DEPTH = 4

CHUNK = 64
Q_BLOCK = 128
N_ATT_HEADS = 8
HEAD_DIM = 64
V_HEAD_DIM = 2 * HEAD_DIM
D_ATT_QK = N_ATT_HEADS * 2 * HEAD_DIM
D_ATT_V = N_ATT_HEADS * V_HEAD_DIM
D_SSM = D_MODEL // 2
SSM_GROUP = 16
N_SSM_GROUPS = D_SSM // SSM_GROUP
SSM_STATE = 64
DT_MIN = 1e-3
DT_MAX = 1e-1
D_FF = 5504
CONV_W = 3
N_BRANCH = 2
IN_COLS = 2 * D_ATT_QK + D_ATT_V + D_SSM + N_BRANCH * D_MODEL
EPS = 1e-6

kernel_name = "hybrid_diffattn_s5_convffn_streaming"


def rmsnorm(x, g):
    xf = x.astype(jnp.float32)
    y = xf * lax.rsqrt(jnp.mean(xf * xf, axis=-1, keepdims=True) + EPS)
    return (y * g.astype(jnp.float32)).astype(x.dtype)


def alibi_slopes():
    h = jnp.arange(1, N_ATT_HEADS + 1, dtype=jnp.float32)
    return 2.0 ** (-8.0 * h / N_ATT_HEADS)


def diff_attention(q, k, v, lam):
    b_, s_ = q.shape[0], q.shape[1]
    nblk = s_ // Q_BLOCK
    scale = HEAD_DIM ** -0.5
    kt = k.transpose(0, 2, 3, 1, 4)
    vt = v.transpose(0, 2, 1, 3)
    qb = (q * scale).reshape(b_, nblk, Q_BLOCK, N_ATT_HEADS, 2, HEAD_DIM)
    qb = qb.transpose(1, 0, 3, 4, 2, 5)
    slopes = alibi_slopes()
    kpos = jnp.arange(s_, dtype=jnp.int32)

    def block(args):
        qblk, bi = args
        qpos = bi * Q_BLOCK + jnp.arange(Q_BLOCK, dtype=jnp.int32)
        chunk_end = (qpos // CHUNK + 1) * CHUNK
        visible = kpos[None, :] < chunk_end[:, None]
        dist = jnp.abs(qpos[:, None] - kpos[None, :]).astype(jnp.float32)
        bias = jnp.where(visible[None], -slopes[:, None, None] * dist[None], -jnp.inf)
        s = jnp.einsum('bhcqd,bhcsd->bhcqs', qblk, kt).astype(jnp.float32) + bias[None, :, None]
        p = jax.nn.softmax(s, axis=-1)
        w = (p[:, :, 0] - lam * p[:, :, 1]).astype(vt.dtype)
        return jnp.einsum('bhqs,bhsd->bhqd', w, vt)

    o = lax.map(block, (qb, jnp.arange(nblk, dtype=jnp.int32)))
    return o.transpose(1, 0, 3, 2, 4).reshape(b_, s_, N_ATT_HEADS, V_HEAD_DIM)


def _complex_affine_combine(earlier, later):
    ar1, ai1, br1, bi1 = earlier
    ar2, ai2, br2, bi2 = later
    return (ar2 * ar1 - ai2 * ai1,
            ar2 * ai1 + ai2 * ar1,
            ar2 * br1 - ai2 * bi1 + br2,
            ar2 * bi1 + ai2 * br1 + bi2)


def s5_ssm(u, a_re, a_im, log_dt, b_re, b_im, c_re, c_im, d_skip):
    b_, s_, _ = u.shape
    ug = u.reshape(b_, s_, N_SSM_GROUPS, SSM_GROUP)
    dt = jnp.exp(log_dt)[:, None]
    mag = jnp.exp(dt * a_re)
    ab_re = mag * jnp.cos(dt * a_im)
    ab_im = mag * jnp.sin(dt * a_im)
    den = a_re * a_re + a_im * a_im
    zr = ab_re - 1.0
    zi = ab_im
    f_re = (zr * a_re + zi * a_im) / den
    f_im = (zi * a_re - zr * a_im) / den
    bb_re = f_re[..., None] * b_re - f_im[..., None] * b_im
    bb_im = f_re[..., None] * b_im + f_im[..., None] * b_re
    x_re = jnp.einsum('bsgi,gpi->sbgp', ug, bb_re)
    x_im = jnp.einsum('bsgi,gpi->sbgp', ug, bb_im)
    a_re_s = jnp.broadcast_to(ab_re[None, None], (s_, 1) + ab_re.shape)
    a_im_s = jnp.broadcast_to(ab_im[None, None], (s_, 1) + ab_im.shape)
    _, _, h_re, h_im = lax.associative_scan(
        _complex_affine_combine, (a_re_s, a_im_s, x_re, x_im), axis=0)
    y = (jnp.einsum('sbgp,gop->bsgo', h_re, c_re)
         - jnp.einsum('sbgp,gop->bsgo', h_im, c_im))
    y = y + d_skip.reshape(N_SSM_GROUPS, SSM_GROUP) * ug
    return y.reshape(b_, s_, D_SSM)


def causal_dwconv(x, w, b):
    s_ = x.shape[1]
    xp = jnp.pad(x, ((0, 0), (CONV_W - 1, 0), (0, 0)))
    y = b + xp[:, 0:s_] * w[0]
    for j in range(1, CONV_W):
        y = y + xp[:, j:j + s_] * w[j]
    return y


def setup_inputs(seed: int = 0) -> dict:
    key = jax.random.key(seed)
    ks = jax.random.split(key, 32)
    L, D = DEPTH, D_MODEL
    G, P, I = N_SSM_GROUPS, SSM_STATE, SSM_GROUP
    f32 = jnp.float32
    nrm = lambda k, shape, s: (jax.random.normal(k, shape, f32) * s)
    a_im = jnp.broadcast_to(jnp.pi * jnp.arange(P, dtype=f32), (L, G, P))
    log_dt = jax.random.uniform(ks[10], (L, G), f32, math.log(DT_MIN), math.log(DT_MAX))
    return {
        "x": nrm(ks[0], (BATCH, SEQ, D), 1.0),
        "norm1_g": 1.0 + nrm(ks[1], (L, D), 0.02),
        "w_in": nrm(ks[2], (L, D, IN_COLS), D ** -0.5),
        "q_norm_g": 1.0 + nrm(ks[3], (L, HEAD_DIM), 0.02),
        "k_norm_g": 1.0 + nrm(ks[4], (L, HEAD_DIM), 0.02),
        "lambda_q1": nrm(ks[5], (L, HEAD_DIM), 0.1),
        "lambda_k1": nrm(ks[6], (L, HEAD_DIM), 0.1),
        "lambda_q2": nrm(ks[7], (L, HEAD_DIM), 0.1),
        "lambda_k2": nrm(ks[8], (L, HEAD_DIM), 0.1),
        "subln_g": 1.0 + nrm(ks[9], (L, V_HEAD_DIM), 0.02),
        "ssm_a_re": -0.5 + nrm(ks[11], (L, G, P), 0.01),
        "ssm_a_im": a_im + 0.0,
        "ssm_log_dt": log_dt,
        "ssm_b_re": nrm(ks[12], (L, G, P, I), (2 * I) ** -0.5),
        "ssm_b_im": nrm(ks[13], (L, G, P, I), (2 * I) ** -0.5),
        "ssm_c_re": nrm(ks[14], (L, G, I, P), (2 * P) ** -0.5),
        "ssm_c_im": nrm(ks[15], (L, G, I, P), (2 * P) ** -0.5),
        "ssm_d": nrm(ks[16], (L, D_SSM), 1.0),
        "ssm_glu_w": nrm(ks[17], (L, D_SSM, D_SSM), D_SSM ** -0.5),
        "ssm_glu_b": nrm(ks[18], (L, D_SSM), 0.01),
        "w_branch_attn": nrm(ks[19], (L, D_ATT_V, D), D_ATT_V ** -0.5),
        "w_branch_ssm": nrm(ks[20], (L, D_SSM, D), D_SSM ** -0.5),
        "w_out": nrm(ks[21], (L, D, D), D ** -0.5),
        "norm2_g": 1.0 + nrm(ks[22], (L, D), 0.02),
        "ffn_w_up": nrm(ks[23], (L, D, 2 * D_FF), D ** -0.5),
        "ffn_conv_w": nrm(ks[24], (L, CONV_W, 2 * D_FF), CONV_W ** -0.5),
        "ffn_conv_b": nrm(ks[25], (L, 2 * D_FF), 0.01),
        "ffn_w_down": nrm(ks[26], (L, D_FF, D), D_FF ** -0.5),
    }


def reference(x, norm1_g, w_in, q_norm_g, k_norm_g, lambda_q1, lambda_k1, lambda_q2,
              lambda_k2, subln_g, ssm_a_re, ssm_a_im, ssm_log_dt, ssm_b_re, ssm_b_im,
              ssm_c_re, ssm_c_im, ssm_d, ssm_glu_w, ssm_glu_b, w_branch_attn,
              w_branch_ssm, w_out, norm2_g, ffn_w_up, ffn_conv_w, ffn_conv_b, ffn_w_down):
    b_, s_, _ = x.shape
    splits = [D_ATT_QK, 2 * D_ATT_QK, 2 * D_ATT_QK + D_ATT_V, 2 * D_ATT_QK + D_ATT_V + D_SSM,
              2 * D_ATT_QK + D_ATT_V + D_SSM + D_MODEL]
    for l in range(DEPTH):
        lam_init = 0.8 - 0.6 * math.exp(-0.3 * l)
        h = rmsnorm(x, norm1_g[l])
        proj = h @ w_in[l]
        q, k, v, u, g_att, g_ssm = jnp.split(proj, splits, axis=-1)
        q = rmsnorm(q.reshape(b_, s_, N_ATT_HEADS, 2, HEAD_DIM), q_norm_g[l])
        k = rmsnorm(k.reshape(b_, s_, N_ATT_HEADS, 2, HEAD_DIM), k_norm_g[l])
        v = v.reshape(b_, s_, N_ATT_HEADS, V_HEAD_DIM)
        lam = (jnp.exp(jnp.sum(lambda_q1[l].astype(jnp.float32) * lambda_k1[l].astype(jnp.float32)))
               - jnp.exp(jnp.sum(lambda_q2[l].astype(jnp.float32) * lambda_k2[l].astype(jnp.float32)))
               + lam_init)
        o = diff_attention(q, k, v, lam)
        o = rmsnorm(o, subln_g[l]) * (1.0 - lam_init)
        o_att = o.reshape(b_, s_, D_ATT_V) @ w_branch_attn[l]

        y = s5_ssm(u, ssm_a_re[l], ssm_a_im[l], ssm_log_dt[l], ssm_b_re[l], ssm_b_im[l],
                   ssm_c_re[l], ssm_c_im[l], ssm_d[l])
        y = jax.nn.gelu(y)
        y = y * jax.nn.sigmoid(y @ ssm_glu_w[l] + ssm_glu_b[l])
        o_ssm = y @ w_branch_ssm[l]

        mixed = jax.nn.sigmoid(g_att) * o_att + jax.nn.sigmoid(g_ssm) * o_ssm
        x = x + mixed @ w_out[l]

        h = rmsnorm(x, norm2_g[l])
        up = causal_dwconv(h @ ffn_w_up[l], ffn_conv_w[l], ffn_conv_b[l])
        a, val = jnp.split(up, 2, axis=-1)
        x = x + (jax.nn.gelu(a) * val) @ ffn_w_down[l]
    return x
```

```python
import functools
import math

import jax
import jax.numpy as jnp
from jax import lax
from jax.experimental import pallas as pl
from jax.experimental.pallas import tpu as pltpu

F32 = jnp.float32
BF16 = jnp.bfloat16

D_MODEL = 2048
N_HEADS = 8
HEAD_DIM = 64
V_HEAD_DIM = 2 * HEAD_DIM
D_QK = N_HEADS * 2 * HEAD_DIM
D_V = N_HEADS * V_HEAD_DIM
D_SSM = D_MODEL // 2
SSM_GROUP = 16
N_GROUPS = D_SSM // SSM_GROUP
SSM_STATE = 64
D_FF = 5504
CONV_W = 3
CHUNK = 64
EPS = 1e-6
IN_COLS = 2 * D_QK + D_V + D_SSM + 2 * D_MODEL

MXU_DIM = 256
FF_TILE = 512
D_FF_PAD = ((D_FF + FF_TILE - 1) // FF_TILE) * FF_TILE
GROUPS_PER_TILE = 8
SSM_NC = GROUPS_PER_TILE * SSM_STATE
NEG_BIG = -1e30
VMEM_LIMIT = 56 * 1024 * 1024


def _gelu_tanh(x):
    c = math.sqrt(2.0 / math.pi)
    return x * (0.5 * (1.0 + jnp.tanh(c * (x + 0.044715 * (x * x * x)))))


def _rms_rows(xf, g):
    ms = jnp.mean(xf * xf, axis=-1, keepdims=True)
    return xf * lax.rsqrt(ms + EPS) * g


def _in_proj_body(x_ref, g_ref, w_ref, qg_ref, kg_ref, ones_ref,
                  q_ref, k_ref, v_ref, u_ref, gate_ref, h_scr, *, tn):
    j = pl.program_id(1)
    nq = D_QK // tn

    @pl.when(j == 0)
    def _():
        h_scr[...] = _rms_rows(x_ref[...], g_ref[...]).astype(BF16)

    acc = jnp.dot(h_scr[...], w_ref[...], preferred_element_type=F32)

    def head_norm(gain_ref, scale):
        outs = []
        for c in range(tn // MXU_DIM):
            a = acc[:, c * MXU_DIM:(c + 1) * MXU_DIM]
            ss = jnp.dot((a * a).astype(BF16), ones_ref[...], preferred_element_type=F32)
            outs.append(a * lax.rsqrt(ss * (1.0 / HEAD_DIM) + EPS) * (gain_ref[...] * scale))
        return jnp.concatenate(outs, axis=1)

    @pl.when(j < nq)
    def _():
        q_ref[...] = head_norm(qg_ref, HEAD_DIM ** -0.5).astype(BF16)

    @pl.when((j >= nq) & (j < 2 * nq))
    def _():
        k_ref[...] = head_norm(kg_ref, 1.0).astype(BF16)

    @pl.when((j >= 2 * nq) & (j < 3 * nq))
    def _():
        v_ref[...] = acc.astype(BF16)

    @pl.when((j >= 3 * nq) & (j < 4 * nq))
    def _():
        u_ref[...] = acc.astype(BF16)

    @pl.when(j >= 4 * nq)
    def _():
        gate_ref[...] = jax.nn.sigmoid(acc).astype(BF16)


def _in_proj(xf, g, w_bf, qg, kg, batch, seq, *, tm=512, tn=512):
    t = xf.shape[0]
    tm = min(tm, seq)
    nsb = seq // tm
    nq = D_QK // tn
    ng = 2 * D_MODEL // tn
    qg_t = jnp.tile(qg.astype(F32), MXU_DIM // HEAD_DIM).reshape(1, MXU_DIM)
    kg_t = jnp.tile(kg.astype(F32), MXU_DIM // HEAD_DIM).reshape(1, MXU_DIM)
    seg = jnp.arange(MXU_DIM) // HEAD_DIM
    ones_bd = (seg[:, None] == seg[None, :]).astype(BF16)

    def cl(j, lo, n):
        return jnp.clip(j - lo, 0, n - 1)

    const = lambda i, j: (0, 0)
    return pl.pallas_call(
        functools.partial(_in_proj_body, tn=tn),
        grid=(t // tm, IN_COLS // tn),
        in_specs=[
            pl.BlockSpec((tm, D_MODEL), lambda i, j: (i, 0)),
            pl.BlockSpec((1, D_MODEL), const),
            pl.BlockSpec((D_MODEL, tn), lambda i, j: (0, j)),
            pl.BlockSpec((1, MXU_DIM), const),
            pl.BlockSpec((1, MXU_DIM), const),
            pl.BlockSpec((MXU_DIM, MXU_DIM), const),
        ],
        out_specs=[
            pl.BlockSpec((tm, tn), lambda i, j: (i, cl(j, 0, nq))),
            pl.BlockSpec((tm, tn), lambda i, j: (i, cl(j, nq, nq))),
            pl.BlockSpec((tm, tn), lambda i, j: (i, cl(j, 2 * nq, nq))),
            pl.BlockSpec((tm, tn), lambda i, j: (i % nsb, (i // nsb) * nq + cl(j, 3 * nq, nq))),
            pl.BlockSpec((tm, tn), lambda i, j: (i, cl(j, 4 * nq, ng))),
        ],
        out_shape=[
            jax.ShapeDtypeStruct((t, D_QK), BF16),
            jax.ShapeDtypeStruct((t, D_QK), BF16),
            jax.ShapeDtypeStruct((t, D_V), BF16),
            jax.ShapeDtypeStruct((seq, batch * D_SSM), BF16),
            jax.ShapeDtypeStruct((t, 2 * D_MODEL), BF16),
        ],
        scratch_shapes=[pltpu.VMEM((tm, D_MODEL), BF16)],
        compiler_params=pltpu.CompilerParams(
            dimension_semantics=("arbitrary", "arbitrary"),
            vmem_limit_bytes=VMEM_LIMIT),
        name="in_proj",
    )(xf, g.reshape(1, D_MODEL), w_bf, qg_t, kg_t, ones_bd)


def _attn_body(slopes_ref, laminit_ref, q_ref, k_ref, v_ref, lq1_ref, lk1_ref, lq2_ref, lk2_ref,
               subg_ref, o_ref, *, seq, tq):
    tk = tq
    h = pl.program_id(1)
    slope = slopes_ref[h]
    lam_init = laminit_ref[0]
    lam = (jnp.exp(jnp.sum(lq1_ref[...] * lk1_ref[...], axis=-1, keepdims=True))
           - jnp.exp(jnp.sum(lq2_ref[...] * lk2_ref[...], axis=-1, keepdims=True))
           + lam_init)

    r = lax.broadcasted_iota(jnp.int32, (tq, tk), 0)
    c = lax.broadcasted_iota(jnp.int32, (tq, tk), 1)
    dmat = jnp.where((c // CHUNK) <= (r // CHUNK),
                     slope * jnp.minimum(c, 2 * r - c).astype(F32), NEG_BIG)
    dmat2 = jnp.concatenate([dmat, dmat], axis=0)
    cb0 = slope * (lax.broadcasted_iota(jnp.int32, (1, tk), 1) - tk).astype(F32)
    lane = lax.broadcasted_iota(jnp.int32, (tq, 2 * HEAD_DIM), 1)
    nt = (((1,), (1,)), ((), ()))
    subg = subg_ref[...] * (1.0 - lam_init)

    def q_block(iq, _):
        q0 = pl.multiple_of(iq * tq, tq)
        qs = q_ref[pl.ds(q0, tq), :]
        zero = jnp.zeros_like(qs)
        qz = jnp.concatenate([jnp.where(lane < HEAD_DIM, qs, zero),
                              jnp.where(lane >= HEAD_DIM, qs, zero)], axis=0)
        s = lax.dot_general(qz, k_ref[pl.ds(q0, tk), :], nt, preferred_element_type=F32) + dmat2
        m = jnp.max(s, axis=-1, keepdims=True)
        p = jnp.exp(s - m)
        l = jnp.sum(p, axis=-1, keepdims=True)
        acc = jnp.dot(p.astype(BF16), v_ref[pl.ds(q0, tk), :], preferred_element_type=F32)

        def kv_block(j, carry):
            m, l, acc = carry
            k0 = pl.multiple_of(j * tk, tk)
            cb = cb0 + slope * ((j + 1 - iq) * tk).astype(F32)
            s = lax.dot_general(qz, k_ref[pl.ds(k0, tk), :], nt, preferred_element_type=F32) + cb
            m_new = jnp.maximum(m, jnp.max(s, axis=-1, keepdims=True))
            alpha = jnp.exp(m - m_new)
            p = jnp.exp(s - m_new)
            l = alpha * l + jnp.sum(p, axis=-1, keepdims=True)
            acc = alpha * acc + jnp.dot(p.astype(BF16), v_ref[pl.ds(k0, tk), :],
                                        preferred_element_type=F32)
            return m_new, l, acc

        m, l, acc = lax.fori_loop(0, iq, kv_block, (m, l, acc))
        o = acc[:tq] / l[:tq] - acc[tq:] * (lam / l[tq:])
        o = o * lax.rsqrt(jnp.mean(o * o, axis=-1, keepdims=True) + EPS) * subg
        o_ref[pl.ds(q0, tq), :] = o.astype(BF16)
        return 0

    lax.fori_loop(0, seq // tq, q_block, 0)


def _attention(q, k, v, lq1, lk1, lq2, lk2, subg, lam_init, batch, seq, *, tq=256):
    slopes = 2.0 ** (-8.0 * jnp.arange(1, N_HEADS + 1, dtype=F32) / N_HEADS)
    lam_arr = jnp.full((1,), lam_init, F32)
    smem = pl.BlockSpec(memory_space=pltpu.SMEM)
    head_blk = pl.BlockSpec((seq, V_HEAD_DIM), lambda b, h: (b, h))
    vec = lambda n: pl.BlockSpec((1, n), lambda b, h: (0, 0))
    row = lambda a: a.astype(F32).reshape(1, -1)
    return pl.pallas_call(
        functools.partial(_attn_body, seq=seq, tq=tq),
        grid=(batch, N_HEADS),
        in_specs=[smem, smem, head_blk, head_blk, head_blk,
                  vec(HEAD_DIM), vec(HEAD_DIM), vec(HEAD_DIM), vec(HEAD_DIM), vec(V_HEAD_DIM)],
        out_specs=head_blk,
        out_shape=jax.ShapeDtypeStruct((batch * seq, D_V), BF16),
        compiler_params=pltpu.CompilerParams(
            dimension_semantics=("parallel", "parallel"),
            vmem_limit_bytes=VMEM_LIMIT),
        name="diff_attention",
    )(slopes, lam_arr, q, k, v, row(lq1), row(lk1), row(lq2), row(lk2), row(subg))


def _ssm_body(u_ref, bblk_ref, are_ref, aim_ref, cblk_ref, d_ref, y_ref,
              x_scr, h_scr, st_scr, *, batch, tt):
    nc = SSM_NC

    @pl.when(pl.program_id(1) == 0)
    def _():
        st_scr[...] = jnp.zeros_like(st_scr)

    u = u_ref[...]
    x_scr[...] = jnp.dot(u, bblk_ref[...], preferred_element_type=F32)
    ar = jnp.broadcast_to(are_ref[...], (batch, nc))
    ai = jnp.broadcast_to(aim_ref[...], (batch, nc))

    def step(t, carry):
        hr, hi = carry
        r0 = pl.multiple_of(t * batch, batch)
        xr = x_scr[pl.ds(r0, batch), :nc]
        xi = x_scr[pl.ds(r0, batch), nc:]
        nr = ar * hr - ai * hi + xr
        ni = ar * hi + ai * hr + xi
        h_scr[pl.ds(r0, batch), :nc] = nr.astype(BF16)
        h_scr[pl.ds(r0, batch), nc:] = ni.astype(BF16)
        return nr, ni

    hr, hi = lax.fori_loop(0, tt, step, (st_scr[:, :nc], st_scr[:, nc:]), unroll=4)
    st_scr[:, :nc] = hr
    st_scr[:, nc:] = hi
    y = jnp.dot(h_scr[...], cblk_ref[...], preferred_element_type=F32)
    y = y + d_ref[...] * u.astype(F32)
    y_ref[...] = _gelu_tanh(y).astype(BF16)


def _ssm_params(a_re, a_im, log_dt, b_re, b_im, c_re, c_im):
    gpt = GROUPS_PER_TILE
    ngt = N_GROUPS // gpt
    dt = jnp.exp(log_dt)[:, None]
    mag = jnp.exp(dt * a_re)
    ab_re = mag * jnp.cos(dt * a_im)
    ab_im = mag * jnp.sin(dt * a_im)
    den = a_re * a_re + a_im * a_im
    zr = ab_re - 1.0
    zi = ab_im
    f_re = (zr * a_re + zi * a_im) / den
    f_im = (zi * a_re - zr * a_im) / den
    bb_re = f_re[..., None] * b_re - f_im[..., None] * b_im
    bb_im = f_re[..., None] * b_im + f_im[..., None] * b_re
    eye = jnp.eye(gpt, dtype=F32)

    def bdiag_in(bb):
        t = bb.reshape(ngt, gpt, SSM_STATE, SSM_GROUP)
        return jnp.einsum('tgpi,gh->tgihp', t, eye).reshape(ngt, gpt * SSM_GROUP, SSM_NC)

    def bdiag_out(cc):
        t = cc.reshape(ngt, gpt, SSM_GROUP, SSM_STATE)
        return jnp.einsum('tgop,gh->tgpho', t, eye).reshape(ngt, SSM_NC, gpt * SSM_GROUP)

    bblk = jnp.concatenate([bdiag_in(bb_re), bdiag_in(bb_im)], axis=-1).astype(BF16)
    cblk = jnp.concatenate([bdiag_out(c_re), -bdiag_out(c_im)], axis=1).astype(BF16)
    return bblk, ab_re.reshape(ngt, 1, SSM_NC), ab_im.reshape(ngt, 1, SSM_NC), cblk


def _ssm(u_tm, bblk, are, aim, cblk, d_skip, batch, seq, *, tt=128):
    gcols = GROUPS_PER_TILE * SSM_GROUP
    ngt = N_GROUPS // GROUPS_PER_TILE
    tt = min(tt, seq)
    rows = tt * batch
    return pl.pallas_call(
        functools.partial(_ssm_body, batch=batch, tt=tt),
        grid=(ngt, seq // tt),
        in_specs=[
            pl.BlockSpec((rows, gcols), lambda g, t: (t, g)),
            pl.BlockSpec((None, gcols, 2 * SSM_NC), lambda g, t: (g, 0, 0)),
            pl.BlockSpec((None, 1, SSM_NC), lambda g, t: (g, 0, 0)),
            pl.BlockSpec((None, 1, SSM_NC), lambda g, t: (g, 0, 0)),
            pl.BlockSpec((None, 2 * SSM_NC, gcols), lambda g, t: (g, 0, 0)),
            pl.BlockSpec((1, gcols), lambda g, t: (0, g)),
        ],
        out_specs=pl.BlockSpec((rows, gcols), lambda g, t: (t, g)),
        out_shape=jax.ShapeDtypeStruct((seq * batch, D_SSM), BF16),
        scratch_shapes=[pltpu.VMEM((rows, 2 * SSM_NC), F32),
                        pltpu.VMEM((rows, 2 * SSM_NC), BF16),
                        pltpu.VMEM((batch, 2 * SSM_NC), F32)],
        compiler_params=pltpu.CompilerParams(
            dimension_semantics=("parallel", "arbitrary"),
            vmem_limit_bytes=VMEM_LIMIT),
        name="s5_scan",
    )(u_tm, bblk, are, aim, cblk, d_skip.astype(F32).reshape(1, D_SSM))


def _mix_body(y_ref, oa_ref, gate_ref, x_ref, gw_ref, gb_ref, wba_ref, wbs_ref, wo_ref, out_ref):
    y = y_ref[...]
    z = jnp.dot(y, gw_ref[...], preferred_element_type=F32) + gb_ref[...]
    y2 = (y.astype(F32) * jax.nn.sigmoid(z)).astype(BF16)
    o_ssm = jnp.dot(y2, wbs_ref[...], preferred_element_type=F32)
    o_att = jnp.dot(oa_ref[...], wba_ref[...], preferred_element_type=F32)
    mixed = (gate_ref[:, :D_MODEL].astype(F32) * o_att
             + gate_ref[:, D_MODEL:].astype(F32) * o_ssm)
    out_ref[...] = x_ref[...] + jnp.dot(mixed.astype(BF16), wo_ref[...],
                                        preferred_element_type=F32)


def _mix(y_tm, o_att, gates, xf, glu_w, glu_b, w_ba, w_bs, w_out, batch, seq, *, tm=256):
    t = xf.shape[0]
    tm = min(tm, seq)
    nsb = seq // tm
    full = lambda shape: pl.BlockSpec(shape, lambda i: (0, 0), pipeline_mode=pl.Buffered(1))
    return pl.pallas_call(
        _mix_body,
        grid=(t // tm,),
        in_specs=[
            pl.BlockSpec((tm, D_SSM), lambda i: (i % nsb, i // nsb)),
            pl.BlockSpec((tm, D_V), lambda i: (i, 0)),
            pl.BlockSpec((tm, 2 * D_MODEL), lambda i: (i, 0)),
            pl.BlockSpec((tm, D_MODEL), lambda i: (i, 0)),
            full((D_SSM, D_SSM)), full((1, D_SSM)),
            full((D_V, D_MODEL)), full((D_SSM, D_MODEL)), full((D_MODEL, D_MODEL)),
        ],
        out_specs=pl.BlockSpec((tm, D_MODEL), lambda i: (i, 0)),
        out_shape=jax.ShapeDtypeStruct((t, D_MODEL), F32),
        compiler_params=pltpu.CompilerParams(
            dimension_semantics=("parallel",),
            vmem_limit_bytes=VMEM_LIMIT),
        name="gated_merge",
    )(y_tm, o_att, gates, xf, glu_w, glu_b.astype(F32).reshape(1, D_SSM), w_ba, w_bs, w_out)


def _shift_rows(up, prev, k):
    body = pltpu.roll(up, k, axis=0)
    top = pltpu.roll(jnp.concatenate([prev, up[:8]], axis=0), k, axis=0)[8:]
    return jnp.concatenate([top, body[8:]], axis=0)


def _ffn_up_body(x_ref, g_ref, wa_ref, wv_ref, cwa_ref, cwv_ref, cba_ref, cbv_ref,
                 act_ref, h_scr, carry_a, carry_v, *, nsb):
    i = pl.program_id(0)
    j = pl.program_id(1)

    @pl.when(j == 0)
    def _():
        h_scr[...] = _rms_rows(x_ref[...], g_ref[...]).astype(BF16)

    h = h_scr[...]
    seq_start = i % nsb == 0

    def conv(w_ref, cw_ref, cb_ref, carry):
        up = jnp.dot(h, w_ref[...], preferred_element_type=F32)
        prev = jnp.where(seq_start, 0.0, carry[j])
        carry[j] = up[up.shape[0] - 8:]
        cw = cw_ref[...]
        return (cb_ref[...] + cw[0:1] * _shift_rows(up, prev, 2)
                + cw[1:2] * _shift_rows(up, prev, 1) + cw[2:3] * up)

    a = conv(wa_ref, cwa_ref, cba_ref, carry_a)
    val = conv(wv_ref, cwv_ref, cbv_ref, carry_v)
    act_ref[...] = (_gelu_tanh(a) * val).astype(BF16)


def _ffn_up(xf, g, wa, wv, cwa, cwv, cba, cbv, seq, *, tm=512, tn=FF_TILE):
    t = xf.shape[0]
    tm = min(tm, seq)
    ncol = D_FF_PAD // tn
    const = lambda i, j: (0, 0)
    col = lambda i, j: (0, j)
    return pl.pallas_call(
        functools.partial(_ffn_up_body, nsb=seq // tm),
        grid=(t // tm, ncol),
        in_specs=[
            pl.BlockSpec((tm, D_MODEL), lambda i, j: (i, 0)),
            pl.BlockSpec((1, D_MODEL), const),
            pl.BlockSpec((D_MODEL, tn), col),
            pl.BlockSpec((D_MODEL, tn), col),
            pl.BlockSpec((CONV_W, tn), col),
            pl.BlockSpec((CONV_W, tn), col),
            pl.BlockSpec((1, tn), col),
            pl.BlockSpec((1, tn), col),
        ],
        out_specs=pl.BlockSpec((tm, tn), lambda i, j: (i, j)),
        out_shape=jax.ShapeDtypeStruct((t, D_FF_PAD), BF16),
        scratch_shapes=[pltpu.VMEM((tm, D_MODEL), BF16),
                        pltpu.VMEM((ncol, 8, tn), F32),
                        pltpu.VMEM((ncol, 8, tn), F32)],
        compiler_params=pltpu.CompilerParams(
            dimension_semantics=("arbitrary", "arbitrary"),
            vmem_limit_bytes=VMEM_LIMIT),
        name="ffn_up_conv_gate",
    )(xf, g.reshape(1, D_MODEL), wa, wv, cwa, cwv, cba, cbv)


def _ffn_down_body(act_ref, w_ref, x_ref, out_ref):
    out_ref[...] = x_ref[...] + jnp.dot(act_ref[...], w_ref[...], preferred_element_type=F32)


def _ffn_down(act, w_down, xf, *, tm=1024, tn=512):
    t = xf.shape[0]
    tm = min(tm, t)
    return pl.pallas_call(
        _ffn_down_body,
        grid=(t // tm, D_MODEL // tn),
        in_specs=[
            pl.BlockSpec((tm, D_FF_PAD), lambda i, j: (i, 0)),
            pl.BlockSpec((D_FF_PAD, tn), lambda i, j: (0, j)),
            pl.BlockSpec((tm, tn), lambda i, j: (i, j)),
        ],
        out_specs=pl.BlockSpec((tm, tn), lambda i, j: (i, j)),
        out_shape=jax.ShapeDtypeStruct((t, D_MODEL), F32),
        compiler_params=pltpu.CompilerParams(
            dimension_semantics=("parallel", "parallel"),
            vmem_limit_bytes=VMEM_LIMIT),
        name="ffn_down",
    )(act, w_down, xf)


def _pad_ff_cols(a):
    return jnp.pad(a, ((0, 0), (0, D_FF_PAD - D_FF)))


def kernel(x, norm1_g, w_in, q_norm_g, k_norm_g, lambda_q1, lambda_k1, lambda_q2, lambda_k2, subln_g, ssm_a_re, ssm_a_im, ssm_log_dt, ssm_b_re, ssm_b_im, ssm_c_re, ssm_c_im, ssm_d, ssm_glu_w, ssm_glu_b, w_branch_attn, w_branch_ssm, w_out, norm2_g, ffn_w_up, ffn_conv_w, ffn_conv_b, ffn_w_down):
    batch, seq, _ = x.shape
    depth = w_in.shape[0]
    xf = x.reshape(batch * seq, D_MODEL)
    for l in range(depth):
        lam_init = 0.8 - 0.6 * math.exp(-0.3 * l)
        q, k, v, u_tm, gates = _in_proj(xf, norm1_g[l], w_in[l].astype(BF16),
                                        q_norm_g[l], k_norm_g[l], batch, seq)
        o_att = _attention(q, k, v, lambda_q1[l], lambda_k1[l], lambda_q2[l], lambda_k2[l],
                           subln_g[l], lam_init, batch, seq)
        bblk, are, aim, cblk = _ssm_params(ssm_a_re[l], ssm_a_im[l], ssm_log_dt[l],
                                           ssm_b_re[l], ssm_b_im[l], ssm_c_re[l], ssm_c_im[l])
        y_tm = _ssm(u_tm.reshape(seq * batch, D_SSM), bblk, are, aim, cblk, ssm_d[l], batch, seq)
        xf = _mix(y_tm.reshape(seq, batch * D_SSM), o_att, gates, xf,
                  ssm_glu_w[l].astype(BF16), ssm_glu_b[l],
                  w_branch_attn[l].astype(BF16), w_branch_ssm[l].astype(BF16),
                  w_out[l].astype(BF16), batch, seq)
        wu = ffn_w_up[l]
        cw = ffn_conv_w[l]
        cb = ffn_conv_b[l].reshape(1, 2 * D_FF)
        act = _ffn_up(xf, norm2_g[l],
                      _pad_ff_cols(wu[:, :D_FF]).astype(BF16), _pad_ff_cols(wu[:, D_FF:]).astype(BF16),
                      _pad_ff_cols(cw[:, :D_FF]), _pad_ff_cols(cw[:, D_FF:]),
                      _pad_ff_cols(cb[:, :D_FF]), _pad_ff_cols(cb[:, D_FF:]), seq)
        w_down = jnp.pad(ffn_w_down[l], ((0, D_FF_PAD - D_FF), (0, 0))).astype(BF16)
        xf = _ffn_down(act, w_down, xf)
    return xf.reshape(batch, seq, D_MODEL)
```

```python
import functools
import math

import jax
import jax.numpy as jnp
from jax import lax
from jax.experimental import pallas as pl
from jax.experimental.pallas import tpu as pltpu

F32 = jnp.float32
BF16 = jnp.bfloat16

D_MODEL = 2048
N_HEADS = 8
HEAD_DIM = 64
V_HEAD_DIM = 2 * HEAD_DIM
D_QK = N_HEADS * 2 * HEAD_DIM
D_V = N_HEADS * V_HEAD_DIM
D_SSM = D_MODEL // 2
SSM_GROUP = 16
N_GROUPS = D_SSM // SSM_GROUP
SSM_STATE = 64
D_FF = 5504
CONV_W = 3
CHUNK = 64
EPS = 1e-6
IN_COLS = 2 * D_QK + D_V + D_SSM + 2 * D_MODEL

MXU_DIM = 256
FF_TILE = 512
D_FF_PAD = ((D_FF + FF_TILE - 1) // FF_TILE) * FF_TILE
GROUPS_PER_TILE = 8
SSM_NC = GROUPS_PER_TILE * SSM_STATE
NEG_BIG = -1e30
Q_SCALE = HEAD_DIM ** -0.5 * math.log2(math.e)
VMEM_LIMIT = 56 * 1024 * 1024


def _gelu_tanh(x):
    c = math.sqrt(2.0 / math.pi)
    return x * (0.5 * (1.0 + jnp.tanh(c * (x + 0.044715 * (x * x * x)))))


def _rms_rows(xf, g):
    ms = jnp.mean(xf * xf, axis=-1, keepdims=True)
    return xf * lax.rsqrt(ms + EPS) * g


def _in_proj_body(x_ref, g_ref, w_ref, qg_ref, kg_ref, ones_ref,
                  q_ref, k_ref, v_ref, u_ref, gate_ref, h_scr, *, tn):
    j = pl.program_id(1)
    nq = D_QK // tn

    @pl.when(j == 0)
    def _():
        h_scr[...] = _rms_rows(x_ref[...], g_ref[...]).astype(BF16)

    acc = jnp.dot(h_scr[...], w_ref[...], preferred_element_type=F32)

    def head_norm(gain_ref, scale):
        outs = []
        for c in range(tn // MXU_DIM):
            a = acc[:, c * MXU_DIM:(c + 1) * MXU_DIM]
            ss = jnp.dot((a * a).astype(BF16), ones_ref[...], preferred_element_type=F32)
            outs.append(a * lax.rsqrt(ss * (1.0 / HEAD_DIM) + EPS) * (gain_ref[...] * scale))
        return jnp.concatenate(outs, axis=1)

    @pl.when(j < nq)
    def _():
        q_ref[...] = head_norm(qg_ref, Q_SCALE).astype(BF16)

    @pl.when((j >= nq) & (j < 2 * nq))
    def _():
        k_ref[...] = head_norm(kg_ref, 1.0).astype(BF16)

    @pl.when((j >= 2 * nq) & (j < 3 * nq))
    def _():
        v_ref[...] = acc.astype(BF16)

    @pl.when((j >= 3 * nq) & (j < 4 * nq))
    def _():
        u_ref[...] = acc.astype(BF16)

    @pl.when(j >= 4 * nq)
    def _():
        gate_ref[...] = acc.astype(BF16)


def _in_proj(xf, g, w_bf, qg, kg, batch, seq, *, tm=1024, tn=512):
    t = xf.shape[0]
    tm = min(tm, seq)
    nsb = seq // tm
    nq = D_QK // tn
    ng = 2 * D_MODEL // tn
    qg_t = jnp.tile(qg.astype(F32), MXU_DIM // HEAD_DIM).reshape(1, MXU_DIM)
    kg_t = jnp.tile(kg.astype(F32), MXU_DIM // HEAD_DIM).reshape(1, MXU_DIM)
    seg = jnp.arange(MXU_DIM) // HEAD_DIM
    ones_bd = (seg[:, None] == seg[None, :]).astype(BF16)

    def cl(j, lo, n):
        return jnp.clip(j - lo, 0, n - 1)

    const = lambda i, j: (0, 0)
    return pl.pallas_call(
        functools.partial(_in_proj_body, tn=tn),
        grid=(t // tm, IN_COLS // tn),
        in_specs=[
            pl.BlockSpec((tm, D_MODEL), lambda i, j: (i, 0)),
            pl.BlockSpec((1, D_MODEL), const),
            pl.BlockSpec((D_MODEL, tn), lambda i, j: (0, j)),
            pl.BlockSpec((1, MXU_DIM), const),
            pl.BlockSpec((1, MXU_DIM), const),
            pl.BlockSpec((MXU_DIM, MXU_DIM), const),
        ],
        out_specs=[
            pl.BlockSpec((tm, tn), lambda i, j: (i, cl(j, 0, nq))),
            pl.BlockSpec((tm, tn), lambda i, j: (i, cl(j, nq, nq))),
            pl.BlockSpec((tm, tn), lambda i, j: (i, cl(j, 2 * nq, nq))),
            pl.BlockSpec((tm, tn), lambda i, j: (i % nsb, (i // nsb) * nq + cl(j, 3 * nq, nq))),
            pl.BlockSpec((tm, tn), lambda i, j: (i, cl(j, 4 * nq, ng))),
        ],
        out_shape=[
            jax.ShapeDtypeStruct((t, D_QK), BF16),
            jax.ShapeDtypeStruct((t, D_QK), BF16),
            jax.ShapeDtypeStruct((t, D_V), BF16),
            jax.ShapeDtypeStruct((seq, batch * D_SSM), BF16),
            jax.ShapeDtypeStruct((t, 2 * D_MODEL), BF16),
        ],
        scratch_shapes=[pltpu.VMEM((tm, D_MODEL), BF16)],
        compiler_params=pltpu.CompilerParams(
            dimension_semantics=("arbitrary", "arbitrary"),
            vmem_limit_bytes=VMEM_LIMIT),
        name="in_proj",
    )(xf, g.reshape(1, D_MODEL), w_bf, qg_t, kg_t, ones_bd)


def _attn_body(slopes_ref, laminit_ref, q_ref, k_ref, v_ref, lq1_ref, lk1_ref, lq2_ref, lk2_ref,
               subg_ref, o_ref, *, seq, tq):
    tk = tq
    h = pl.program_id(1)
    slope = slopes_ref[h]
    lam_init = laminit_ref[0]
    lam = (jnp.exp(jnp.sum(lq1_ref[...] * lk1_ref[...], axis=-1, keepdims=True))
           - jnp.exp(jnp.sum(lq2_ref[...] * lk2_ref[...], axis=-1, keepdims=True))
           + lam_init)

    r = lax.broadcasted_iota(jnp.int32, (tq, tk), 0)
    c = lax.broadcasted_iota(jnp.int32, (tq, tk), 1)
    dmat = jnp.where((c // CHUNK) <= (r // CHUNK),
                     slope * jnp.minimum(c, 2 * r - c).astype(F32), NEG_BIG)
    dmat2 = jnp.concatenate([dmat, dmat], axis=0)
    cb0 = slope * (lax.broadcasted_iota(jnp.int32, (1, tk), 1) - tk).astype(F32)
    lane = lax.broadcasted_iota(jnp.int32, (tq, 2 * HEAD_DIM), 1)
    nt = (((1,), (1,)), ((), ()))
    subg = subg_ref[...] * (1.0 - lam_init)

    def q_block(iq, _):
        q0 = pl.multiple_of(iq * tq, tq)
        qs = q_ref[pl.ds(q0, tq), :]
        zero = jnp.zeros_like(qs)
        qz = jnp.concatenate([jnp.where(lane < HEAD_DIM, qs, zero),
                              jnp.where(lane >= HEAD_DIM, qs, zero)], axis=0)
        s = lax.dot_general(qz, k_ref[pl.ds(q0, tk), :], nt, preferred_element_type=F32) + dmat2
        m = jnp.max(s, axis=-1, keepdims=True)
        p = jnp.exp(s - m)
        l = jnp.sum(p, axis=-1, keepdims=True)
        acc = jnp.dot(p.astype(BF16), v_ref[pl.ds(q0, tk), :], preferred_element_type=F32)

        def kv_block(j, carry):
            m, l, acc = carry
            k0 = pl.multiple_of(j * tk, tk)
            cb = cb0 + slope * ((j + 1 - iq) * tk).astype(F32)
            s = lax.dot_general(qz, k_ref[pl.ds(k0, tk), :], nt, preferred_element_type=F32) + cb
            m_new = jnp.maximum(m, jnp.max(s, axis=-1, keepdims=True))
            alpha = jnp.exp(m - m_new)
            p = jnp.exp(s - m_new)
            l = alpha * l + jnp.sum(p, axis=-1, keepdims=True)
            acc = alpha * acc + jnp.dot(p.astype(BF16), v_ref[pl.ds(k0, tk), :],
                                        preferred_element_type=F32)
            return m_new, l, acc

        m, l, acc = lax.fori_loop(0, iq, kv_block, (m, l, acc))
        o = acc[:tq] / l[:tq] - acc[tq:] * (lam / l[tq:])
        o = o * lax.rsqrt(jnp.mean(o * o, axis=-1, keepdims=True) + EPS) * subg
        o_ref[pl.ds(q0, tq), :] = o.astype(BF16)
        return 0

    lax.fori_loop(0, seq // tq, q_block, 0)


def _attention(q, k, v, lq1, lk1, lq2, lk2, subg, lam_init, batch, seq, *, tq=256):
    slopes = 2.0 ** (-8.0 * jnp.arange(1, N_HEADS + 1, dtype=F32) / N_HEADS)
    lam_arr = jnp.full((1,), lam_init, F32)
    smem = pl.BlockSpec(memory_space=pltpu.SMEM)
    head_blk = pl.BlockSpec((seq, V_HEAD_DIM), lambda b, h: (b, h))
    vec = lambda n: pl.BlockSpec((1, n), lambda b, h: (0, 0))
    row = lambda a: a.astype(F32).reshape(1, -1)
    return pl.pallas_call(
        functools.partial(_attn_body, seq=seq, tq=tq),
        grid=(batch, N_HEADS),
        in_specs=[smem, smem, head_blk, head_blk, head_blk,
                  vec(HEAD_DIM), vec(HEAD_DIM), vec(HEAD_DIM), vec(HEAD_DIM), vec(V_HEAD_DIM)],
        out_specs=head_blk,
        out_shape=jax.ShapeDtypeStruct((batch * seq, D_V), BF16),
        compiler_params=pltpu.CompilerParams(
            dimension_semantics=("parallel", "parallel"),
            vmem_limit_bytes=VMEM_LIMIT),
        name="diff_attention",
    )(slopes, lam_arr, q, k, v, row(lq1), row(lk1), row(lq2), row(lk2), row(subg))


N_BIAS_ROWS = 3


def _attn2_body(sl_ref, laminit_ref, q_ref, k_ref, v_ref, lq1_ref, lk1_ref, lq2_ref, lk2_ref,
                subg_ref, o_ref, kaug_scr, vt_scr, dmat_scr, *, seq, tq, hpb):
    tk = tq
    nblk = seq // tk
    hd2 = 2 * HEAD_DIM
    hg = pl.program_id(1)
    lam_init = laminit_ref[0]
    lam = (jnp.exp(jnp.sum(lq1_ref[...] * lk1_ref[...], axis=-1, keepdims=True))
           - jnp.exp(jnp.sum(lq2_ref[...] * lk2_ref[...], axis=-1, keepdims=True))
           + lam_init)
    subg = subg_ref[...] * (1.0 - lam_init)

    koff = lax.broadcasted_iota(jnp.int32, (seq, hd2), 0) % tk
    klane = lax.broadcasted_iota(jnp.int32, (seq, hd2), 1)
    kbias = jnp.where(klane < N_BIAS_ROWS, koff, 0).astype(F32).astype(BF16)
    c = lax.broadcasted_iota(jnp.int32, (tk, tq), 0)
    r = lax.broadcasted_iota(jnp.int32, (tk, tq), 1)
    visible = (c // CHUNK) <= (r // CHUNK)
    ahead = jnp.maximum(c - r, 0).astype(F32)
    brow_i = lax.broadcasted_iota(jnp.int32, (hd2, tq), 0)
    sl2 = []
    brows = []
    for hh in range(hpb):
        h = hg * hpb + hh
        s_hi, s_mid, s_lo = sl_ref[h, 0], sl_ref[h, 1], sl_ref[h, 2]
        sl2.append(s_hi + s_mid + s_lo)
        brows.append(jnp.where(brow_i == 0, s_hi, jnp.where(brow_i == 1, s_mid,
                     jnp.where(brow_i == 2, s_lo, 0.0))).astype(BF16))
        kaug_scr[hh, :, 0:hd2] = k_ref[:, hh * hd2:(hh + 1) * hd2]
        kaug_scr[hh, :, hd2:2 * hd2] = kbias
        for blk in range(nblk):
            vt_scr[hh, blk] = v_ref[blk * tk:(blk + 1) * tk,
                                    hh * hd2:(hh + 1) * hd2].astype(F32).T.astype(BF16)
        dmat_scr[hh] = jnp.where(visible, (-2.0 * sl2[hh]) * ahead, NEG_BIG)

    def q_block(iq, _):
        q0 = pl.multiple_of(iq * tq, tq)
        nch = 2 * hpb
        qas = []
        for hh in range(hpb):
            qt = q_ref[pl.ds(q0, tq), hh * hd2:(hh + 1) * hd2].astype(F32).T
            for comp in range(2):
                keep = (brow_i < HEAD_DIM) if comp == 0 else (brow_i >= HEAD_DIM)
                qas.append(jnp.concatenate([jnp.where(keep, qt, 0.0).astype(BF16), brows[hh]],
                                           axis=0))
        ss = [jnp.dot(kaug_scr[idx // 2, pl.ds(q0, tk), :], qas[idx],
                      preferred_element_type=F32) + dmat_scr[idx // 2] for idx in range(nch)]
        ps, stats = [], []
        for idx in range(nch):
            m = jnp.max(ss[idx], axis=0, keepdims=True)
            p = jnp.exp2(ss[idx] - m)
            stats.append((m, jnp.sum(p, axis=0, keepdims=True)))
            ps.append(p.astype(BF16))
        state = []
        for idx in range(nch):
            acc = jnp.dot(vt_scr[idx // 2, iq], ps[idx], preferred_element_type=F32)
            state.extend([stats[idx][0], stats[idx][1], acc])

        def kv_block(j, carry):
            carry = list(carry)
            k0 = pl.multiple_of(j * tk, tk)
            boff = ((j - iq) * tk).astype(F32)
            nch = 2 * hpb
            ss = [jnp.dot(kaug_scr[idx // 2, pl.ds(k0, tk), :], qas[idx],
                          preferred_element_type=F32) for idx in range(nch)]
            ps, stats = [], []
            for idx in range(nch):
                m, l, _ = carry[3 * idx:3 * idx + 3]
                bc = sl2[idx // 2] * boff
                m_new = jnp.maximum(m, jnp.max(ss[idx], axis=0, keepdims=True) + bc)
                alpha = jnp.exp2(m - m_new)
                p = jnp.exp2(ss[idx] - (m_new - bc))
                stats.append((m_new, alpha * l + jnp.sum(p, axis=0, keepdims=True), alpha))
                ps.append(p.astype(BF16))
            out = []
            for idx in range(nch):
                m_new, l_new, alpha = stats[idx]
                acc = alpha * carry[3 * idx + 2] + jnp.dot(vt_scr[idx // 2, j], ps[idx],
                                                           preferred_element_type=F32)
                out.extend([m_new, l_new, acc])
            return tuple(out)

        state = lax.fori_loop(0, iq, kv_block, tuple(state))
        for hh in range(hpb):
            _, l0, a0, _, l1, a1 = state[6 * hh:6 * hh + 6]
            ot = a0 * (1.0 / l0) - a1 * (lam / l1)
            ot = ot * lax.rsqrt(jnp.mean(ot * ot, axis=0, keepdims=True) + EPS)
            o_ref[pl.ds(q0, tq), hh * hd2:(hh + 1) * hd2] = (ot.T * subg).astype(BF16)
        return 0

    lax.fori_loop(0, seq // tq, q_block, 0)


def _attention2(q, k, v, lq1, lk1, lq2, lk2, subg, lam_init, batch, seq, *, tq=256, hpb=4):
    slopes = 2.0 ** (-8.0 * jnp.arange(1, N_HEADS + 1, dtype=F32) / N_HEADS)
    sl = slopes * math.log2(math.e)
    s_hi = sl.astype(BF16).astype(F32)
    s_mid = (sl - s_hi).astype(BF16).astype(F32)
    s_lo = (sl - s_hi - s_mid).astype(BF16).astype(F32)
    sl3 = jnp.stack([s_hi, s_mid, s_lo], axis=1)
    lam_arr = jnp.full((1,), lam_init, F32)
    smem = pl.BlockSpec(memory_space=pltpu.SMEM)
    hw = hpb * V_HEAD_DIM
    head_blk = pl.BlockSpec((seq, hw), lambda b, h: (b, h))
    vec = lambda n: pl.BlockSpec((1, n), lambda b, h: (0, 0))
    row = lambda a: a.astype(F32).reshape(1, -1)
    return pl.pallas_call(
        functools.partial(_attn2_body, seq=seq, tq=tq, hpb=hpb),
        grid=(batch, N_HEADS // hpb),
        in_specs=[smem, smem, head_blk, head_blk, head_blk,
                  vec(HEAD_DIM), vec(HEAD_DIM), vec(HEAD_DIM), vec(HEAD_DIM), vec(V_HEAD_DIM)],
        out_specs=head_blk,
        out_shape=jax.ShapeDtypeStruct((batch * seq, D_V), BF16),
        scratch_shapes=[pltpu.VMEM((hpb, seq, 2 * V_HEAD_DIM), BF16),
                        pltpu.VMEM((hpb, seq // tq, V_HEAD_DIM, tq), BF16),
                        pltpu.VMEM((hpb, tq, tq), F32)],
        compiler_params=pltpu.CompilerParams(
            dimension_semantics=("parallel", "parallel"),
            vmem_limit_bytes=VMEM_LIMIT),
        name="diff_attention",
    )(sl3, lam_arr, q, k, v, row(lq1), row(lk1), row(lq2), row(lk2), row(subg))


def _ssm_body(u_ref, bblk_ref, are_ref, aim_ref, cblk_ref, d_ref, y_ref,
              x_scr, h_scr, st_scr, *, batch, tt):
    nc = SSM_NC

    @pl.when(pl.program_id(1) == 0)
    def _():
        st_scr[...] = jnp.zeros_like(st_scr)

    u = u_ref[...]
    x_scr[...] = jnp.dot(u, bblk_ref[...], preferred_element_type=F32)
    ar = jnp.broadcast_to(are_ref[...], (batch, nc))
    ai = jnp.broadcast_to(aim_ref[...], (batch, nc))

    def step(t, carry):
        hr, hi = carry
        r0 = pl.multiple_of(t * batch, batch)
        xr = x_scr[pl.ds(r0, batch), :nc]
        xi = x_scr[pl.ds(r0, batch), nc:]
        nr = ar * hr - ai * hi + xr
        ni = ar * hi + ai * hr + xi
        h_scr[pl.ds(r0, batch), :nc] = nr.astype(BF16)
        h_scr[pl.ds(r0, batch), nc:] = ni.astype(BF16)
        return nr, ni

    hr, hi = lax.fori_loop(0, tt, step, (st_scr[:, :nc], st_scr[:, nc:]), unroll=4)
    st_scr[:, :nc] = hr
    st_scr[:, nc:] = hi
    y = jnp.dot(h_scr[...], cblk_ref[...], preferred_element_type=F32)
    y = y + d_ref[...] * u.astype(F32)
    y_ref[...] = _gelu_tanh(y).astype(BF16)


def _ssm_params(a_re, a_im, log_dt, b_re, b_im, c_re, c_im):
    gpt = GROUPS_PER_TILE
    ngt = N_GROUPS // gpt
    dt = jnp.exp(log_dt)[:, None]
    mag = jnp.exp(dt * a_re)
    ab_re = mag * jnp.cos(dt * a_im)
    ab_im = mag * jnp.sin(dt * a_im)
    den = a_re * a_re + a_im * a_im
    zr = ab_re - 1.0
    zi = ab_im
    f_re = (zr * a_re + zi * a_im) / den
    f_im = (zi * a_re - zr * a_im) / den
    bb_re = f_re[..., None] * b_re - f_im[..., None] * b_im
    bb_im = f_re[..., None] * b_im + f_im[..., None] * b_re
    eye = jnp.eye(gpt, dtype=F32)

    def bdiag_in(bb):
        t = bb.reshape(ngt, gpt, SSM_STATE, SSM_GROUP)
        return jnp.einsum('tgpi,gh->tgihp', t, eye).reshape(ngt, gpt * SSM_GROUP, SSM_NC)

    def bdiag_out(cc):
        t = cc.reshape(ngt, gpt, SSM_GROUP, SSM_STATE)
        return jnp.einsum('tgop,gh->tgpho', t, eye).reshape(ngt, SSM_NC, gpt * SSM_GROUP)

    bblk = jnp.concatenate([bdiag_in(bb_re), bdiag_in(bb_im)], axis=-1).astype(BF16)
    cblk = jnp.concatenate([bdiag_out(c_re), -bdiag_out(c_im)], axis=1).astype(BF16)
    return bblk, ab_re.reshape(ngt, 1, SSM_NC), ab_im.reshape(ngt, 1, SSM_NC), cblk


def _ssm(u_tm, bblk, are, aim, cblk, d_skip, batch, seq, *, tt=128):
    gcols = GROUPS_PER_TILE * SSM_GROUP
    ngt = N_GROUPS // GROUPS_PER_TILE
    tt = min(tt, seq)
    rows = tt * batch
    return pl.pallas_call(
        functools.partial(_ssm_body, batch=batch, tt=tt),
        grid=(ngt, seq // tt),
        in_specs=[
            pl.BlockSpec((rows, gcols), lambda g, t: (t, g)),
            pl.BlockSpec((None, gcols, 2 * SSM_NC), lambda g, t: (g, 0, 0)),
            pl.BlockSpec((None, 1, SSM_NC), lambda g, t: (g, 0, 0)),
            pl.BlockSpec((None, 1, SSM_NC), lambda g, t: (g, 0, 0)),
            pl.BlockSpec((None, 2 * SSM_NC, gcols), lambda g, t: (g, 0, 0)),
            pl.BlockSpec((1, gcols), lambda g, t: (0, g)),
        ],
        out_specs=pl.BlockSpec((rows, gcols), lambda g, t: (t, g)),
        out_shape=jax.ShapeDtypeStruct((seq * batch, D_SSM), BF16),
        scratch_shapes=[pltpu.VMEM((rows, 2 * SSM_NC), F32),
                        pltpu.VMEM((rows, 2 * SSM_NC), BF16),
                        pltpu.VMEM((batch, 2 * SSM_NC), F32)],
        compiler_params=pltpu.CompilerParams(
            dimension_semantics=("parallel", "arbitrary"),
            vmem_limit_bytes=VMEM_LIMIT),
        name="s5_scan",
    )(u_tm, bblk, are, aim, cblk, d_skip.astype(F32).reshape(1, D_SSM))


def _mix_body(y_ref, oa_ref, gate_ref, x_ref, gw_ref, gb_ref, wba_ref, wbs_ref, wo_ref, out_ref):
    y = y_ref[...]
    z = jnp.dot(y, gw_ref[...], preferred_element_type=F32) + gb_ref[...]
    y2 = (y.astype(F32) * jax.nn.sigmoid(z)).astype(BF16)
    o_ssm = jnp.dot(y2, wbs_ref[...], preferred_element_type=F32)
    o_att = jnp.dot(oa_ref[...], wba_ref[...], preferred_element_type=F32)
    mixed = (jax.nn.sigmoid(gate_ref[:, :D_MODEL].astype(F32)) * o_att
             + jax.nn.sigmoid(gate_ref[:, D_MODEL:].astype(F32)) * o_ssm)
    out_ref[...] = x_ref[...] + jnp.dot(mixed.astype(BF16), wo_ref[...],
                                        preferred_element_type=F32)


def _mix(y_tm, o_att, gates, xf, glu_w, glu_b, w_ba, w_bs, w_out, batch, seq, *, tm=256):
    t = xf.shape[0]
    tm = min(tm, seq)
    nsb = seq // tm
    full = lambda shape: pl.BlockSpec(shape, lambda i: (0, 0), pipeline_mode=pl.Buffered(1))
    return pl.pallas_call(
        _mix_body,
        grid=(t // tm,),
        in_specs=[
            pl.BlockSpec((tm, D_SSM), lambda i: (i % nsb, i // nsb)),
            pl.BlockSpec((tm, D_V), lambda i: (i, 0)),
            pl.BlockSpec((tm, 2 * D_MODEL), lambda i: (i, 0)),
            pl.BlockSpec((tm, D_MODEL), lambda i: (i, 0)),
            full((D_SSM, D_SSM)), full((1, D_SSM)),
            full((D_V, D_MODEL)), full((D_SSM, D_MODEL)), full((D_MODEL, D_MODEL)),
        ],
        out_specs=pl.BlockSpec((tm, D_MODEL), lambda i: (i, 0)),
        out_shape=jax.ShapeDtypeStruct((t, D_MODEL), F32),
        compiler_params=pltpu.CompilerParams(
            dimension_semantics=("parallel",),
            vmem_limit_bytes=VMEM_LIMIT),
        name="gated_merge",
    )(y_tm, o_att, gates, xf, glu_w, glu_b.astype(F32).reshape(1, D_SSM), w_ba, w_bs, w_out)


def _shift_rows(up, prev, k):
    body = pltpu.roll(up, k, axis=0)
    top = pltpu.roll(jnp.concatenate([prev, up[:8]], axis=0), k, axis=0)[8:]
    return jnp.concatenate([top, body[8:]], axis=0)


def _ffn_up_body(x_ref, g_ref, wa_ref, wv_ref, cwa_ref, cwv_ref, cba_ref, cbv_ref,
                 act_ref, h_scr, carry_a, carry_v, *, nsb):
    i = pl.program_id(0)
    j = pl.program_id(1)

    @pl.when(j == 0)
    def _():
        h_scr[...] = _rms_rows(x_ref[...], g_ref[...]).astype(BF16)

    h = h_scr[...]
    seq_start = i % nsb == 0

    def conv(w_ref, cw_ref, cb_ref, carry):
        up = jnp.dot(h, w_ref[...], preferred_element_type=F32)
        prev = jnp.where(seq_start, 0.0, carry[j])
        carry[j] = up[up.shape[0] - 8:]
        cw = cw_ref[...]
        return (cb_ref[...] + cw[0:1] * _shift_rows(up, prev, 2)
                + cw[1:2] * _shift_rows(up, prev, 1) + cw[2:3] * up)

    a = conv(wa_ref, cwa_ref, cba_ref, carry_a)
    val = conv(wv_ref, cwv_ref, cbv_ref, carry_v)
    act_ref[...] = (_gelu_tanh(a) * val).astype(BF16)


def _ffn_up(xf, g, wa, wv, cwa, cwv, cba, cbv, seq, *, tm=512, tn=FF_TILE):
    t = xf.shape[0]
    tm = min(tm, seq)
    ncol = D_FF_PAD // tn
    const = lambda i, j: (0, 0)
    col = lambda i, j: (0, j)
    return pl.pallas_call(
        functools.partial(_ffn_up_body, nsb=seq // tm),
        grid=(t // tm, ncol),
        in_specs=[
            pl.BlockSpec((tm, D_MODEL), lambda i, j: (i, 0)),
            pl.BlockSpec((1, D_MODEL), const),
            pl.BlockSpec((D_MODEL, tn), col),
            pl.BlockSpec((D_MODEL, tn), col),
            pl.BlockSpec((CONV_W, tn), col),
            pl.BlockSpec((CONV_W, tn), col),
            pl.BlockSpec((1, tn), col),
            pl.BlockSpec((1, tn), col),
        ],
        out_specs=pl.BlockSpec((tm, tn), lambda i, j: (i, j)),
        out_shape=jax.ShapeDtypeStruct((t, D_FF_PAD), BF16),
        scratch_shapes=[pltpu.VMEM((tm, D_MODEL), BF16),
                        pltpu.VMEM((ncol, 8, tn), F32),
                        pltpu.VMEM((ncol, 8, tn), F32)],
        compiler_params=pltpu.CompilerParams(
            dimension_semantics=("arbitrary", "arbitrary"),
            vmem_limit_bytes=VMEM_LIMIT),
        name="ffn_up_conv_gate",
    )(xf, g.reshape(1, D_MODEL), wa, wv, cwa, cwv, cba, cbv)


def _ffn_down_body(act_ref, w_ref, x_ref, out_ref):
    out_ref[...] = x_ref[...] + jnp.dot(act_ref[...], w_ref[...], preferred_element_type=F32)


def _ffn_down(act, w_down, xf, *, tm=1024, tn=512):
    t = xf.shape[0]
    tm = min(tm, t)
    return pl.pallas_call(
        _ffn_down_body,
        grid=(t // tm, D_MODEL // tn),
        in_specs=[
            pl.BlockSpec((tm, D_FF_PAD), lambda i, j: (i, 0)),
            pl.BlockSpec((D_FF_PAD, tn), lambda i, j: (0, j)),
            pl.BlockSpec((tm, tn), lambda i, j: (i, j)),
        ],
        out_specs=pl.BlockSpec((tm, tn), lambda i, j: (i, j)),
        out_shape=jax.ShapeDtypeStruct((t, D_MODEL), F32),
        compiler_params=pltpu.CompilerParams(
            dimension_semantics=("parallel", "parallel"),
            vmem_limit_bytes=VMEM_LIMIT),
        name="ffn_down",
    )(act, w_down, xf)


def _pad_ff_cols(a):
    return jnp.pad(a, ((0, 0), (0, D_FF_PAD - D_FF)))


def kernel(x, norm1_g, w_in, q_norm_g, k_norm_g, lambda_q1, lambda_k1, lambda_q2, lambda_k2, subln_g, ssm_a_re, ssm_a_im, ssm_log_dt, ssm_b_re, ssm_b_im, ssm_c_re, ssm_c_im, ssm_d, ssm_glu_w, ssm_glu_b, w_branch_attn, w_branch_ssm, w_out, norm2_g, ffn_w_up, ffn_conv_w, ffn_conv_b, ffn_w_down):
    batch, seq, _ = x.shape
    depth = w_in.shape[0]
    xf = x.reshape(batch * seq, D_MODEL)
    for l in range(depth):
        lam_init = 0.8 - 0.6 * math.exp(-0.3 * l)
        q, k, v, u_tm, gates = _in_proj(xf, norm1_g[l], w_in[l].astype(BF16),
                                        q_norm_g[l], k_norm_g[l], batch, seq)
        o_att = _attention2(q, k, v, lambda_q1[l], lambda_k1[l], lambda_q2[l], lambda_k2[l],
                           subln_g[l], lam_init, batch, seq)
        bblk, are, aim, cblk = _ssm_params(ssm_a_re[l], ssm_a_im[l], ssm_log_dt[l],
                                           ssm_b_re[l], ssm_b_im[l], ssm_c_re[l], ssm_c_im[l])
        y_tm = _ssm(u_tm.reshape(seq * batch, D_SSM), bblk, are, aim, cblk, ssm_d[l], batch, seq)
        xf = _mix(y_tm.reshape(seq, batch * D_SSM), o_att, gates, xf,
                  ssm_glu_w[l].astype(BF16), ssm_glu_b[l],
                  w_branch_attn[l].astype(BF16), w_branch_ssm[l].astype(BF16),
                  w_out[l].astype(BF16), batch, seq)
        wu = ffn_w_up[l]
        cw = ffn_conv_w[l]
        cb = ffn_conv_b[l].reshape(1, 2 * D_FF)
        act = _ffn_up(xf, norm2_g[l],
                      _pad_ff_cols(wu[:, :D_FF]).astype(BF16), _pad_ff_cols(wu[:, D_FF:]).astype(BF16),
                      _pad_ff_cols(cw[:, :D_FF]), _pad_ff_cols(cw[:, D_FF:]),
                      _pad_ff_cols(cb[:, :D_FF]), _pad_ff_cols(cb[:, D_FF:]), seq)
        w_down = jnp.pad(ffn_w_down[l], ((0, D_FF_PAD - D_FF), (0, 0))).astype(BF16)
        xf = _ffn_down(act, w_down, xf)
    return xf.reshape(batch, seq, D_MODEL)
```

```python
import functools
import math

import jax
import jax.numpy as jnp
from jax import lax
from jax.experimental import pallas as pl
from jax.experimental.pallas import tpu as pltpu

F32 = jnp.float32
BF16 = jnp.bfloat16

D_MODEL = 2048
N_HEADS = 8
HEAD_DIM = 64
V_HEAD_DIM = 2 * HEAD_DIM
D_QK = N_HEADS * 2 * HEAD_DIM
D_V = N_HEADS * V_HEAD_DIM
D_SSM = D_MODEL // 2
SSM_GROUP = 16
N_GROUPS = D_SSM // SSM_GROUP
SSM_STATE = 64
D_FF = 5504
CONV_W = 3
CHUNK = 64
EPS = 1e-6
IN_COLS = 2 * D_QK + D_V + D_SSM + 2 * D_MODEL

MXU_DIM = 256
FF_TILE = 512
D_FF_PAD = ((D_FF + FF_TILE - 1) // FF_TILE) * FF_TILE
GROUPS_PER_TILE = 8
SSM_NC = GROUPS_PER_TILE * SSM_STATE
NEG_BIG = -1e30
Q_SCALE = HEAD_DIM ** -0.5 * math.log2(math.e)
VMEM_LIMIT = 56 * 1024 * 1024


def _gelu_tanh(x):
    c = math.sqrt(2.0 / math.pi)
    return x * (0.5 * (1.0 + jnp.tanh(c * (x + 0.044715 * (x * x * x)))))


def _rms_rows(xf, g):
    ms = jnp.mean(xf * xf, axis=-1, keepdims=True)
    return xf * lax.rsqrt(ms + EPS) * g


def _in_proj_body(x_ref, g_ref, w_ref, qg_ref, kg_ref, ones_ref,
                  q_ref, k_ref, v_ref, u_ref, gate_ref, h_scr, *, tn):
    j = pl.program_id(1)
    nq = D_QK // tn

    @pl.when(j == 0)
    def _():
        h_scr[...] = _rms_rows(x_ref[...], g_ref[...]).astype(BF16)

    acc = jnp.dot(h_scr[...], w_ref[...], preferred_element_type=F32)

    def head_norm(gain_ref, scale):
        outs = []
        for c in range(tn // MXU_DIM):
            a = acc[:, c * MXU_DIM:(c + 1) * MXU_DIM]
            ss = jnp.dot((a * a).astype(BF16), ones_ref[...], preferred_element_type=F32)
            outs.append(a * lax.rsqrt(ss * (1.0 / HEAD_DIM) + EPS) * (gain_ref[...] * scale))
        return jnp.concatenate(outs, axis=1)

    @pl.when(j < nq)
    def _():
        q_ref[...] = head_norm(qg_ref, Q_SCALE).astype(BF16)

    @pl.when((j >= nq) & (j < 2 * nq))
    def _():
        k_ref[...] = head_norm(kg_ref, 1.0).astype(BF16)

    @pl.when((j >= 2 * nq) & (j < 3 * nq))
    def _():
        v_ref[...] = acc.astype(BF16)

    @pl.when((j >= 3 * nq) & (j < 4 * nq))
    def _():
        u_ref[...] = acc.astype(BF16)

    @pl.when(j >= 4 * nq)
    def _():
        gate_ref[...] = acc.astype(BF16)


def _in_proj(xf, g, w_bf, qg, kg, batch, seq, *, tm=1024, tn=512):
    t = xf.shape[0]
    tm = min(tm, seq)
    nq = D_QK // tn
    ng = 2 * D_MODEL // tn
    qg_t = jnp.tile(qg.astype(F32), MXU_DIM // HEAD_DIM).reshape(1, MXU_DIM)
    kg_t = jnp.tile(kg.astype(F32), MXU_DIM // HEAD_DIM).reshape(1, MXU_DIM)
    seg = jnp.arange(MXU_DIM) // HEAD_DIM
    ones_bd = (seg[:, None] == seg[None, :]).astype(BF16)

    def cl(j, lo, n):
        return jnp.clip(j - lo, 0, n - 1)

    const = lambda i, j: (0, 0)
    return pl.pallas_call(
        functools.partial(_in_proj_body, tn=tn),
        grid=(t // tm, IN_COLS // tn),
        in_specs=[
            pl.BlockSpec((tm, D_MODEL), lambda i, j: (i, 0)),
            pl.BlockSpec((1, D_MODEL), const),
            pl.BlockSpec((D_MODEL, tn), lambda i, j: (0, j)),
            pl.BlockSpec((1, MXU_DIM), const),
            pl.BlockSpec((1, MXU_DIM), const),
            pl.BlockSpec((MXU_DIM, MXU_DIM), const),
        ],
        out_specs=[
            pl.BlockSpec((tm, tn), lambda i, j: (i, cl(j, 0, nq))),
            pl.BlockSpec((tm, tn), lambda i, j: (i, cl(j, nq, nq))),
            pl.BlockSpec((tm, tn), lambda i, j: (i, cl(j, 2 * nq, nq))),
            pl.BlockSpec((tm, tn), lambda i, j: (i, cl(j, 3 * nq, nq))),
            pl.BlockSpec((tm, tn), lambda i, j: (i, cl(j, 4 * nq, ng))),
        ],
        out_shape=[
            jax.ShapeDtypeStruct((t, D_QK), BF16),
            jax.ShapeDtypeStruct((t, D_QK), BF16),
            jax.ShapeDtypeStruct((t, D_V), BF16),
            jax.ShapeDtypeStruct((t, D_SSM), BF16),
            jax.ShapeDtypeStruct((t, 2 * D_MODEL), BF16),
        ],
        scratch_shapes=[pltpu.VMEM((tm, D_MODEL), BF16)],
        compiler_params=pltpu.CompilerParams(
            dimension_semantics=("arbitrary", "arbitrary"),
            vmem_limit_bytes=VMEM_LIMIT),
        name="in_proj",
    )(xf, g.reshape(1, D_MODEL), w_bf, qg_t, kg_t, ones_bd)


N_BIAS_ROWS = 3


def _attn_body(sl_ref, laminit_ref, q_ref, k_ref, v_ref, lq1_ref, lk1_ref, lq2_ref, lk2_ref,
               subg_ref, o_ref, kaug_scr, vt_scr, dmat_scr, *, seq, tq, hpb):
    tk = tq
    nblk = seq // tk
    hd2 = 2 * HEAD_DIM
    nch = 2 * hpb
    hg = pl.program_id(1)
    lam_init = laminit_ref[0]
    lam = (jnp.exp(jnp.sum(lq1_ref[...] * lk1_ref[...], axis=-1, keepdims=True))
           - jnp.exp(jnp.sum(lq2_ref[...] * lk2_ref[...], axis=-1, keepdims=True))
           + lam_init)
    subg = subg_ref[...] * (1.0 - lam_init)

    koff = lax.broadcasted_iota(jnp.int32, (seq, hd2), 0) % tk
    klane = lax.broadcasted_iota(jnp.int32, (seq, hd2), 1)
    kbias = jnp.where(klane < N_BIAS_ROWS, koff, 0).astype(F32).astype(BF16)
    c = lax.broadcasted_iota(jnp.int32, (tk, tq), 0)
    r = lax.broadcasted_iota(jnp.int32, (tk, tq), 1)
    visible = (c // CHUNK) <= (r // CHUNK)
    ahead = jnp.maximum(c - r, 0).astype(F32)
    brow_i = lax.broadcasted_iota(jnp.int32, (hd2, tq), 0)
    sl2 = []
    brows = []
    for hh in range(hpb):
        h = hg * hpb + hh
        s_hi, s_mid, s_lo = sl_ref[h, 0], sl_ref[h, 1], sl_ref[h, 2]
        sl2.append(s_hi + s_mid + s_lo)
        brows.append(jnp.where(brow_i == 0, s_hi, jnp.where(brow_i == 1, s_mid,
                     jnp.where(brow_i == 2, s_lo, 0.0))).astype(BF16))
        kaug_scr[hh, :, 0:hd2] = k_ref[:, hh * hd2:(hh + 1) * hd2]
        kaug_scr[hh, :, hd2:2 * hd2] = kbias
        for blk in range(nblk):
            vt_scr[hh, blk] = v_ref[blk * tk:(blk + 1) * tk,
                                    hh * hd2:(hh + 1) * hd2].astype(F32).T.astype(BF16)
        dmat_scr[hh] = jnp.where(visible, (-2.0 * sl2[hh]) * ahead, NEG_BIG)

    def q_block(iq, _):
        q0 = pl.multiple_of(iq * tq, tq)
        qas = []
        for hh in range(hpb):
            qt = q_ref[pl.ds(q0, tq), hh * hd2:(hh + 1) * hd2].astype(F32).T
            for comp in range(2):
                keep = (brow_i < HEAD_DIM) if comp == 0 else (brow_i >= HEAD_DIM)
                qas.append(jnp.concatenate([jnp.where(keep, qt, 0.0).astype(BF16), brows[hh]],
                                           axis=0))
        ss = [jnp.dot(kaug_scr[idx // 2, pl.ds(q0, tk), :], qas[idx],
                      preferred_element_type=F32) + dmat_scr[idx // 2] for idx in range(nch)]
        ps, stats = [], []
        for idx in range(nch):
            m = jnp.max(ss[idx], axis=0, keepdims=True)
            p = jnp.exp2(ss[idx] - m)
            stats.append((m, jnp.sum(p, axis=0, keepdims=True)))
            ps.append(p.astype(BF16))
        state = []
        for idx in range(nch):
            acc = jnp.dot(vt_scr[idx // 2, iq], ps[idx], preferred_element_type=F32)
            state.extend([stats[idx][0], stats[idx][1], acc])

        def kv_block(j, carry):
            k0 = pl.multiple_of(j * tk, tk)
            boff = lax.convert_element_type((j - iq) * tk, F32)
            ss = [jnp.dot(kaug_scr[idx // 2, pl.ds(k0, tk), :], qas[idx],
                          preferred_element_type=F32) for idx in range(nch)]
            ps, stats = [], []
            for idx in range(nch):
                m, l, _ = carry[3 * idx:3 * idx + 3]
                bc = sl2[idx // 2] * boff
                m_new = jnp.maximum(m, jnp.max(ss[idx], axis=0, keepdims=True) + bc)
                alpha = jnp.exp2(m - m_new)
                p = jnp.exp2(ss[idx] - (m_new - bc))
                stats.append((m_new, alpha * l + jnp.sum(p, axis=0, keepdims=True), alpha))
                ps.append(p.astype(BF16))
            out = []
            for idx in range(nch):
                m_new, l_new, alpha = stats[idx]
                acc = alpha * carry[3 * idx + 2] + jnp.dot(vt_scr[idx // 2, j], ps[idx],
                                                           preferred_element_type=F32)
                out.extend([m_new, l_new, acc])
            return tuple(out)

        state = lax.fori_loop(0, iq, kv_block, tuple(state))
        for hh in range(hpb):
            _, l0, a0, _, l1, a1 = state[6 * hh:6 * hh + 6]
            ot = a0 * (1.0 / l0) - a1 * (lam / l1)
            ot = ot * lax.rsqrt(jnp.mean(ot * ot, axis=0, keepdims=True) + EPS)
            o_ref[pl.ds(q0, tq), hh * hd2:(hh + 1) * hd2] = (ot.T * subg).astype(BF16)
        return 0

    lax.fori_loop(0, seq // tq, q_block, 0)


def _attention(q, k, v, lq1, lk1, lq2, lk2, subg, lam_init, batch, seq, *, tq=256, hpb=4):
    slopes = 2.0 ** (-8.0 * jnp.arange(1, N_HEADS + 1, dtype=F32) / N_HEADS)
    sl = slopes * math.log2(math.e)
    s_hi = sl.astype(BF16).astype(F32)
    s_mid = (sl - s_hi).astype(BF16).astype(F32)
    s_lo = (sl - s_hi - s_mid).astype(BF16).astype(F32)
    sl3 = jnp.stack([s_hi, s_mid, s_lo], axis=1)
    lam_arr = jnp.full((1,), lam_init, F32)
    smem = pl.BlockSpec(memory_space=pltpu.SMEM)
    hw = hpb * V_HEAD_DIM
    head_blk = pl.BlockSpec((seq, hw), lambda b, h: (b, h))
    vec = lambda n: pl.BlockSpec((1, n), lambda b, h: (0, 0))
    row = lambda a: a.astype(F32).reshape(1, -1)
    return pl.pallas_call(
        functools.partial(_attn_body, seq=seq, tq=tq, hpb=hpb),
        grid=(batch, N_HEADS // hpb),
        in_specs=[smem, smem, head_blk, head_blk, head_blk,
                  vec(HEAD_DIM), vec(HEAD_DIM), vec(HEAD_DIM), vec(HEAD_DIM), vec(V_HEAD_DIM)],
        out_specs=head_blk,
        out_shape=jax.ShapeDtypeStruct((batch * seq, D_V), BF16),
        scratch_shapes=[pltpu.VMEM((hpb, seq, 2 * V_HEAD_DIM), BF16),
                        pltpu.VMEM((hpb, seq // tq, V_HEAD_DIM, tq), BF16),
                        pltpu.VMEM((hpb, tq, tq), F32)],
        compiler_params=pltpu.CompilerParams(
            dimension_semantics=("parallel", "parallel"),
            vmem_limit_bytes=VMEM_LIMIT),
        name="diff_attention",
    )(sl3, lam_arr, q, k, v, row(lq1), row(lk1), row(lq2), row(lk2), row(subg))


def _ssm_body(u_ref, bblk_ref, are_ref, aim_ref, cblk_ref, d_ref, y_ref,
              x_scr, h_scr, st_scr, *, batch, tt):
    nc = st_scr.shape[-1] // 2
    gcols = u_ref.shape[-1]

    @pl.when(pl.program_id(1) == 0)
    def _():
        st_scr[...] = jnp.zeros_like(st_scr)

    u_f = pltpu.einshape("btn->tbn", u_ref[...].astype(F32)).reshape(tt * batch, gcols)
    x_scr[...] = jnp.dot(u_f.astype(BF16), bblk_ref[...], preferred_element_type=F32)
    ar = jnp.broadcast_to(are_ref[...], (batch, nc))
    ai = jnp.broadcast_to(aim_ref[...], (batch, nc))

    def step(t, carry):
        hr, hi = carry
        r0 = pl.multiple_of(t * batch, batch)
        xr = x_scr[pl.ds(r0, batch), :nc]
        xi = x_scr[pl.ds(r0, batch), nc:]
        nr = ar * hr - ai * hi + xr
        ni = ar * hi + ai * hr + xi
        h_scr[pl.ds(r0, batch), :nc] = nr.astype(BF16)
        h_scr[pl.ds(r0, batch), nc:] = ni.astype(BF16)
        return nr, ni

    hr, hi = lax.fori_loop(0, tt, step, (st_scr[:, :nc], st_scr[:, nc:]), unroll=4)
    st_scr[:, :nc] = hr
    st_scr[:, nc:] = hi
    y = jnp.dot(h_scr[...], cblk_ref[...], preferred_element_type=F32)
    y = _gelu_tanh(y + d_ref[...] * u_f)
    y_ref[...] = pltpu.einshape("tbn->btn", y.reshape(tt, batch, gcols)).astype(BF16)


def _ssm(u3, bblk, are, aim, cblk, d_skip, *, tt=128):
    batch, seq, _ = u3.shape
    gcols = GROUPS_PER_TILE * SSM_GROUP
    ngt = N_GROUPS // GROUPS_PER_TILE
    tt = min(tt, seq)
    rows = tt * batch
    return pl.pallas_call(
        functools.partial(_ssm_body, batch=batch, tt=tt),
        grid=(ngt, seq // tt),
        in_specs=[
            pl.BlockSpec((batch, tt, gcols), lambda g, t: (0, t, g)),
            pl.BlockSpec((None, gcols, 2 * SSM_NC), lambda g, t: (g, 0, 0)),
            pl.BlockSpec((None, 1, SSM_NC), lambda g, t: (g, 0, 0)),
            pl.BlockSpec((None, 1, SSM_NC), lambda g, t: (g, 0, 0)),
            pl.BlockSpec((None, 2 * SSM_NC, gcols), lambda g, t: (g, 0, 0)),
            pl.BlockSpec((1, gcols), lambda g, t: (0, g)),
        ],
        out_specs=pl.BlockSpec((batch, tt, gcols), lambda g, t: (0, t, g)),
        out_shape=jax.ShapeDtypeStruct((batch, seq, D_SSM), BF16),
        scratch_shapes=[pltpu.VMEM((rows, 2 * SSM_NC), F32),
                        pltpu.VMEM((rows, 2 * SSM_NC), BF16),
                        pltpu.VMEM((batch, 2 * SSM_NC), F32)],
        compiler_params=pltpu.CompilerParams(
            dimension_semantics=("parallel", "arbitrary"),
            vmem_limit_bytes=VMEM_LIMIT),
        name="s5_scan",
    )(u3, bblk, are, aim, cblk, d_skip.astype(F32).reshape(1, D_SSM))


def _ssm_params(a_re, a_im, log_dt, b_re, b_im, c_re, c_im):
    gpt = GROUPS_PER_TILE
    ngt = N_GROUPS // gpt
    dt = jnp.exp(log_dt)[:, None]
    mag = jnp.exp(dt * a_re)
    ab_re = mag * jnp.cos(dt * a_im)
    ab_im = mag * jnp.sin(dt * a_im)
    den = a_re * a_re + a_im * a_im
    zr = ab_re - 1.0
    zi = ab_im
    f_re = (zr * a_re + zi * a_im) / den
    f_im = (zi * a_re - zr * a_im) / den
    bb_re = f_re[..., None] * b_re - f_im[..., None] * b_im
    bb_im = f_re[..., None] * b_im + f_im[..., None] * b_re
    eye = jnp.eye(gpt, dtype=F32)

    def bdiag_in(bb):
        t = bb.reshape(ngt, gpt, SSM_STATE, SSM_GROUP)
        return jnp.einsum('tgpi,gh->tgihp', t, eye).reshape(ngt, gpt * SSM_GROUP, SSM_NC)

    def bdiag_out(cc):
        t = cc.reshape(ngt, gpt, SSM_GROUP, SSM_STATE)
        return jnp.einsum('tgop,gh->tgpho', t, eye).reshape(ngt, SSM_NC, gpt * SSM_GROUP)

    bblk = jnp.concatenate([bdiag_in(bb_re), bdiag_in(bb_im)], axis=-1).astype(BF16)
    cblk = jnp.concatenate([bdiag_out(c_re), -bdiag_out(c_im)], axis=1).astype(BF16)
    return bblk, ab_re.reshape(ngt, 1, SSM_NC), ab_im.reshape(ngt, 1, SSM_NC), cblk


def _mix_body(y_ref, oa_ref, gate_ref, x_ref, gw_ref, gb_ref, wba_ref, wbs_ref, wo_ref, out_ref):
    y = y_ref[...]
    z = jnp.dot(y, gw_ref[...], preferred_element_type=F32) + gb_ref[...]
    y2 = (y.astype(F32) * jax.nn.sigmoid(z)).astype(BF16)
    o_ssm = jnp.dot(y2, wbs_ref[...], preferred_element_type=F32)
    o_att = jnp.dot(oa_ref[...], wba_ref[...], preferred_element_type=F32)
    mixed = (jax.nn.sigmoid(gate_ref[:, :D_MODEL].astype(F32)) * o_att
             + jax.nn.sigmoid(gate_ref[:, D_MODEL:].astype(F32)) * o_ssm)
    out_ref[...] = x_ref[...] + jnp.dot(mixed.astype(BF16), wo_ref[...],
                                        preferred_element_type=F32)


def _mix(y, o_att, gates, xf, glu_w, glu_b, w_ba, w_bs, w_out, batch, seq, *, tm=256):
    t = xf.shape[0]
    tm = min(tm, seq)
    full = lambda shape: pl.BlockSpec(shape, lambda i: (0, 0), pipeline_mode=pl.Buffered(1))
    return pl.pallas_call(
        _mix_body,
        grid=(t // tm,),
        in_specs=[
            pl.BlockSpec((tm, D_SSM), lambda i: (i, 0)),
            pl.BlockSpec((tm, D_V), lambda i: (i, 0)),
            pl.BlockSpec((tm, 2 * D_MODEL), lambda i: (i, 0)),
            pl.BlockSpec((tm, D_MODEL), lambda i: (i, 0)),
            full((D_SSM, D_SSM)), full((1, D_SSM)),
            full((D_V, D_MODEL)), full((D_SSM, D_MODEL)), full((D_MODEL, D_MODEL)),
        ],
        out_specs=pl.BlockSpec((tm, D_MODEL), lambda i: (i, 0)),
        out_shape=jax.ShapeDtypeStruct((t, D_MODEL), F32),
        compiler_params=pltpu.CompilerParams(
            dimension_semantics=("parallel",),
            vmem_limit_bytes=VMEM_LIMIT),
        name="gated_merge",
    )(y, o_att, gates, xf, glu_w, glu_b.astype(F32).reshape(1, D_SSM), w_ba, w_bs, w_out)


def _shift_rows(up, prev, k):
    body = pltpu.roll(up, k, axis=0)
    top = pltpu.roll(jnp.concatenate([prev, up[:8]], axis=0), k, axis=0)[8:]
    return jnp.concatenate([top, body[8:]], axis=0)


def _ffn_up_body(x_ref, g_ref, wa_ref, wv_ref, cwa_ref, cwv_ref, cba_ref, cbv_ref,
                 act_ref, h_scr, carry_a, carry_v, *, nsb):
    i = pl.program_id(0)
    j = pl.program_id(1)

    @pl.when(j == 0)
    def _():
        h_scr[...] = _rms_rows(x_ref[...], g_ref[...]).astype(BF16)

    h = h_scr[...]
    seq_start = i % nsb == 0

    def conv(w_ref, cw_ref, cb_ref, carry):
        up = jnp.dot(h, w_ref[...], preferred_element_type=F32)
        prev = jnp.where(seq_start, 0.0, carry[j])
        carry[j] = up[up.shape[0] - 8:]
        cw = cw_ref[...]
        return (cb_ref[...] + cw[0:1] * _shift_rows(up, prev, 2)
                + cw[1:2] * _shift_rows(up, prev, 1) + cw[2:3] * up)

    a = conv(wa_ref, cwa_ref, cba_ref, carry_a)
    val = conv(wv_ref, cwv_ref, cbv_ref, carry_v)
    act_ref[...] = (_gelu_tanh(a) * val).astype(BF16)


def _ffn_up(xf, g, wa, wv, cwa, cwv, cba, cbv, seq, *, tm=512, tn=FF_TILE):
    t = xf.shape[0]
    tm = min(tm, seq)
    ncol = D_FF_PAD // tn
    const = lambda i, j: (0, 0)
    col = lambda i, j: (0, j)
    return pl.pallas_call(
        functools.partial(_ffn_up_body, nsb=seq // tm),
        grid=(t // tm, ncol),
        in_specs=[
            pl.BlockSpec((tm, D_MODEL), lambda i, j: (i, 0)),
            pl.BlockSpec((1, D_MODEL), const),
            pl.BlockSpec((D_MODEL, tn), col),
            pl.BlockSpec((D_MODEL, tn), col),
            pl.BlockSpec((CONV_W, tn), col),
            pl.BlockSpec((CONV_W, tn), col),
            pl.BlockSpec((1, tn), col),
            pl.BlockSpec((1, tn), col),
        ],
        out_specs=pl.BlockSpec((tm, tn), lambda i, j: (i, j)),
        out_shape=jax.ShapeDtypeStruct((t, D_FF_PAD), BF16),
        scratch_shapes=[pltpu.VMEM((tm, D_MODEL), BF16),
                        pltpu.VMEM((ncol, 8, tn), F32),
                        pltpu.VMEM((ncol, 8, tn), F32)],
        compiler_params=pltpu.CompilerParams(
            dimension_semantics=("arbitrary", "arbitrary"),
            vmem_limit_bytes=VMEM_LIMIT),
        name="ffn_up_conv_gate",
    )(xf, g.reshape(1, D_MODEL), wa, wv, cwa, cwv, cba, cbv)


def _ffn_down_body(act_ref, w_ref, x_ref, out_ref):
    out_ref[...] = x_ref[...] + jnp.dot(act_ref[...], w_ref[...], preferred_element_type=F32)


def _ffn_down(act, w_down, xf, *, tm=1024, tn=512):
    t = xf.shape[0]
    tm = min(tm, t)
    return pl.pallas_call(
        _ffn_down_body,
        grid=(t // tm, D_MODEL // tn),
        in_specs=[
            pl.BlockSpec((tm, D_FF_PAD), lambda i, j: (i, 0)),
            pl.BlockSpec((D_FF_PAD, tn), lambda i, j: (0, j)),
            pl.BlockSpec((tm, tn), lambda i, j: (i, j)),
        ],
        out_specs=pl.BlockSpec((tm, tn), lambda i, j: (i, j)),
        out_shape=jax.ShapeDtypeStruct((t, D_MODEL), F32),
        compiler_params=pltpu.CompilerParams(
            dimension_semantics=("parallel", "parallel"),
            vmem_limit_bytes=VMEM_LIMIT),
        name="ffn_down",
    )(act, w_down, xf)


def _pad_ff_cols(a):
    return jnp.pad(a, ((0, 0), (0, D_FF_PAD - D_FF)))


def kernel(x, norm1_g, w_in, q_norm_g, k_norm_g, lambda_q1, lambda_k1, lambda_q2, lambda_k2, subln_g, ssm_a_re, ssm_a_im, ssm_log_dt, ssm_b_re, ssm_b_im, ssm_c_re, ssm_c_im, ssm_d, ssm_glu_w, ssm_glu_b, w_branch_attn, w_branch_ssm, w_out, norm2_g, ffn_w_up, ffn_conv_w, ffn_conv_b, ffn_w_down):
    batch, seq, _ = x.shape
    depth = w_in.shape[0]
    xf = x.reshape(batch * seq, D_MODEL)
    for l in range(depth):
        lam_init = 0.8 - 0.6 * math.exp(-0.3 * l)
        q, k, v, u, gates = _in_proj(xf, norm1_g[l], w_in[l].astype(BF16),
                                     q_norm_g[l], k_norm_g[l], batch, seq)
        o_att = _attention(q, k, v, lambda_q1[l], lambda_k1[l], lambda_q2[l], lambda_k2[l],
                           subln_g[l], lam_init, batch, seq)
        bblk, are, aim, cblk = _ssm_params(ssm_a_re[l], ssm_a_im[l], ssm_log_dt[l],
                                           ssm_b_re[l], ssm_b_im[l], ssm_c_re[l], ssm_c_im[l])
        y = _ssm(u.reshape(batch, seq, D_SSM), bblk, are, aim, cblk, ssm_d[l])
        xf = _mix(y.reshape(batch * seq, D_SSM), o_att, gates, xf,
                  ssm_glu_w[l].astype(BF16), ssm_glu_b[l],
                  w_branch_attn[l].astype(BF16), w_branch_ssm[l].astype(BF16),
                  w_out[l].astype(BF16), batch, seq)
        wu = ffn_w_up[l]
        cw = ffn_conv_w[l]
        cb = ffn_conv_b[l].reshape(1, 2 * D_FF)
        act = _ffn_up(xf, norm2_g[l],
                      _pad_ff_cols(wu[:, :D_FF]).astype(BF16), _pad_ff_cols(wu[:, D_FF:]).astype(BF16),
                      _pad_ff_cols(cw[:, :D_FF]), _pad_ff_cols(cw[:, D_FF:]),
                      _pad_ff_cols(cb[:, :D_FF]), _pad_ff_cols(cb[:, D_FF:]), seq)
        w_down = jnp.pad(ffn_w_down[l], ((0, D_FF_PAD - D_FF), (0, 0))).astype(BF16)
        xf = _ffn_down(act, w_down, xf)
    return xf.reshape(batch, seq, D_MODEL)
```

```python
import functools
import math

import jax
import jax.numpy as jnp
from jax import lax
from jax.experimental import pallas as pl
from jax.experimental.pallas import tpu as pltpu

F32 = jnp.float32
BF16 = jnp.bfloat16

D_MODEL = 2048
N_HEADS = 8
HEAD_DIM = 64
V_HEAD_DIM = 2 * HEAD_DIM
D_QK = N_HEADS * 2 * HEAD_DIM
D_V = N_HEADS * V_HEAD_DIM
D_SSM = D_MODEL // 2
SSM_GROUP = 16
N_GROUPS = D_SSM // SSM_GROUP
SSM_STATE = 64
D_FF = 5504
CONV_W = 3
CHUNK = 64
EPS = 1e-6
IN_COLS = 2 * D_QK + D_V + D_SSM + 2 * D_MODEL

MXU_DIM = 256
FF_TILE = 512
D_FF_PAD = ((D_FF + FF_TILE - 1) // FF_TILE) * FF_TILE
GROUPS_PER_TILE = 8
SSM_NC = GROUPS_PER_TILE * SSM_STATE
NEG_BIG = -1e30
Q_SCALE = HEAD_DIM ** -0.5 * math.log2(math.e)
VMEM_LIMIT = 56 * 1024 * 1024


def _gelu_tanh(x):
    c = math.sqrt(2.0 / math.pi)
    return x * (0.5 * (1.0 + jnp.tanh(c * (x + 0.044715 * (x * x * x)))))


def _rms_rows(xf, g):
    ms = jnp.mean(xf * xf, axis=-1, keepdims=True)
    return xf * lax.rsqrt(ms + EPS) * g


def _in_proj_body(x_ref, g_ref, w_ref, qg_ref, kg_ref, ones_ref,
                  q_ref, k_ref, v_ref, u_ref, gate_ref, h_scr, *, tn):
    j = pl.program_id(1)
    nq = D_QK // tn

    @pl.when(j == 0)
    def _():
        h_scr[...] = _rms_rows(x_ref[...], g_ref[...]).astype(BF16)

    acc = jnp.dot(h_scr[...], w_ref[...], preferred_element_type=F32)

    def head_norm(gain_ref, scale):
        outs = []
        for c in range(tn // MXU_DIM):
            a = acc[:, c * MXU_DIM:(c + 1) * MXU_DIM]
            ss = jnp.dot((a * a).astype(BF16), ones_ref[...], preferred_element_type=F32)
            outs.append(a * lax.rsqrt(ss * (1.0 / HEAD_DIM) + EPS) * (gain_ref[...] * scale))
        return jnp.concatenate(outs, axis=1)

    @pl.when(j < nq)
    def _():
        q_ref[...] = head_norm(qg_ref, Q_SCALE).astype(BF16)

    @pl.when((j >= nq) & (j < 2 * nq))
    def _():
        k_ref[...] = head_norm(kg_ref, 1.0).astype(BF16)

    @pl.when((j >= 2 * nq) & (j < 3 * nq))
    def _():
        v_ref[...] = acc.astype(BF16)

    @pl.when((j >= 3 * nq) & (j < 4 * nq))
    def _():
        u_ref[...] = acc.astype(BF16)

    @pl.when(j >= 4 * nq)
    def _():
        gate_ref[...] = acc.astype(BF16)


def _in_proj(xf, g, w_bf, qg, kg, batch, seq, *, tm=1024, tn=512):
    t = xf.shape[0]
    tm = min(tm, seq)
    nq = D_QK // tn
    ng = 2 * D_MODEL // tn
    qg_t = jnp.tile(qg.astype(F32), MXU_DIM // HEAD_DIM).reshape(1, MXU_DIM)
    kg_t = jnp.tile(kg.astype(F32), MXU_DIM // HEAD_DIM).reshape(1, MXU_DIM)
    seg = jnp.arange(MXU_DIM) // HEAD_DIM
    ones_bd = (seg[:, None] == seg[None, :]).astype(BF16)

    def cl(j, lo, n):
        return jnp.clip(j - lo, 0, n - 1)

    const = lambda i, j: (0, 0)
    return pl.pallas_call(
        functools.partial(_in_proj_body, tn=tn),
        grid=(t // tm, IN_COLS // tn),
        in_specs=[
            pl.BlockSpec((tm, D_MODEL), lambda i, j: (i, 0)),
            pl.BlockSpec((1, D_MODEL), const),
            pl.BlockSpec((D_MODEL, tn), lambda i, j: (0, j)),
            pl.BlockSpec((1, MXU_DIM), const),
            pl.BlockSpec((1, MXU_DIM), const),
            pl.BlockSpec((MXU_DIM, MXU_DIM), const),
        ],
        out_specs=[
            pl.BlockSpec((tm, tn), lambda i, j: (i, cl(j, 0, nq))),
            pl.BlockSpec((tm, tn), lambda i, j: (i, cl(j, nq, nq))),
            pl.BlockSpec((tm, tn), lambda i, j: (i, cl(j, 2 * nq, nq))),
            pl.BlockSpec((tm, tn), lambda i, j: (i, cl(j, 3 * nq, nq))),
            pl.BlockSpec((tm, tn), lambda i, j: (i, cl(j, 4 * nq, ng))),
        ],
        out_shape=[
            jax.ShapeDtypeStruct((t, D_QK), BF16),
            jax.ShapeDtypeStruct((t, D_QK), BF16),
            jax.ShapeDtypeStruct((t, D_V), BF16),
            jax.ShapeDtypeStruct((t, D_SSM), BF16),
            jax.ShapeDtypeStruct((t, 2 * D_MODEL), BF16),
        ],
        scratch_shapes=[pltpu.VMEM((tm, D_MODEL), BF16)],
        compiler_params=pltpu.CompilerParams(
            dimension_semantics=("arbitrary", "arbitrary"),
            vmem_limit_bytes=VMEM_LIMIT),
        name="in_proj",
    )(xf, g.reshape(1, D_MODEL), w_bf, qg_t, kg_t, ones_bd)


N_BIAS_ROWS = 3


def _attn_body(sl_ref, laminit_ref, q_ref, k_ref, v_ref, lq1_ref, lk1_ref, lq2_ref, lk2_ref,
               subg_ref, o_ref, kaug_scr, vt_scr, dmat_scr, *, seq, tq, hpb):
    tk = tq
    nblk = seq // tk
    hd2 = 2 * HEAD_DIM
    nch = 2 * hpb
    hg = pl.program_id(1)
    lam_init = laminit_ref[0]
    lam = (jnp.exp(jnp.sum(lq1_ref[...] * lk1_ref[...], axis=-1, keepdims=True))
           - jnp.exp(jnp.sum(lq2_ref[...] * lk2_ref[...], axis=-1, keepdims=True))
           + lam_init)
    subg = subg_ref[...] * (1.0 - lam_init)

    koff = lax.broadcasted_iota(jnp.int32, (seq, hd2), 0) % tk
    klane = lax.broadcasted_iota(jnp.int32, (seq, hd2), 1)
    kbias = jnp.where(klane < N_BIAS_ROWS, koff, 0).astype(F32).astype(BF16)
    c = lax.broadcasted_iota(jnp.int32, (tk, tq), 0)
    r = lax.broadcasted_iota(jnp.int32, (tk, tq), 1)
    visible = (c // CHUNK) <= (r // CHUNK)
    ahead = jnp.maximum(c - r, 0).astype(F32)
    brow_i = lax.broadcasted_iota(jnp.int32, (hd2, tq), 0)
    sl2 = []
    brows = []
    for hh in range(hpb):
        h = hg * hpb + hh
        s_hi, s_mid, s_lo = sl_ref[h, 0], sl_ref[h, 1], sl_ref[h, 2]
        sl2.append(s_hi + s_mid + s_lo)
        brows.append(jnp.where(brow_i == 0, s_hi, jnp.where(brow_i == 1, s_mid,
                     jnp.where(brow_i == 2, s_lo, 0.0))).astype(BF16))
        kaug_scr[hh, :, 0:hd2] = k_ref[:, hh * hd2:(hh + 1) * hd2]
        kaug_scr[hh, :, hd2:2 * hd2] = kbias
        for blk in range(nblk):
            vt_scr[hh, blk] = v_ref[blk * tk:(blk + 1) * tk,
                                    hh * hd2:(hh + 1) * hd2].astype(F32).T.astype(BF16)
        dmat_scr[hh] = jnp.where(visible, (-2.0 * sl2[hh]) * ahead, NEG_BIG)

    def q_block(iq, _):
        q0 = pl.multiple_of(iq * tq, tq)
        qas = []
        for hh in range(hpb):
            qt = q_ref[pl.ds(q0, tq), hh * hd2:(hh + 1) * hd2].astype(F32).T
            for comp in range(2):
                keep = (brow_i < HEAD_DIM) if comp == 0 else (brow_i >= HEAD_DIM)
                qas.append(jnp.concatenate([jnp.where(keep, qt, 0.0).astype(BF16), brows[hh]],
                                           axis=0))
        ss = [jnp.dot(kaug_scr[idx // 2, pl.ds(q0, tk), :], qas[idx],
                      preferred_element_type=F32) + dmat_scr[idx // 2] for idx in range(nch)]
        ps, stats = [], []
        for idx in range(nch):
            m = jnp.max(ss[idx], axis=0, keepdims=True)
            p = jnp.exp2(ss[idx] - m)
            stats.append((m, jnp.sum(p, axis=0, keepdims=True)))
            ps.append(p.astype(BF16))
        state = []
        for idx in range(nch):
            acc = jnp.dot(vt_scr[idx // 2, iq], ps[idx], preferred_element_type=F32)
            state.extend([stats[idx][0], stats[idx][1], acc])

        def kv_block(j, carry):
            k0 = pl.multiple_of(j * tk, tk)
            boff = lax.convert_element_type((j - iq) * tk, F32)
            ss = [jnp.dot(kaug_scr[idx // 2, pl.ds(k0, tk), :], qas[idx],
                          preferred_element_type=F32) for idx in range(nch)]
            ps, stats = [], []
            for idx in range(nch):
                m, l, _ = carry[3 * idx:3 * idx + 3]
                bc = sl2[idx // 2] * boff
                m_new = jnp.maximum(m, jnp.max(ss[idx], axis=0, keepdims=True) + bc)
                alpha = jnp.exp2(m - m_new)
                p = jnp.exp2(ss[idx] - (m_new - bc))
                stats.append((m_new, alpha * l + jnp.sum(p, axis=0, keepdims=True), alpha))
                ps.append(p.astype(BF16))
            out = []
            for idx in range(nch):
                m_new, l_new, alpha = stats[idx]
                acc = alpha * carry[3 * idx + 2] + jnp.dot(vt_scr[idx // 2, j], ps[idx],
                                                           preferred_element_type=F32)
                out.extend([m_new, l_new, acc])
            return tuple(out)

        state = lax.fori_loop(0, iq, kv_block, tuple(state))
        for hh in range(hpb):
            _, l0, a0, _, l1, a1 = state[6 * hh:6 * hh + 6]
            ot = a0 * (1.0 / l0) - a1 * (lam / l1)
            ot = ot * lax.rsqrt(jnp.mean(ot * ot, axis=0, keepdims=True) + EPS)
            o_ref[pl.ds(q0, tq), hh * hd2:(hh + 1) * hd2] = (ot.T * subg).astype(BF16)
        return 0

    lax.fori_loop(0, seq // tq, q_block, 0)


def _attention(q, k, v, lq1, lk1, lq2, lk2, subg, lam_init, batch, seq, *, tq=256, hpb=4):
    slopes = 2.0 ** (-8.0 * jnp.arange(1, N_HEADS + 1, dtype=F32) / N_HEADS)
    sl = slopes * math.log2(math.e)
    s_hi = sl.astype(BF16).astype(F32)
    s_mid = (sl - s_hi).astype(BF16).astype(F32)
    s_lo = (sl - s_hi - s_mid).astype(BF16).astype(F32)
    sl3 = jnp.stack([s_hi, s_mid, s_lo], axis=1)
    lam_arr = jnp.full((1,), lam_init, F32)
    smem = pl.BlockSpec(memory_space=pltpu.SMEM)
    hw = hpb * V_HEAD_DIM
    head_blk = pl.BlockSpec((seq, hw), lambda b, h: (b, h))
    vec = lambda n: pl.BlockSpec((1, n), lambda b, h: (0, 0))
    row = lambda a: a.astype(F32).reshape(1, -1)
    return pl.pallas_call(
        functools.partial(_attn_body, seq=seq, tq=tq, hpb=hpb),
        grid=(batch, N_HEADS // hpb),
        in_specs=[smem, smem, head_blk, head_blk, head_blk,
                  vec(HEAD_DIM), vec(HEAD_DIM), vec(HEAD_DIM), vec(HEAD_DIM), vec(V_HEAD_DIM)],
        out_specs=head_blk,
        out_shape=jax.ShapeDtypeStruct((batch * seq, D_V), BF16),
        scratch_shapes=[pltpu.VMEM((hpb, seq, 2 * V_HEAD_DIM), BF16),
                        pltpu.VMEM((hpb, seq // tq, V_HEAD_DIM, tq), BF16),
                        pltpu.VMEM((hpb, tq, tq), F32)],
        compiler_params=pltpu.CompilerParams(
            dimension_semantics=("parallel", "parallel"),
            vmem_limit_bytes=VMEM_LIMIT),
        name="diff_attention",
    )(sl3, lam_arr, q, k, v, row(lq1), row(lk1), row(lq2), row(lk2), row(subg))


def _ssm_body(u_ref, bblk_ref, are_ref, aim_ref, cblk_ref, d_ref, y_ref,
              x_scr, h_scr, st_scr, *, batch, tt):
    nc = st_scr.shape[-1] // 2
    gcols = u_ref.shape[-1]

    @pl.when(pl.program_id(1) == 0)
    def _():
        st_scr[...] = jnp.zeros_like(st_scr)

    u_f = pltpu.einshape("btn->tbn", u_ref[...].astype(F32)).reshape(tt * batch, gcols)
    x_scr[...] = jnp.dot(u_f.astype(BF16), bblk_ref[...], preferred_element_type=F32)
    ar = jnp.broadcast_to(are_ref[...], (batch, nc))
    ai = jnp.broadcast_to(aim_ref[...], (batch, nc))

    def step(t, carry):
        hr, hi = carry
        r0 = pl.multiple_of(t * batch, batch)
        xr = x_scr[pl.ds(r0, batch), :nc]
        xi = x_scr[pl.ds(r0, batch), nc:]
        nr = ar * hr - ai * hi + xr
        ni = ar * hi + ai * hr + xi
        h_scr[pl.ds(r0, batch), :nc] = nr.astype(BF16)
        h_scr[pl.ds(r0, batch), nc:] = ni.astype(BF16)
        return nr, ni

    hr, hi = lax.fori_loop(0, tt, step, (st_scr[:, :nc], st_scr[:, nc:]), unroll=4)
    st_scr[:, :nc] = hr
    st_scr[:, nc:] = hi
    y = jnp.dot(h_scr[...], cblk_ref[...], preferred_element_type=F32)
    y = _gelu_tanh(y + d_ref[...] * u_f)
    y_ref[...] = pltpu.einshape("tbn->btn", y.reshape(tt, batch, gcols)).astype(BF16)


def _ssm(u3, bblk, are, aim, cblk, d_skip, *, tt=128):
    batch, seq, _ = u3.shape
    gcols = GROUPS_PER_TILE * SSM_GROUP
    ngt = N_GROUPS // GROUPS_PER_TILE
    tt = min(tt, seq)
    rows = tt * batch
    return pl.pallas_call(
        functools.partial(_ssm_body, batch=batch, tt=tt),
        grid=(ngt, seq // tt),
        in_specs=[
            pl.BlockSpec((batch, tt, gcols), lambda g, t: (0, t, g)),
            pl.BlockSpec((None, gcols, 2 * SSM_NC), lambda g, t: (g, 0, 0)),
            pl.BlockSpec((None, 1, SSM_NC), lambda g, t: (g, 0, 0)),
            pl.BlockSpec((None, 1, SSM_NC), lambda g, t: (g, 0, 0)),
            pl.BlockSpec((None, 2 * SSM_NC, gcols), lambda g, t: (g, 0, 0)),
            pl.BlockSpec((1, gcols), lambda g, t: (0, g)),
        ],
        out_specs=pl.BlockSpec((batch, tt, gcols), lambda g, t: (0, t, g)),
        out_shape=jax.ShapeDtypeStruct((batch, seq, D_SSM), BF16),
        scratch_shapes=[pltpu.VMEM((rows, 2 * SSM_NC), F32),
                        pltpu.VMEM((rows, 2 * SSM_NC), BF16),
                        pltpu.VMEM((batch, 2 * SSM_NC), F32)],
        compiler_params=pltpu.CompilerParams(
            dimension_semantics=("parallel", "arbitrary"),
            vmem_limit_bytes=VMEM_LIMIT),
        name="s5_scan",
    )(u3, bblk, are, aim, cblk, d_skip.astype(F32).reshape(1, D_SSM))


def _ssm_params(a_re, a_im, log_dt, b_re, b_im, c_re, c_im):
    gpt = GROUPS_PER_TILE
    ngt = N_GROUPS // gpt
    dt = jnp.exp(log_dt)[:, None]
    mag = jnp.exp(dt * a_re)
    ab_re = mag * jnp.cos(dt * a_im)
    ab_im = mag * jnp.sin(dt * a_im)
    den = a_re * a_re + a_im * a_im
    zr = ab_re - 1.0
    zi = ab_im
    f_re = (zr * a_re + zi * a_im) / den
    f_im = (zi * a_re - zr * a_im) / den
    bb_re = f_re[..., None] * b_re - f_im[..., None] * b_im
    bb_im = f_re[..., None] * b_im + f_im[..., None] * b_re
    eye = jnp.eye(gpt, dtype=F32)

    def bdiag_in(bb):
        t = bb.reshape(ngt, gpt, SSM_STATE, SSM_GROUP)
        return jnp.einsum('tgpi,gh->tgihp', t, eye).reshape(ngt, gpt * SSM_GROUP, SSM_NC)

    def bdiag_out(cc):
        t = cc.reshape(ngt, gpt, SSM_GROUP, SSM_STATE)
        return jnp.einsum('tgop,gh->tgpho', t, eye).reshape(ngt, SSM_NC, gpt * SSM_GROUP)

    bblk = jnp.concatenate([bdiag_in(bb_re), bdiag_in(bb_im)], axis=-1).astype(BF16)
    cblk = jnp.concatenate([bdiag_out(c_re), -bdiag_out(c_im)], axis=1).astype(BF16)
    return bblk, ab_re.reshape(ngt, 1, SSM_NC), ab_im.reshape(ngt, 1, SSM_NC), cblk


def _mix_body(y_ref, oa_ref, gate_ref, x_ref, gw_ref, gb_ref, wba_ref, wbs_ref, wo_ref, out_ref):
    y = y_ref[...]
    z = jnp.dot(y, gw_ref[...], preferred_element_type=F32) + gb_ref[...]
    y2 = (y.astype(F32) * jax.nn.sigmoid(z)).astype(BF16)
    o_ssm = jnp.dot(y2, wbs_ref[...], preferred_element_type=F32)
    o_att = jnp.dot(oa_ref[...], wba_ref[...], preferred_element_type=F32)
    mixed = (jax.nn.sigmoid(gate_ref[:, :D_MODEL].astype(F32)) * o_att
             + jax.nn.sigmoid(gate_ref[:, D_MODEL:].astype(F32)) * o_ssm)
    out_ref[...] = x_ref[...] + jnp.dot(mixed.astype(BF16), wo_ref[...],
                                        preferred_element_type=F32)


def _mix(y, o_att, gates, xf, glu_w, glu_b, w_ba, w_bs, w_out, batch, seq, *, tm=256):
    t = xf.shape[0]
    tm = min(tm, seq)
    full = lambda shape: pl.BlockSpec(shape, lambda i: (0, 0), pipeline_mode=pl.Buffered(1))
    return pl.pallas_call(
        _mix_body,
        grid=(t // tm,),
        in_specs=[
            pl.BlockSpec((tm, D_SSM), lambda i: (i, 0)),
            pl.BlockSpec((tm, D_V), lambda i: (i, 0)),
            pl.BlockSpec((tm, 2 * D_MODEL), lambda i: (i, 0)),
            pl.BlockSpec((tm, D_MODEL), lambda i: (i, 0)),
            full((D_SSM, D_SSM)), full((1, D_SSM)),
            full((D_V, D_MODEL)), full((D_SSM, D_MODEL)), full((D_MODEL, D_MODEL)),
        ],
        out_specs=pl.BlockSpec((tm, D_MODEL), lambda i: (i, 0)),
        out_shape=jax.ShapeDtypeStruct((t, D_MODEL), F32),
        compiler_params=pltpu.CompilerParams(
            dimension_semantics=("parallel",),
            vmem_limit_bytes=VMEM_LIMIT),
        name="gated_merge",
    )(y, o_att, gates, xf, glu_w, glu_b.astype(F32).reshape(1, D_SSM), w_ba, w_bs, w_out)


def _shift_rows(up, prev, k):
    body = pltpu.roll(up, k, axis=0)
    top = pltpu.roll(jnp.concatenate([prev, up[:8]], axis=0), k, axis=0)[8:]
    return jnp.concatenate([top, body[8:]], axis=0)


def _ffn_up_body(x_ref, g_ref, wa_ref, wv_ref, cwa_ref, cwv_ref, cba_ref, cbv_ref,
                 act_ref, h_scr, carry_a, carry_v, *, nsb):
    i = pl.program_id(0)
    j = pl.program_id(1)

    @pl.when(j == 0)
    def _():
        h_scr[...] = _rms_rows(x_ref[...], g_ref[...]).astype(BF16)

    h = h_scr[...]
    seq_start = i % nsb == 0

    def conv(w_ref, cw_ref, cb_ref, carry):
        up = jnp.dot(h, w_ref[...], preferred_element_type=F32)
        prev = jnp.where(seq_start, 0.0, carry[j])
        carry[j] = up[up.shape[0] - 8:]
        cw = cw_ref[...]
        return (cb_ref[...] + cw[0:1] * _shift_rows(up, prev, 2)
                + cw[1:2] * _shift_rows(up, prev, 1) + cw[2:3] * up)

    a = conv(wa_ref, cwa_ref, cba_ref, carry_a)
    val = conv(wv_ref, cwv_ref, cbv_ref, carry_v)
    act_ref[...] = (_gelu_tanh(a) * val).astype(BF16)


def _ffn_up(xf, g, wa, wv, cwa, cwv, cba, cbv, seq, *, tm=512, tn=FF_TILE):
    t = xf.shape[0]
    tm = min(tm, seq)
    ncol = D_FF_PAD // tn
    const = lambda i, j: (0, 0)
    col = lambda i, j: (0, j)
    return pl.pallas_call(
        functools.partial(_ffn_up_body, nsb=seq // tm),
        grid=(t // tm, ncol),
        in_specs=[
            pl.BlockSpec((tm, D_MODEL), lambda i, j: (i, 0)),
            pl.BlockSpec((1, D_MODEL), const),
            pl.BlockSpec((D_MODEL, tn), col),
            pl.BlockSpec((D_MODEL, tn), col),
            pl.BlockSpec((CONV_W, tn), col),
            pl.BlockSpec((CONV_W, tn), col),
            pl.BlockSpec((1, tn), col),
            pl.BlockSpec((1, tn), col),
        ],
        out_specs=pl.BlockSpec((tm, tn), lambda i, j: (i, j)),
        out_shape=jax.ShapeDtypeStruct((t, D_FF_PAD), BF16),
        scratch_shapes=[pltpu.VMEM((tm, D_MODEL), BF16),
                        pltpu.VMEM((ncol, 8, tn), F32),
                        pltpu.VMEM((ncol, 8, tn), F32)],
        compiler_params=pltpu.CompilerParams(
            dimension_semantics=("arbitrary", "arbitrary"),
            vmem_limit_bytes=VMEM_LIMIT),
        name="ffn_up_conv_gate",
    )(xf, g.reshape(1, D_MODEL), wa, wv, cwa, cwv, cba, cbv)


def _ffn_down_body(act_ref, w_ref, x_ref, out_ref):
    out_ref[...] = x_ref[...] + jnp.dot(act_ref[...], w_ref[...], preferred_element_type=F32)


def _ffn_down(act, w_down, xf, *, tm=1024, tn=512):
    t = xf.shape[0]
    tm = min(tm, t)
    return pl.pallas_call(
        _ffn_down_body,
        grid=(t // tm, D_MODEL // tn),
        in_specs=[
            pl.BlockSpec((tm, D_FF_PAD), lambda i, j: (i, 0)),
            pl.BlockSpec((D_FF_PAD, tn), lambda i, j: (0, j)),
            pl.BlockSpec((tm, tn), lambda i, j: (i, j)),
        ],
        out_specs=pl.BlockSpec((tm, tn), lambda i, j: (i, j)),
        out_shape=jax.ShapeDtypeStruct((t, D_MODEL), F32),
        compiler_params=pltpu.CompilerParams(
            dimension_semantics=("parallel", "parallel"),
            vmem_limit_bytes=VMEM_LIMIT),
        name="ffn_down",
    )(act, w_down, xf)


def _ffn_body(x_ref, g_ref, wa_ref, wv_ref, cwa_ref, cwv_ref, cba_ref, cbv_ref, wd_ref,
              out_ref, h_scr, act_scr, acc_scr, raw_a, raw_v, carry_a, carry_v, *, batch, tt, ncol):
    i = pl.program_id(0)
    j = pl.program_id(1)
    slot = j % 2
    rows = tt * batch
    hist = (CONV_W - 1) * batch

    @pl.when(j == 0)
    def _():
        x_tb = pltpu.einshape("btn->tbn", x_ref[...]).reshape(rows, D_MODEL)
        h_scr[...] = _rms_rows(x_tb, g_ref[...]).astype(BF16)
        acc_scr[...] = jnp.zeros_like(acc_scr)

    nsub = wa_ref.shape[1] // MXU_DIM
    sub = lambda c: slice(c * MXU_DIM, (c + 1) * MXU_DIM)

    def up_dots():
        h = h_scr[...]
        for c in range(nsub):
            raw_a[:, sub(c)] = jnp.dot(h, wa_ref[:, sub(c)], preferred_element_type=F32)
            raw_v[:, sub(c)] = jnp.dot(h, wv_ref[:, sub(c)], preferred_element_type=F32)

    def down_dot():
        return jnp.dot(act_scr[1 - slot], wd_ref[...], preferred_element_type=F32)

    def gate():
        def conv(raw, c, cw_ref, cb_ref, carry):
            up = raw[:, sub(c)]
            prev = jnp.where(i == 0, 0.0, carry[j, :, sub(c)])
            carry[j, :, sub(c)] = up[rows - hist:]
            ext = jnp.concatenate([prev, up], axis=0)
            cw = cw_ref[:, sub(c)]
            out = cb_ref[:, sub(c)] + cw[CONV_W - 1:CONV_W] * up
            for tap in range(CONV_W - 1):
                out = out + cw[tap:tap + 1] * ext[tap * batch:tap * batch + rows]
            return out

        for c in range(nsub):
            a = conv(raw_a, c, cwa_ref, cba_ref, carry_a)
            val = conv(raw_v, c, cwv_ref, cbv_ref, carry_v)
            act_scr[slot, :, sub(c)] = (_gelu_tanh(a) * val).astype(BF16)

    @pl.when(j == 0)
    def _():
        up_dots()
        gate()

    @pl.when((j > 0) & (j < ncol))
    def _():
        up_dots()
        acc_scr[...] += down_dot()
        gate()

    @pl.when(j == ncol)
    def _():
        y = acc_scr[...] + down_dot()
        out_ref[...] = x_ref[...] + pltpu.einshape("tbn->btn", y.reshape(tt, batch, D_MODEL))


def _ffn(x3, g, wa, wv, cwa, cwv, cba, cbv, w_down, *, tm=512, tn=FF_TILE):
    batch, seq, _ = x3.shape
    tt = min(tm // batch, seq)
    rows = tt * batch
    ncol = D_FF_PAD // tn
    hist = (CONV_W - 1) * batch
    const = lambda i, j: (0, 0)
    col = lambda i, j: (0, jnp.minimum(j, ncol - 1))
    return pl.pallas_call(
        functools.partial(_ffn_body, batch=batch, tt=tt, ncol=ncol),
        grid=(seq // tt, ncol + 1),
        in_specs=[
            pl.BlockSpec((batch, tt, D_MODEL), lambda i, j: (0, i, 0)),
            pl.BlockSpec((1, D_MODEL), const),
            pl.BlockSpec((D_MODEL, tn), col),
            pl.BlockSpec((D_MODEL, tn), col),
            pl.BlockSpec((CONV_W, tn), col),
            pl.BlockSpec((CONV_W, tn), col),
            pl.BlockSpec((1, tn), col),
            pl.BlockSpec((1, tn), col),
            pl.BlockSpec((tn, D_MODEL), lambda i, j: (jnp.maximum(j - 1, 0), 0)),
        ],
        out_specs=pl.BlockSpec((batch, tt, D_MODEL), lambda i, j: (0, i, 0)),
        out_shape=jax.ShapeDtypeStruct((batch, seq, D_MODEL), F32),
        scratch_shapes=[pltpu.VMEM((rows, D_MODEL), BF16),
                        pltpu.VMEM((2, rows, tn), BF16),
                        pltpu.VMEM((rows, D_MODEL), F32),
                        pltpu.VMEM((rows, tn), F32), pltpu.VMEM((rows, tn), F32),
                        pltpu.VMEM((ncol, hist, tn), F32),
                        pltpu.VMEM((ncol, hist, tn), F32)],
        compiler_params=pltpu.CompilerParams(
            dimension_semantics=("arbitrary", "arbitrary"),
            vmem_limit_bytes=VMEM_LIMIT),
        name="ffn_fused",
    )(x3, g.reshape(1, D_MODEL), wa, wv, cwa, cwv, cba, cbv, w_down)


def _pad_ff_cols(a):
    return jnp.pad(a, ((0, 0), (0, D_FF_PAD - D_FF)))


def kernel(x, norm1_g, w_in, q_norm_g, k_norm_g, lambda_q1, lambda_k1, lambda_q2, lambda_k2, subln_g, ssm_a_re, ssm_a_im, ssm_log_dt, ssm_b_re, ssm_b_im, ssm_c_re, ssm_c_im, ssm_d, ssm_glu_w, ssm_glu_b, w_branch_attn, w_branch_ssm, w_out, norm2_g, ffn_w_up, ffn_conv_w, ffn_conv_b, ffn_w_down):
    batch, seq, _ = x.shape
    depth = w_in.shape[0]
    xf = x.reshape(batch * seq, D_MODEL)
    for l in range(depth):
        lam_init = 0.8 - 0.6 * math.exp(-0.3 * l)
        q, k, v, u, gates = _in_proj(xf, norm1_g[l], w_in[l].astype(BF16),
                                     q_norm_g[l], k_norm_g[l], batch, seq)
        o_att = _attention(q, k, v, lambda_q1[l], lambda_k1[l], lambda_q2[l], lambda_k2[l],
                           subln_g[l], lam_init, batch, seq)
        bblk, are, aim, cblk = _ssm_params(ssm_a_re[l], ssm_a_im[l], ssm_log_dt[l],
                                           ssm_b_re[l], ssm_b_im[l], ssm_c_re[l], ssm_c_im[l])
        y = _ssm(u.reshape(batch, seq, D_SSM), bblk, are, aim, cblk, ssm_d[l])
        xf = _mix(y.reshape(batch * seq, D_SSM), o_att, gates, xf,
                  ssm_glu_w[l].astype(BF16), ssm_glu_b[l],
                  w_branch_attn[l].astype(BF16), w_branch_ssm[l].astype(BF16),
                  w_out[l].astype(BF16), batch, seq)
        wu = ffn_w_up[l]
        cw = ffn_conv_w[l]
        cb = ffn_conv_b[l].reshape(1, 2 * D_FF)
        w_down = jnp.pad(ffn_w_down[l], ((0, D_FF_PAD - D_FF), (0, 0))).astype(BF16)
        x3 = _ffn(xf.reshape(batch, seq, D_MODEL), norm2_g[l],
                  _pad_ff_cols(wu[:, :D_FF]).astype(BF16), _pad_ff_cols(wu[:, D_FF:]).astype(BF16),
                  _pad_ff_cols(cw[:, :D_FF]), _pad_ff_cols(cw[:, D_FF:]),
                  _pad_ff_cols(cb[:, :D_FF]), _pad_ff_cols(cb[:, D_FF:]), w_down)
        xf = x3.reshape(batch * seq, D_MODEL)
    return xf.reshape(batch, seq, D_MODEL)
```

```python
import functools
import math

import jax
import jax.numpy as jnp
from jax import lax
from jax.experimental import pallas as pl
from jax.experimental.pallas import tpu as pltpu

F32 = jnp.float32
BF16 = jnp.bfloat16

D_MODEL = 2048
N_HEADS = 8
HEAD_DIM = 64
V_HEAD_DIM = 2 * HEAD_DIM
D_QK = N_HEADS * 2 * HEAD_DIM
D_V = N_HEADS * V_HEAD_DIM
D_SSM = D_MODEL // 2
SSM_GROUP = 16
N_GROUPS = D_SSM // SSM_GROUP
SSM_STATE = 64
D_FF = 5504
CONV_W = 3
CHUNK = 64
EPS = 1e-6
IN_COLS = 2 * D_QK + D_V + D_SSM + 2 * D_MODEL

MXU_DIM = 256
FF_TILE = 512
ROW_CHUNK = 128
D_FF_PAD = ((D_FF + FF_TILE - 1) // FF_TILE) * FF_TILE
GROUPS_PER_TILE = 8
SSM_NC = GROUPS_PER_TILE * SSM_STATE
NEG_BIG = -1e30
Q_SCALE = HEAD_DIM ** -0.5 * math.log2(math.e)
VMEM_LIMIT = 56 * 1024 * 1024


def _gelu_tanh(x):
    c = math.sqrt(2.0 / math.pi)
    return x * (0.5 * (1.0 + jnp.tanh(c * (x + 0.044715 * (x * x * x)))))


def _rms_rows(xf, g):
    ms = jnp.mean(xf * xf, axis=-1, keepdims=True)
    return xf * lax.rsqrt(ms + EPS) * g


def _in_proj_body(x_ref, g_ref, w_ref, qg_ref, kg_ref, ones_ref,
                  q_ref, k_ref, v_ref, u_ref, gate_ref, h_scr, *, tn):
    j = pl.program_id(1)
    nq = D_QK // tn

    @pl.when(j == 0)
    def _():
        h_scr[...] = _rms_rows(x_ref[...], g_ref[...]).astype(BF16)

    tm = h_scr.shape[0]
    rc = min(ROW_CHUNK, tm)

    def tile(out_ref, epilogue):
        nchunk = tm // rc
        rows = lambda q: slice(q * rc, (q + 1) * rc)
        dot_q = lambda q: jnp.dot(h_scr[rows(q), :], w_ref[...], preferred_element_type=F32)
        acc = dot_q(0)
        for q in range(1, nchunk):
            nxt = dot_q(q)
            out_ref[rows(q - 1), :] = epilogue(acc).astype(BF16)
            acc = nxt
        out_ref[rows(nchunk - 1), :] = epilogue(acc).astype(BF16)

    def head_norm(gain_ref, scale):
        def epilogue(acc):
            outs = []
            for c in range(tn // MXU_DIM):
                a = acc[:, c * MXU_DIM:(c + 1) * MXU_DIM]
                ss = jnp.dot((a * a).astype(BF16), ones_ref[...], preferred_element_type=F32)
                outs.append(a * lax.rsqrt(ss * (1.0 / HEAD_DIM) + EPS) * (gain_ref[...] * scale))
            return jnp.concatenate(outs, axis=1)
        return epilogue

    plain = lambda acc: acc

    @pl.when(j < nq)
    def _():
        tile(q_ref, head_norm(qg_ref, Q_SCALE))

    @pl.when((j >= nq) & (j < 2 * nq))
    def _():
        tile(k_ref, head_norm(kg_ref, 1.0))

    @pl.when((j >= 2 * nq) & (j < 3 * nq))
    def _():
        tile(v_ref, plain)

    @pl.when((j >= 3 * nq) & (j < 4 * nq))
    def _():
        tile(u_ref, plain)

    @pl.when(j >= 4 * nq)
    def _():
        tile(gate_ref, plain)


def _in_proj(xf, g, w_bf, qg, kg, batch, seq, *, tm=1024, tn=512):
    t = xf.shape[0]
    tm = min(tm, seq)
    nq = D_QK // tn
    ng = 2 * D_MODEL // tn
    qg_t = jnp.tile(qg.astype(F32), MXU_DIM // HEAD_DIM).reshape(1, MXU_DIM)
    kg_t = jnp.tile(kg.astype(F32), MXU_DIM // HEAD_DIM).reshape(1, MXU_DIM)
    seg = jnp.arange(MXU_DIM) // HEAD_DIM
    ones_bd = (seg[:, None] == seg[None, :]).astype(BF16)

    def cl(j, lo, n):
        return jnp.clip(j - lo, 0, n - 1)

    const = lambda i, j: (0, 0)
    return pl.pallas_call(
        functools.partial(_in_proj_body, tn=tn),
        grid=(t // tm, IN_COLS // tn),
        in_specs=[
            pl.BlockSpec((tm, D_MODEL), lambda i, j: (i, 0)),
            pl.BlockSpec((1, D_MODEL), const),
            pl.BlockSpec((D_MODEL, tn), lambda i, j: (0, j)),
            pl.BlockSpec((1, MXU_DIM), const),
            pl.BlockSpec((1, MXU_DIM), const),
            pl.BlockSpec((MXU_DIM, MXU_DIM), const),
        ],
        out_specs=[
            pl.BlockSpec((tm, tn), lambda i, j: (i, cl(j, 0, nq))),
            pl.BlockSpec((tm, tn), lambda i, j: (i, cl(j, nq, nq))),
            pl.BlockSpec((tm, tn), lambda i, j: (i, cl(j, 2 * nq, nq))),
            pl.BlockSpec((tm, tn), lambda i, j: (i, cl(j, 3 * nq, nq))),
            pl.BlockSpec((tm, tn), lambda i, j: (i, cl(j, 4 * nq, ng))),
        ],
        out_shape=[
            jax.ShapeDtypeStruct((t, D_QK), BF16),
            jax.ShapeDtypeStruct((t, D_QK), BF16),
            jax.ShapeDtypeStruct((t, D_V), BF16),
            jax.ShapeDtypeStruct((t, D_SSM), BF16),
            jax.ShapeDtypeStruct((t, 2 * D_MODEL), BF16),
        ],
        scratch_shapes=[pltpu.VMEM((tm, D_MODEL), BF16)],
        compiler_params=pltpu.CompilerParams(
            dimension_semantics=("arbitrary", "arbitrary"),
            vmem_limit_bytes=VMEM_LIMIT),
        name="in_proj",
    )(xf, g.reshape(1, D_MODEL), w_bf, qg_t, kg_t, ones_bd)


N_BIAS_ROWS = 3


def _attn_body(sl_ref, laminit_ref, q_ref, k_ref, v_ref, lq1_ref, lk1_ref, lq2_ref, lk2_ref,
               subg_ref, o_ref, kaug_scr, vt_scr, dmat_scr, *, seq, tq, hpb):
    tk = tq
    nblk = seq // tk
    hd2 = 2 * HEAD_DIM
    nch = 2 * hpb
    hg = pl.program_id(1)
    lam_init = laminit_ref[0]
    lam = (jnp.exp(jnp.sum(lq1_ref[...] * lk1_ref[...], axis=-1, keepdims=True))
           - jnp.exp(jnp.sum(lq2_ref[...] * lk2_ref[...], axis=-1, keepdims=True))
           + lam_init)
    subg = subg_ref[...] * (1.0 - lam_init)

    koff = lax.broadcasted_iota(jnp.int32, (seq, hd2), 0) % tk
    klane = lax.broadcasted_iota(jnp.int32, (seq, hd2), 1)
    kbias = jnp.where(klane < N_BIAS_ROWS, koff, 0).astype(F32).astype(BF16)
    c = lax.broadcasted_iota(jnp.int32, (tk, tq), 0)
    r = lax.broadcasted_iota(jnp.int32, (tk, tq), 1)
    visible = (c // CHUNK) <= (r // CHUNK)
    ahead = jnp.maximum(c - r, 0).astype(F32)
    brow_i = lax.broadcasted_iota(jnp.int32, (hd2, tq), 0)
    sl2 = []
    brows = []
    for hh in range(hpb):
        h = hg * hpb + hh
        s_hi, s_mid, s_lo = sl_ref[h, 0], sl_ref[h, 1], sl_ref[h, 2]
        sl2.append(s_hi + s_mid + s_lo)
        brows.append(jnp.where(brow_i == 0, s_hi, jnp.where(brow_i == 1, s_mid,
                     jnp.where(brow_i == 2, s_lo, 0.0))).astype(BF16))
        kaug_scr[hh, :, 0:hd2] = k_ref[:, hh * hd2:(hh + 1) * hd2]
        kaug_scr[hh, :, hd2:2 * hd2] = kbias
        for blk in range(nblk):
            vt_scr[hh, blk] = v_ref[blk * tk:(blk + 1) * tk,
                                    hh * hd2:(hh + 1) * hd2].astype(F32).T.astype(BF16)
        dmat_scr[hh] = jnp.where(visible, (-2.0 * sl2[hh]) * ahead, NEG_BIG)

    def q_block(iq, _):
        q0 = pl.multiple_of(iq * tq, tq)
        qas = []
        for hh in range(hpb):
            qt = q_ref[pl.ds(q0, tq), hh * hd2:(hh + 1) * hd2].astype(F32).T
            for comp in range(2):
                keep = (brow_i < HEAD_DIM) if comp == 0 else (brow_i >= HEAD_DIM)
                qas.append(jnp.concatenate([jnp.where(keep, qt, 0.0).astype(BF16), brows[hh]],
                                           axis=0))
        ss = [jnp.dot(kaug_scr[idx // 2, pl.ds(q0, tk), :], qas[idx],
                      preferred_element_type=F32) + dmat_scr[idx // 2] for idx in range(nch)]
        ps, stats = [], []
        for idx in range(nch):
            m = jnp.max(ss[idx], axis=0, keepdims=True)
            p = jnp.exp2(ss[idx] - m)
            stats.append((m, jnp.sum(p, axis=0, keepdims=True)))
            ps.append(p.astype(BF16))
        state = []
        for idx in range(nch):
            acc = jnp.dot(vt_scr[idx // 2, iq], ps[idx], preferred_element_type=F32)
            state.extend([stats[idx][0], stats[idx][1], acc])

        def kv_block(j, carry):
            k0 = pl.multiple_of(j * tk, tk)
            boff = lax.convert_element_type((j - iq) * tk, F32)
            ss = [jnp.dot(kaug_scr[idx // 2, pl.ds(k0, tk), :], qas[idx],
                          preferred_element_type=F32) for idx in range(nch)]
            ps, stats = [], []
            for idx in range(nch):
                m, l, _ = carry[3 * idx:3 * idx + 3]
                bc = sl2[idx // 2] * boff
                m_new = jnp.maximum(m, jnp.max(ss[idx], axis=0, keepdims=True) + bc)
                alpha = jnp.exp2(m - m_new)
                p = jnp.exp2(ss[idx] - (m_new - bc))
                stats.append((m_new, alpha * l + jnp.sum(p, axis=0, keepdims=True), alpha))
                ps.append(p.astype(BF16))
            out = []
            for idx in range(nch):
                m_new, l_new, alpha = stats[idx]
                acc = alpha * carry[3 * idx + 2] + jnp.dot(vt_scr[idx // 2, j], ps[idx],
                                                           preferred_element_type=F32)
                out.extend([m_new, l_new, acc])
            return tuple(out)

        state = lax.fori_loop(0, iq, kv_block, tuple(state))
        for hh in range(hpb):
            _, l0, a0, _, l1, a1 = state[6 * hh:6 * hh + 6]
            ot = a0 * (1.0 / l0) - a1 * (lam / l1)
            ot = ot * lax.rsqrt(jnp.mean(ot * ot, axis=0, keepdims=True) + EPS)
            o_ref[pl.ds(q0, tq), hh * hd2:(hh + 1) * hd2] = (ot.T * subg).astype(BF16)
        return 0

    lax.fori_loop(0, seq // tq, q_block, 0)


def _attention(q, k, v, lq1, lk1, lq2, lk2, subg, lam_init, batch, seq, *, tq=256, hpb=4):
    slopes = 2.0 ** (-8.0 * jnp.arange(1, N_HEADS + 1, dtype=F32) / N_HEADS)
    sl = slopes * math.log2(math.e)
    s_hi = sl.astype(BF16).astype(F32)
    s_mid = (sl - s_hi).astype(BF16).astype(F32)
    s_lo = (sl - s_hi - s_mid).astype(BF16).astype(F32)
    sl3 = jnp.stack([s_hi, s_mid, s_lo], axis=1)
    lam_arr = jnp.full((1,), lam_init, F32)
    smem = pl.BlockSpec(memory_space=pltpu.SMEM)
    hw = hpb * V_HEAD_DIM
    head_blk = pl.BlockSpec((seq, hw), lambda b, h: (b, h))
    vec = lambda n: pl.BlockSpec((1, n), lambda b, h: (0, 0))
    row = lambda a: a.astype(F32).reshape(1, -1)
    return pl.pallas_call(
        functools.partial(_attn_body, seq=seq, tq=tq, hpb=hpb),
        grid=(batch, N_HEADS // hpb),
        in_specs=[smem, smem, head_blk, head_blk, head_blk,
                  vec(HEAD_DIM), vec(HEAD_DIM), vec(HEAD_DIM), vec(HEAD_DIM), vec(V_HEAD_DIM)],
        out_specs=head_blk,
        out_shape=jax.ShapeDtypeStruct((batch * seq, D_V), BF16),
        scratch_shapes=[pltpu.VMEM((hpb, seq, 2 * V_HEAD_DIM), BF16),
                        pltpu.VMEM((hpb, seq // tq, V_HEAD_DIM, tq), BF16),
                        pltpu.VMEM((hpb, tq, tq), F32)],
        compiler_params=pltpu.CompilerParams(
            dimension_semantics=("parallel", "parallel"),
            vmem_limit_bytes=VMEM_LIMIT),
        name="diff_attention",
    )(sl3, lam_arr, q, k, v, row(lq1), row(lk1), row(lq2), row(lk2), row(subg))


def _ssm_body(u_ref, bblk_ref, are_ref, aim_ref, cblk_ref, d_ref, y_ref,
              x_scr, h_scr, st_scr, *, batch, tt):
    nc = st_scr.shape[-1] // 2
    gcols = u_ref.shape[-1]

    @pl.when(pl.program_id(1) == 0)
    def _():
        st_scr[...] = jnp.zeros_like(st_scr)

    u_f = pltpu.einshape("btn->tbn", u_ref[...].astype(F32)).reshape(tt * batch, gcols)
    x_scr[...] = jnp.dot(u_f.astype(BF16), bblk_ref[...], preferred_element_type=F32)
    ar = jnp.broadcast_to(are_ref[...], (batch, nc))
    ai = jnp.broadcast_to(aim_ref[...], (batch, nc))

    def step(t, carry):
        hr, hi = carry
        r0 = pl.multiple_of(t * batch, batch)
        xr = x_scr[pl.ds(r0, batch), :nc]
        xi = x_scr[pl.ds(r0, batch), nc:]
        nr = ar * hr - ai * hi + xr
        ni = ar * hi + ai * hr + xi
        h_scr[pl.ds(r0, batch), :nc] = nr.astype(BF16)
        h_scr[pl.ds(r0, batch), nc:] = ni.astype(BF16)
        return nr, ni

    hr, hi = lax.fori_loop(0, tt, step, (st_scr[:, :nc], st_scr[:, nc:]), unroll=4)
    st_scr[:, :nc] = hr
    st_scr[:, nc:] = hi
    y = jnp.dot(h_scr[...], cblk_ref[...], preferred_element_type=F32)
    y = _gelu_tanh(y + d_ref[...] * u_f)
    y_ref[...] = pltpu.einshape("tbn->btn", y.reshape(tt, batch, gcols)).astype(BF16)


def _ssm(u3, bblk, are, aim, cblk, d_skip, *, tt=128):
    batch, seq, _ = u3.shape
    gcols = GROUPS_PER_TILE * SSM_GROUP
    ngt = N_GROUPS // GROUPS_PER_TILE
    tt = min(tt, seq)
    rows = tt * batch
    return pl.pallas_call(
        functools.partial(_ssm_body, batch=batch, tt=tt),
        grid=(ngt, seq // tt),
        in_specs=[
            pl.BlockSpec((batch, tt, gcols), lambda g, t: (0, t, g)),
            pl.BlockSpec((None, gcols, 2 * SSM_NC), lambda g, t: (g, 0, 0)),
            pl.BlockSpec((None, 1, SSM_NC), lambda g, t: (g, 0, 0)),
            pl.BlockSpec((None, 1, SSM_NC), lambda g, t: (g, 0, 0)),
            pl.BlockSpec((None, 2 * SSM_NC, gcols), lambda g, t: (g, 0, 0)),
            pl.BlockSpec((1, gcols), lambda g, t: (0, g)),
        ],
        out_specs=pl.BlockSpec((batch, tt, gcols), lambda g, t: (0, t, g)),
        out_shape=jax.ShapeDtypeStruct((batch, seq, D_SSM), BF16),
        scratch_shapes=[pltpu.VMEM((rows, 2 * SSM_NC), F32),
                        pltpu.VMEM((rows, 2 * SSM_NC), BF16),
                        pltpu.VMEM((batch, 2 * SSM_NC), F32)],
        compiler_params=pltpu.CompilerParams(
            dimension_semantics=("parallel", "arbitrary"),
            vmem_limit_bytes=VMEM_LIMIT),
        name="s5_scan",
    )(u3, bblk, are, aim, cblk, d_skip.astype(F32).reshape(1, D_SSM))


def _ssm_params(a_re, a_im, log_dt, b_re, b_im, c_re, c_im):
    gpt = GROUPS_PER_TILE
    ngt = N_GROUPS // gpt
    dt = jnp.exp(log_dt)[:, None]
    mag = jnp.exp(dt * a_re)
    ab_re = mag * jnp.cos(dt * a_im)
    ab_im = mag * jnp.sin(dt * a_im)
    den = a_re * a_re + a_im * a_im
    zr = ab_re - 1.0
    zi = ab_im
    f_re = (zr * a_re + zi * a_im) / den
    f_im = (zi * a_re - zr * a_im) / den
    bb_re = f_re[..., None] * b_re - f_im[..., None] * b_im
    bb_im = f_re[..., None] * b_im + f_im[..., None] * b_re
    eye = jnp.eye(gpt, dtype=F32)

    def bdiag_in(bb):
        t = bb.reshape(ngt, gpt, SSM_STATE, SSM_GROUP)
        return jnp.einsum('tgpi,gh->tgihp', t, eye).reshape(ngt, gpt * SSM_GROUP, SSM_NC)

    def bdiag_out(cc):
        t = cc.reshape(ngt, gpt, SSM_GROUP, SSM_STATE)
        return jnp.einsum('tgop,gh->tgpho', t, eye).reshape(ngt, SSM_NC, gpt * SSM_GROUP)

    bblk = jnp.concatenate([bdiag_in(bb_re), bdiag_in(bb_im)], axis=-1).astype(BF16)
    cblk = jnp.concatenate([bdiag_out(c_re), -bdiag_out(c_im)], axis=1).astype(BF16)
    return bblk, ab_re.reshape(ngt, 1, SSM_NC), ab_im.reshape(ngt, 1, SSM_NC), cblk


def _mix_body(y_ref, oa_ref, gate_ref, x_ref, gw_ref, gb_ref, wba_ref, wbs_ref, wo_ref, out_ref):
    y = y_ref[...]
    z = jnp.dot(y, gw_ref[...], preferred_element_type=F32) + gb_ref[...]
    y2 = (y.astype(F32) * jax.nn.sigmoid(z)).astype(BF16)
    o_ssm = jnp.dot(y2, wbs_ref[...], preferred_element_type=F32)
    o_att = jnp.dot(oa_ref[...], wba_ref[...], preferred_element_type=F32)
    mixed = (jax.nn.sigmoid(gate_ref[:, :D_MODEL].astype(F32)) * o_att
             + jax.nn.sigmoid(gate_ref[:, D_MODEL:].astype(F32)) * o_ssm)
    out_ref[...] = x_ref[...] + jnp.dot(mixed.astype(BF16), wo_ref[...],
                                        preferred_element_type=F32)


def _mix(y, o_att, gates, xf, glu_w, glu_b, w_ba, w_bs, w_out, batch, seq, *, tm=256):
    t = xf.shape[0]
    tm = min(tm, seq)
    full = lambda shape: pl.BlockSpec(shape, lambda i: (0, 0), pipeline_mode=pl.Buffered(1))
    return pl.pallas_call(
        _mix_body,
        grid=(t // tm,),
        in_specs=[
            pl.BlockSpec((tm, D_SSM), lambda i: (i, 0)),
            pl.BlockSpec((tm, D_V), lambda i: (i, 0)),
            pl.BlockSpec((tm, 2 * D_MODEL), lambda i: (i, 0)),
            pl.BlockSpec((tm, D_MODEL), lambda i: (i, 0)),
            full((D_SSM, D_SSM)), full((1, D_SSM)),
            full((D_V, D_MODEL)), full((D_SSM, D_MODEL)), full((D_MODEL, D_MODEL)),
        ],
        out_specs=pl.BlockSpec((tm, D_MODEL), lambda i: (i, 0)),
        out_shape=jax.ShapeDtypeStruct((t, D_MODEL), F32),
        compiler_params=pltpu.CompilerParams(
            dimension_semantics=("parallel",),
            vmem_limit_bytes=VMEM_LIMIT),
        name="gated_merge",
    )(y, o_att, gates, xf, glu_w, glu_b.astype(F32).reshape(1, D_SSM), w_ba, w_bs, w_out)


def _shift_rows(up, prev, k):
    body = pltpu.roll(up, k, axis=0)
    top = pltpu.roll(jnp.concatenate([prev, up[:8]], axis=0), k, axis=0)[8:]
    return jnp.concatenate([top, body[8:]], axis=0)


def _ffn_up_body(x_ref, g_ref, wa_ref, wv_ref, cwa_ref, cwv_ref, cba_ref, cbv_ref,
                 act_ref, h_scr, carry_a, carry_v, *, nsb):
    i = pl.program_id(0)
    j = pl.program_id(1)

    @pl.when(j == 0)
    def _():
        h_scr[...] = _rms_rows(x_ref[...], g_ref[...]).astype(BF16)

    seq_start = i % nsb == 0
    tm = h_scr.shape[0]
    rc = min(ROW_CHUNK, tm)
    prev_a = jnp.where(seq_start, 0.0, carry_a[j])
    prev_v = jnp.where(seq_start, 0.0, carry_v[j])

    def conv(up, prev, cw_ref, cb_ref):
        cw = cw_ref[...]
        return (cb_ref[...] + cw[0:1] * _shift_rows(up, prev, 2)
                + cw[1:2] * _shift_rows(up, prev, 1) + cw[2:3] * up)

    for q in range(tm // rc):
        h = h_scr[q * rc:(q + 1) * rc, :]
        up_a = jnp.dot(h, wa_ref[...], preferred_element_type=F32)
        up_v = jnp.dot(h, wv_ref[...], preferred_element_type=F32)
        a = conv(up_a, prev_a, cwa_ref, cba_ref)
        val = conv(up_v, prev_v, cwv_ref, cbv_ref)
        act_ref[q * rc:(q + 1) * rc, :] = (_gelu_tanh(a) * val).astype(BF16)
        prev_a, prev_v = up_a[rc - 8:], up_v[rc - 8:]
    carry_a[j] = prev_a
    carry_v[j] = prev_v


def _ffn_up(xf, g, wa, wv, cwa, cwv, cba, cbv, seq, *, tm=512, tn=FF_TILE):
    t = xf.shape[0]
    tm = min(tm, seq)
    ncol = D_FF_PAD // tn
    const = lambda i, j: (0, 0)
    col = lambda i, j: (0, j)
    return pl.pallas_call(
        functools.partial(_ffn_up_body, nsb=seq // tm),
        grid=(t // tm, ncol),
        in_specs=[
            pl.BlockSpec((tm, D_MODEL), lambda i, j: (i, 0)),
            pl.BlockSpec((1, D_MODEL), const),
            pl.BlockSpec((D_MODEL, tn), col),
            pl.BlockSpec((D_MODEL, tn), col),
            pl.BlockSpec((CONV_W, tn), col),
            pl.BlockSpec((CONV_W, tn), col),
            pl.BlockSpec((1, tn), col),
            pl.BlockSpec((1, tn), col),
        ],
        out_specs=pl.BlockSpec((tm, tn), lambda i, j: (i, j)),
        out_shape=jax.ShapeDtypeStruct((t, D_FF_PAD), BF16),
        scratch_shapes=[pltpu.VMEM((tm, D_MODEL), BF16),
                        pltpu.VMEM((ncol, 8, tn), F32),
                        pltpu.VMEM((ncol, 8, tn), F32)],
        compiler_params=pltpu.CompilerParams(
            dimension_semantics=("arbitrary", "arbitrary"),
            vmem_limit_bytes=VMEM_LIMIT),
        name="ffn_up_conv_gate",
    )(xf, g.reshape(1, D_MODEL), wa, wv, cwa, cwv, cba, cbv)


def _ffn_down_body(act_ref, w_ref, x_ref, out_ref):
    out_ref[...] = x_ref[...] + jnp.dot(act_ref[...], w_ref[...], preferred_element_type=F32)


def _ffn_down(act, w_down, xf, *, tm=1024, tn=512):
    t = xf.shape[0]
    tm = min(tm, t)
    return pl.pallas_call(
        _ffn_down_body,
        grid=(t // tm, D_MODEL // tn),
        in_specs=[
            pl.BlockSpec((tm, D_FF_PAD), lambda i, j: (i, 0)),
            pl.BlockSpec((D_FF_PAD, tn), lambda i, j: (0, j)),
            pl.BlockSpec((tm, tn), lambda i, j: (i, j)),
        ],
        out_specs=pl.BlockSpec((tm, tn), lambda i, j: (i, j)),
        out_shape=jax.ShapeDtypeStruct((t, D_MODEL), F32),
        compiler_params=pltpu.CompilerParams(
            dimension_semantics=("parallel", "parallel"),
            vmem_limit_bytes=VMEM_LIMIT),
        name="ffn_down",
    )(act, w_down, xf)


def _ffn_body(x_ref, g_ref, wa_ref, wv_ref, cwa_ref, cwv_ref, cba_ref, cbv_ref, wd_ref,
              out_ref, h_scr, act_scr, acc_scr, raw_a, raw_v, carry_a, carry_v, *, batch, tt, ncol):
    i = pl.program_id(0)
    j = pl.program_id(1)
    slot = j % 2
    rows = tt * batch
    hist = (CONV_W - 1) * batch

    @pl.when(j == 0)
    def _():
        x_tb = pltpu.einshape("btn->tbn", x_ref[...]).reshape(rows, D_MODEL)
        h_scr[...] = _rms_rows(x_tb, g_ref[...]).astype(BF16)
        acc_scr[...] = jnp.zeros_like(acc_scr)

    nsub = wa_ref.shape[1] // MXU_DIM
    sub = lambda c: slice(c * MXU_DIM, (c + 1) * MXU_DIM)

    def up_dots():
        h = h_scr[...]
        for c in range(nsub):
            raw_a[:, sub(c)] = jnp.dot(h, wa_ref[:, sub(c)], preferred_element_type=F32)
            raw_v[:, sub(c)] = jnp.dot(h, wv_ref[:, sub(c)], preferred_element_type=F32)

    def down_dot():
        return jnp.dot(act_scr[1 - slot], wd_ref[...], preferred_element_type=F32)

    def gate():
        def conv(raw, c, cw_ref, cb_ref, carry):
            up = raw[:, sub(c)]
            prev = jnp.where(i == 0, 0.0, carry[j, :, sub(c)])
            carry[j, :, sub(c)] = up[rows - hist:]
            ext = jnp.concatenate([prev, up], axis=0)
            cw = cw_ref[:, sub(c)]
            out = cb_ref[:, sub(c)] + cw[CONV_W - 1:CONV_W] * up
            for tap in range(CONV_W - 1):
                out = out + cw[tap:tap + 1] * ext[tap * batch:tap * batch + rows]
            return out

        for c in range(nsub):
            a = conv(raw_a, c, cwa_ref, cba_ref, carry_a)
            val = conv(raw_v, c, cwv_ref, cbv_ref, carry_v)
            act_scr[slot, :, sub(c)] = (_gelu_tanh(a) * val).astype(BF16)

    @pl.when(j == 0)
    def _():
        up_dots()
        gate()

    @pl.when((j > 0) & (j < ncol))
    def _():
        up_dots()
        acc_scr[...] += down_dot()
        gate()

    @pl.when(j == ncol)
    def _():
        y = acc_scr[...] + down_dot()
        out_ref[...] = x_ref[...] + pltpu.einshape("tbn->btn", y.reshape(tt, batch, D_MODEL))


def _ffn(x3, g, wa, wv, cwa, cwv, cba, cbv, w_down, *, tm=512, tn=FF_TILE):
    batch, seq, _ = x3.shape
    tt = min(tm // batch, seq)
    rows = tt * batch
    ncol = D_FF_PAD // tn
    hist = (CONV_W - 1) * batch
    const = lambda i, j: (0, 0)
    col = lambda i, j: (0, jnp.minimum(j, ncol - 1))
    return pl.pallas_call(
        functools.partial(_ffn_body, batch=batch, tt=tt, ncol=ncol),
        grid=(seq // tt, ncol + 1),
        in_specs=[
            pl.BlockSpec((batch, tt, D_MODEL), lambda i, j: (0, i, 0)),
            pl.BlockSpec((1, D_MODEL), const),
            pl.BlockSpec((D_MODEL, tn), col),
            pl.BlockSpec((D_MODEL, tn), col),
            pl.BlockSpec((CONV_W, tn), col),
            pl.BlockSpec((CONV_W, tn), col),
            pl.BlockSpec((1, tn), col),
            pl.BlockSpec((1, tn), col),
            pl.BlockSpec((tn, D_MODEL), lambda i, j: (jnp.maximum(j - 1, 0), 0)),
        ],
        out_specs=pl.BlockSpec((batch, tt, D_MODEL), lambda i, j: (0, i, 0)),
        out_shape=jax.ShapeDtypeStruct((batch, seq, D_MODEL), F32),
        scratch_shapes=[pltpu.VMEM((rows, D_MODEL), BF16),
                        pltpu.VMEM((2, rows, tn), BF16),
                        pltpu.VMEM((rows, D_MODEL), F32),
                        pltpu.VMEM((rows, tn), F32), pltpu.VMEM((rows, tn), F32),
                        pltpu.VMEM((ncol, hist, tn), F32),
                        pltpu.VMEM((ncol, hist, tn), F32)],
        compiler_params=pltpu.CompilerParams(
            dimension_semantics=("arbitrary", "arbitrary"),
            vmem_limit_bytes=VMEM_LIMIT),
        name="ffn_fused",
    )(x3, g.reshape(1, D_MODEL), wa, wv, cwa, cwv, cba, cbv, w_down)


def _pad_ff_cols(a):
    return jnp.pad(a, ((0, 0), (0, D_FF_PAD - D_FF)))


def kernel(x, norm1_g, w_in, q_norm_g, k_norm_g, lambda_q1, lambda_k1, lambda_q2, lambda_k2, subln_g, ssm_a_re, ssm_a_im, ssm_log_dt, ssm_b_re, ssm_b_im, ssm_c_re, ssm_c_im, ssm_d, ssm_glu_w, ssm_glu_b, w_branch_attn, w_branch_ssm, w_out, norm2_g, ffn_w_up, ffn_conv_w, ffn_conv_b, ffn_w_down):
    batch, seq, _ = x.shape
    depth = w_in.shape[0]
    xf = x.reshape(batch * seq, D_MODEL)
    for l in range(depth):
        lam_init = 0.8 - 0.6 * math.exp(-0.3 * l)
        q, k, v, u, gates = _in_proj(xf, norm1_g[l], w_in[l].astype(BF16),
                                     q_norm_g[l], k_norm_g[l], batch, seq)
        o_att = _attention(q, k, v, lambda_q1[l], lambda_k1[l], lambda_q2[l], lambda_k2[l],
                           subln_g[l], lam_init, batch, seq)
        bblk, are, aim, cblk = _ssm_params(ssm_a_re[l], ssm_a_im[l], ssm_log_dt[l],
                                           ssm_b_re[l], ssm_b_im[l], ssm_c_re[l], ssm_c_im[l])
        y = _ssm(u.reshape(batch, seq, D_SSM), bblk, are, aim, cblk, ssm_d[l])
        xf = _mix(y.reshape(batch * seq, D_SSM), o_att, gates, xf,
                  ssm_glu_w[l].astype(BF16), ssm_glu_b[l],
                  w_branch_attn[l].astype(BF16), w_branch_ssm[l].astype(BF16),
                  w_out[l].astype(BF16), batch, seq)
        wu = ffn_w_up[l]
        cw = ffn_conv_w[l]
        cb = ffn_conv_b[l].reshape(1, 2 * D_FF)
        w_down = jnp.pad(ffn_w_down[l], ((0, D_FF_PAD - D_FF), (0, 0))).astype(BF16)
        act = _ffn_up(xf, norm2_g[l],
                      _pad_ff_cols(wu[:, :D_FF]).astype(BF16), _pad_ff_cols(wu[:, D_FF:]).astype(BF16),
                      _pad_ff_cols(cw[:, :D_FF]), _pad_ff_cols(cw[:, D_FF:]),
                      _pad_ff_cols(cb[:, :D_FF]), _pad_ff_cols(cb[:, D_FF:]), seq)
        xf = _ffn_down(act, w_down, xf)
    return xf.reshape(batch, seq, D_MODEL)
```

```python
import functools
import math

import jax
import jax.numpy as jnp
from jax import lax
from jax.experimental import pallas as pl
from jax.experimental.pallas import tpu as pltpu

F32 = jnp.float32
BF16 = jnp.bfloat16

D_MODEL = 2048
N_HEADS = 8
HEAD_DIM = 64
V_HEAD_DIM = 2 * HEAD_DIM
D_QK = N_HEADS * 2 * HEAD_DIM
D_V = N_HEADS * V_HEAD_DIM
D_SSM = D_MODEL // 2
SSM_GROUP = 16
N_GROUPS = D_SSM // SSM_GROUP
SSM_STATE = 64
D_FF = 5504
CONV_W = 3
CHUNK = 64
EPS = 1e-6
IN_COLS = 2 * D_QK + D_V + D_SSM + 2 * D_MODEL

MXU_DIM = 256
FF_TILE = 512
ROW_CHUNK = 256
D_FF_PAD = ((D_FF + FF_TILE - 1) // FF_TILE) * FF_TILE
GROUPS_PER_TILE = 8
SSM_NC = GROUPS_PER_TILE * SSM_STATE
NEG_BIG = -1e30
Q_SCALE = HEAD_DIM ** -0.5 * math.log2(math.e)
VMEM_LIMIT = 56 * 1024 * 1024


def _gelu_tanh(x):
    c = math.sqrt(2.0 / math.pi)
    return x * (0.5 * (1.0 + jnp.tanh(c * (x + 0.044715 * (x * x * x)))))


def _rms_rows(xf, g):
    ms = jnp.mean(xf * xf, axis=-1, keepdims=True)
    return xf * lax.rsqrt(ms + EPS) * g


def _in_proj_body(x_ref, g_ref, w_ref, qg_ref, kg_ref, ones_ref,
                  q_ref, k_ref, v_ref, u_ref, gate_ref, h_scr, *, tn):
    j = pl.program_id(1)
    nq = D_QK // tn

    @pl.when(j == 0)
    def _():
        h_scr[...] = _rms_rows(x_ref[...], g_ref[...]).astype(BF16)

    tm = h_scr.shape[0]
    rc = min(ROW_CHUNK, tm)

    def tile(out_ref, epilogue):
        nchunk = tm // rc
        rows = lambda q: slice(q * rc, (q + 1) * rc)
        dot_q = lambda q: jnp.dot(h_scr[rows(q), :], w_ref[...], preferred_element_type=F32)
        acc = dot_q(0)
        for q in range(1, nchunk):
            nxt = dot_q(q)
            out_ref[rows(q - 1), :] = epilogue(acc).astype(BF16)
            acc = nxt
        out_ref[rows(nchunk - 1), :] = epilogue(acc).astype(BF16)

    def head_norm(gain_ref, scale):
        def epilogue(acc):
            outs = []
            for c in range(tn // MXU_DIM):
                a = acc[:, c * MXU_DIM:(c + 1) * MXU_DIM]
                ss = jnp.dot((a * a).astype(BF16), ones_ref[...], preferred_element_type=F32)
                outs.append(a * lax.rsqrt(ss * (1.0 / HEAD_DIM) + EPS) * (gain_ref[...] * scale))
            return jnp.concatenate(outs, axis=1)
        return epilogue

    plain = lambda acc: acc

    @pl.when(j < nq)
    def _():
        tile(q_ref, head_norm(qg_ref, Q_SCALE))

    @pl.when((j >= nq) & (j < 2 * nq))
    def _():
        tile(k_ref, head_norm(kg_ref, 1.0))

    @pl.when((j >= 2 * nq) & (j < 3 * nq))
    def _():
        tile(v_ref, plain)

    @pl.when((j >= 3 * nq) & (j < 4 * nq))
    def _():
        tile(u_ref, plain)

    @pl.when(j >= 4 * nq)
    def _():
        tile(gate_ref, plain)


def _in_proj(xf, g, w_bf, qg, kg, batch, seq, *, tm=1024, tn=512):
    t = xf.shape[0]
    tm = min(tm, seq)
    nq = D_QK // tn
    ng = 2 * D_MODEL // tn
    qg_t = jnp.tile(qg.astype(F32), MXU_DIM // HEAD_DIM).reshape(1, MXU_DIM)
    kg_t = jnp.tile(kg.astype(F32), MXU_DIM // HEAD_DIM).reshape(1, MXU_DIM)
    seg = jnp.arange(MXU_DIM) // HEAD_DIM
    ones_bd = (seg[:, None] == seg[None, :]).astype(BF16)

    def cl(j, lo, n):
        return jnp.clip(j - lo, 0, n - 1)

    const = lambda i, j: (0, 0)
    return pl.pallas_call(
        functools.partial(_in_proj_body, tn=tn),
        grid=(t // tm, IN_COLS // tn),
        in_specs=[
            pl.BlockSpec((tm, D_MODEL), lambda i, j: (i, 0)),
            pl.BlockSpec((1, D_MODEL), const),
            pl.BlockSpec((D_MODEL, tn), lambda i, j: (0, j)),
            pl.BlockSpec((1, MXU_DIM), const),
            pl.BlockSpec((1, MXU_DIM), const),
            pl.BlockSpec((MXU_DIM, MXU_DIM), const),
        ],
        out_specs=[
            pl.BlockSpec((tm, tn), lambda i, j: (i, cl(j, 0, nq))),
            pl.BlockSpec((tm, tn), lambda i, j: (i, cl(j, nq, nq))),
            pl.BlockSpec((tm, tn), lambda i, j: (i, cl(j, 2 * nq, nq))),
            pl.BlockSpec((tm, tn), lambda i, j: (i, cl(j, 3 * nq, nq))),
            pl.BlockSpec((tm, tn), lambda i, j: (i, cl(j, 4 * nq, ng))),
        ],
        out_shape=[
            jax.ShapeDtypeStruct((t, D_QK), BF16),
            jax.ShapeDtypeStruct((t, D_QK), BF16),
            jax.ShapeDtypeStruct((t, D_V), BF16),
            jax.ShapeDtypeStruct((t, D_SSM), BF16),
            jax.ShapeDtypeStruct((t, 2 * D_MODEL), BF16),
        ],
        scratch_shapes=[pltpu.VMEM((tm, D_MODEL), BF16)],
        compiler_params=pltpu.CompilerParams(
            dimension_semantics=("arbitrary", "arbitrary"),
            vmem_limit_bytes=VMEM_LIMIT),
        name="in_proj",
    )(xf, g.reshape(1, D_MODEL), w_bf, qg_t, kg_t, ones_bd)


N_BIAS_ROWS = 3


def _attn_body(sl_ref, laminit_ref, q_ref, k_ref, v_ref, lq1_ref, lk1_ref, lq2_ref, lk2_ref,
               subg_ref, o_ref, kaug_scr, vt_scr, dmat_scr, *, seq, tq, hpb):
    tk = tq
    nblk = seq // tk
    hd2 = 2 * HEAD_DIM
    nch = 2 * hpb
    hg = pl.program_id(1)
    lam_init = laminit_ref[0]
    lam = (jnp.exp(jnp.sum(lq1_ref[...] * lk1_ref[...], axis=-1, keepdims=True))
           - jnp.exp(jnp.sum(lq2_ref[...] * lk2_ref[...], axis=-1, keepdims=True))
           + lam_init)
    subg = subg_ref[...] * (1.0 - lam_init)

    koff = lax.broadcasted_iota(jnp.int32, (seq, hd2), 0) % tk
    klane = lax.broadcasted_iota(jnp.int32, (seq, hd2), 1)
    kbias = jnp.where(klane < N_BIAS_ROWS, koff, 0).astype(F32).astype(BF16)
    c = lax.broadcasted_iota(jnp.int32, (tk, tq), 0)
    r = lax.broadcasted_iota(jnp.int32, (tk, tq), 1)
    visible = (c // CHUNK) <= (r // CHUNK)
    ahead = jnp.maximum(c - r, 0).astype(F32)
    brow_i = lax.broadcasted_iota(jnp.int32, (hd2, tq), 0)
    sl2 = []
    brows = []
    for hh in range(hpb):
        h = hg * hpb + hh
        s_hi, s_mid, s_lo = sl_ref[h, 0], sl_ref[h, 1], sl_ref[h, 2]
        sl2.append(s_hi + s_mid + s_lo)
        brows.append(jnp.where(brow_i == 0, s_hi, jnp.where(brow_i == 1, s_mid,
                     jnp.where(brow_i == 2, s_lo, 0.0))).astype(BF16))
        kaug_scr[hh, :, 0:hd2] = k_ref[:, hh * hd2:(hh + 1) * hd2]
        kaug_scr[hh, :, hd2:2 * hd2] = kbias
        for blk in range(nblk):
            vt_scr[hh, blk] = v_ref[blk * tk:(blk + 1) * tk,
                                    hh * hd2:(hh + 1) * hd2].astype(F32).T.astype(BF16)
        dmat_scr[hh] = jnp.where(visible, (-2.0 * sl2[hh]) * ahead, NEG_BIG)

    def q_block(iq, _):
        q0 = pl.multiple_of(iq * tq, tq)
        qas = []
        for hh in range(hpb):
            qt = q_ref[pl.ds(q0, tq), hh * hd2:(hh + 1) * hd2].astype(F32).T
            for comp in range(2):
                keep = (brow_i < HEAD_DIM) if comp == 0 else (brow_i >= HEAD_DIM)
                qas.append(jnp.concatenate([jnp.where(keep, qt, 0.0).astype(BF16), brows[hh]],
                                           axis=0))
        ss = [jnp.dot(kaug_scr[idx // 2, pl.ds(q0, tk), :], qas[idx],
                      preferred_element_type=F32) + dmat_scr[idx // 2] for idx in range(nch)]
        ps, stats = [], []
        for idx in range(nch):
            m = jnp.max(ss[idx], axis=0, keepdims=True)
            p = jnp.exp2(ss[idx] - m)
            stats.append((m, jnp.sum(p, axis=0, keepdims=True)))
            ps.append(p.astype(BF16))
        state = []
        for idx in range(nch):
            acc = jnp.dot(vt_scr[idx // 2, iq], ps[idx], preferred_element_type=F32)
            state.extend([stats[idx][0], stats[idx][1], acc])

        def kv_block(j, carry):
            k0 = pl.multiple_of(j * tk, tk)
            boff = lax.convert_element_type((j - iq) * tk, F32)
            ss = [jnp.dot(kaug_scr[idx // 2, pl.ds(k0, tk), :], qas[idx],
                          preferred_element_type=F32) for idx in range(nch)]
            ps, stats = [], []
            for idx in range(nch):
                m, l, _ = carry[3 * idx:3 * idx + 3]
                bc = sl2[idx // 2] * boff
                m_new = jnp.maximum(m, jnp.max(ss[idx], axis=0, keepdims=True) + bc)
                alpha = jnp.exp2(m - m_new)
                p = jnp.exp2(ss[idx] - (m_new - bc))
                stats.append((m_new, alpha * l + jnp.sum(p, axis=0, keepdims=True), alpha))
                ps.append(p.astype(BF16))
            out = []
            for idx in range(nch):
                m_new, l_new, alpha = stats[idx]
                acc = alpha * carry[3 * idx + 2] + jnp.dot(vt_scr[idx // 2, j], ps[idx],
                                                           preferred_element_type=F32)
                out.extend([m_new, l_new, acc])
            return tuple(out)

        state = lax.fori_loop(0, iq, kv_block, tuple(state))
        for hh in range(hpb):
            _, l0, a0, _, l1, a1 = state[6 * hh:6 * hh + 6]
            ot = a0 * (1.0 / l0) - a1 * (lam / l1)
            ot = ot * lax.rsqrt(jnp.mean(ot * ot, axis=0, keepdims=True) + EPS)
            o_ref[pl.ds(q0, tq), hh * hd2:(hh + 1) * hd2] = (ot.T * subg).astype(BF16)
        return 0

    lax.fori_loop(0, seq // tq, q_block, 0)


def _attention(q, k, v, lq1, lk1, lq2, lk2, subg, lam_init, batch, seq, *, tq=256, hpb=4):
    slopes = 2.0 ** (-8.0 * jnp.arange(1, N_HEADS + 1, dtype=F32) / N_HEADS)
    sl = slopes * math.log2(math.e)
    s_hi = sl.astype(BF16).astype(F32)
    s_mid = (sl - s_hi).astype(BF16).astype(F32)
    s_lo = (sl - s_hi - s_mid).astype(BF16).astype(F32)
    sl3 = jnp.stack([s_hi, s_mid, s_lo], axis=1)
    lam_arr = jnp.full((1,), lam_init, F32)
    smem = pl.BlockSpec(memory_space=pltpu.SMEM)
    hw = hpb * V_HEAD_DIM
    head_blk = pl.BlockSpec((seq, hw), lambda b, h: (b, h))
    vec = lambda n: pl.BlockSpec((1, n), lambda b, h: (0, 0))
    row = lambda a: a.astype(F32).reshape(1, -1)
    return pl.pallas_call(
        functools.partial(_attn_body, seq=seq, tq=tq, hpb=hpb),
        grid=(batch, N_HEADS // hpb),
        in_specs=[smem, smem, head_blk, head_blk, head_blk,
                  vec(HEAD_DIM), vec(HEAD_DIM), vec(HEAD_DIM), vec(HEAD_DIM), vec(V_HEAD_DIM)],
        out_specs=head_blk,
        out_shape=jax.ShapeDtypeStruct((batch * seq, D_V), BF16),
        scratch_shapes=[pltpu.VMEM((hpb, seq, 2 * V_HEAD_DIM), BF16),
                        pltpu.VMEM((hpb, seq // tq, V_HEAD_DIM, tq), BF16),
                        pltpu.VMEM((hpb, tq, tq), F32)],
        compiler_params=pltpu.CompilerParams(
            dimension_semantics=("parallel", "parallel"),
            vmem_limit_bytes=VMEM_LIMIT),
        name="diff_attention",
    )(sl3, lam_arr, q, k, v, row(lq1), row(lk1), row(lq2), row(lk2), row(subg))


def _ssm_body(u_ref, bblk_ref, are_ref, aim_ref, cblk_ref, d_ref, y_ref,
              x_scr, h_scr, st_scr, *, batch, tt):
    nc = st_scr.shape[-1] // 2
    gcols = u_ref.shape[-1]

    @pl.when(pl.program_id(1) == 0)
    def _():
        st_scr[...] = jnp.zeros_like(st_scr)

    u_f = pltpu.einshape("btn->tbn", u_ref[...].astype(F32)).reshape(tt * batch, gcols)
    x_scr[...] = jnp.dot(u_f.astype(BF16), bblk_ref[...], preferred_element_type=F32)
    ar = jnp.broadcast_to(are_ref[...], (batch, nc))
    ai = jnp.broadcast_to(aim_ref[...], (batch, nc))

    def step(t, carry):
        hr, hi = carry
        r0 = pl.multiple_of(t * batch, batch)
        xr = x_scr[pl.ds(r0, batch), :nc]
        xi = x_scr[pl.ds(r0, batch), nc:]
        nr = ar * hr - ai * hi + xr
        ni = ar * hi + ai * hr + xi
        h_scr[pl.ds(r0, batch), :nc] = nr.astype(BF16)
        h_scr[pl.ds(r0, batch), nc:] = ni.astype(BF16)
        return nr, ni

    hr, hi = lax.fori_loop(0, tt, step, (st_scr[:, :nc], st_scr[:, nc:]), unroll=4)
    st_scr[:, :nc] = hr
    st_scr[:, nc:] = hi
    y = jnp.dot(h_scr[...], cblk_ref[...], preferred_element_type=F32)
    y = _gelu_tanh(y + d_ref[...] * u_f)
    y_ref[...] = pltpu.einshape("tbn->btn", y.reshape(tt, batch, gcols)).astype(BF16)


def _ssm(u3, bblk, are, aim, cblk, d_skip, *, tt=128):
    batch, seq, _ = u3.shape
    gcols = GROUPS_PER_TILE * SSM_GROUP
    ngt = N_GROUPS // GROUPS_PER_TILE
    tt = min(tt, seq)
    rows = tt * batch
    return pl.pallas_call(
        functools.partial(_ssm_body, batch=batch, tt=tt),
        grid=(ngt, seq // tt),
        in_specs=[
            pl.BlockSpec((batch, tt, gcols), lambda g, t: (0, t, g)),
            pl.BlockSpec((None, gcols, 2 * SSM_NC), lambda g, t: (g, 0, 0)),
            pl.BlockSpec((None, 1, SSM_NC), lambda g, t: (g, 0, 0)),
            pl.BlockSpec((None, 1, SSM_NC), lambda g, t: (g, 0, 0)),
            pl.BlockSpec((None, 2 * SSM_NC, gcols), lambda g, t: (g, 0, 0)),
            pl.BlockSpec((1, gcols), lambda g, t: (0, g)),
        ],
        out_specs=pl.BlockSpec((batch, tt, gcols), lambda g, t: (0, t, g)),
        out_shape=jax.ShapeDtypeStruct((batch, seq, D_SSM), BF16),
        scratch_shapes=[pltpu.VMEM((rows, 2 * SSM_NC), F32),
                        pltpu.VMEM((rows, 2 * SSM_NC), BF16),
                        pltpu.VMEM((batch, 2 * SSM_NC), F32)],
        compiler_params=pltpu.CompilerParams(
            dimension_semantics=("parallel", "arbitrary"),
            vmem_limit_bytes=VMEM_LIMIT),
        name="s5_scan",
    )(u3, bblk, are, aim, cblk, d_skip.astype(F32).reshape(1, D_SSM))


def _ssm_params(a_re, a_im, log_dt, b_re, b_im, c_re, c_im):
    gpt = GROUPS_PER_TILE
    ngt = N_GROUPS // gpt
    dt = jnp.exp(log_dt)[:, None]
    mag = jnp.exp(dt * a_re)
    ab_re = mag * jnp.cos(dt * a_im)
    ab_im = mag * jnp.sin(dt * a_im)
    den = a_re * a_re + a_im * a_im
    zr = ab_re - 1.0
    zi = ab_im
    f_re = (zr * a_re + zi * a_im) / den
    f_im = (zi * a_re - zr * a_im) / den
    bb_re = f_re[..., None] * b_re - f_im[..., None] * b_im
    bb_im = f_re[..., None] * b_im + f_im[..., None] * b_re
    eye = jnp.eye(gpt, dtype=F32)

    def bdiag_in(bb):
        t = bb.reshape(ngt, gpt, SSM_STATE, SSM_GROUP)
        return jnp.einsum('tgpi,gh->tgihp', t, eye).reshape(ngt, gpt * SSM_GROUP, SSM_NC)

    def bdiag_out(cc):
        t = cc.reshape(ngt, gpt, SSM_GROUP, SSM_STATE)
        return jnp.einsum('tgop,gh->tgpho', t, eye).reshape(ngt, SSM_NC, gpt * SSM_GROUP)

    bblk = jnp.concatenate([bdiag_in(bb_re), bdiag_in(bb_im)], axis=-1).astype(BF16)
    cblk = jnp.concatenate([bdiag_out(c_re), -bdiag_out(c_im)], axis=1).astype(BF16)
    return bblk, ab_re.reshape(ngt, 1, SSM_NC), ab_im.reshape(ngt, 1, SSM_NC), cblk


def _mix_body(y_ref, oa_ref, gate_ref, x_ref, gw_ref, gb_ref, wba_ref, wbs_ref, wo_ref, out_ref):
    y = y_ref[...]
    z = jnp.dot(y, gw_ref[...], preferred_element_type=F32) + gb_ref[...]
    y2 = (y.astype(F32) * jax.nn.sigmoid(z)).astype(BF16)
    o_ssm = jnp.dot(y2, wbs_ref[...], preferred_element_type=F32)
    o_att = jnp.dot(oa_ref[...], wba_ref[...], preferred_element_type=F32)
    mixed = (jax.nn.sigmoid(gate_ref[:, :D_MODEL].astype(F32)) * o_att
             + jax.nn.sigmoid(gate_ref[:, D_MODEL:].astype(F32)) * o_ssm)
    out_ref[...] = x_ref[...] + jnp.dot(mixed.astype(BF16), wo_ref[...],
                                        preferred_element_type=F32)


def _mix(y, o_att, gates, xf, glu_w, glu_b, w_ba, w_bs, w_out, batch, seq, *, tm=256):
    t = xf.shape[0]
    tm = min(tm, seq)
    full = lambda shape: pl.BlockSpec(shape, lambda i: (0, 0), pipeline_mode=pl.Buffered(1))
    return pl.pallas_call(
        _mix_body,
        grid=(t // tm,),
        in_specs=[
            pl.BlockSpec((tm, D_SSM), lambda i: (i, 0)),
            pl.BlockSpec((tm, D_V), lambda i: (i, 0)),
            pl.BlockSpec((tm, 2 * D_MODEL), lambda i: (i, 0)),
            pl.BlockSpec((tm, D_MODEL), lambda i: (i, 0)),
            full((D_SSM, D_SSM)), full((1, D_SSM)),
            full((D_V, D_MODEL)), full((D_SSM, D_MODEL)), full((D_MODEL, D_MODEL)),
        ],
        out_specs=pl.BlockSpec((tm, D_MODEL), lambda i: (i, 0)),
        out_shape=jax.ShapeDtypeStruct((t, D_MODEL), F32),
        compiler_params=pltpu.CompilerParams(
            dimension_semantics=("parallel",),
            vmem_limit_bytes=VMEM_LIMIT),
        name="gated_merge",
    )(y, o_att, gates, xf, glu_w, glu_b.astype(F32).reshape(1, D_SSM), w_ba, w_bs, w_out)


def _shift_rows(up, prev, k):
    body = pltpu.roll(up, k, axis=0)
    top = pltpu.roll(jnp.concatenate([prev, up[:8]], axis=0), k, axis=0)[8:]
    return jnp.concatenate([top, body[8:]], axis=0)


def _ffn_up_body(x_ref, g_ref, wa_ref, wv_ref, cwa_ref, cwv_ref, cba_ref, cbv_ref,
                 act_ref, h_scr, carry_a, carry_v, *, nsb):
    i = pl.program_id(0)
    j = pl.program_id(1)

    @pl.when(j == 0)
    def _():
        h_scr[...] = _rms_rows(x_ref[...], g_ref[...]).astype(BF16)

    seq_start = i % nsb == 0
    tm = h_scr.shape[0]
    rc = min(ROW_CHUNK, tm)
    prev_a = jnp.where(seq_start, 0.0, carry_a[j])
    prev_v = jnp.where(seq_start, 0.0, carry_v[j])

    def conv(up, prev, cw_ref, cb_ref):
        cw = cw_ref[...]
        return (cb_ref[...] + cw[0:1] * _shift_rows(up, prev, 2)
                + cw[1:2] * _shift_rows(up, prev, 1) + cw[2:3] * up)

    for q in range(tm // rc):
        h = h_scr[q * rc:(q + 1) * rc, :]
        up_a = jnp.dot(h, wa_ref[...], preferred_element_type=F32)
        up_v = jnp.dot(h, wv_ref[...], preferred_element_type=F32)
        a = conv(up_a, prev_a, cwa_ref, cba_ref)
        val = conv(up_v, prev_v, cwv_ref, cbv_ref)
        act_ref[q * rc:(q + 1) * rc, :] = (_gelu_tanh(a) * val).astype(BF16)
        prev_a, prev_v = up_a[rc - 8:], up_v[rc - 8:]
    carry_a[j] = prev_a
    carry_v[j] = prev_v


def _ffn_up(xf, g, wa, wv, cwa, cwv, cba, cbv, seq, *, tm=512, tn=FF_TILE):
    t = xf.shape[0]
    tm = min(tm, seq)
    ncol = D_FF_PAD // tn
    const = lambda i, j: (0, 0)
    col = lambda i, j: (0, j)
    return pl.pallas_call(
        functools.partial(_ffn_up_body, nsb=seq // tm),
        grid=(t // tm, ncol),
        in_specs=[
            pl.BlockSpec((tm, D_MODEL), lambda i, j: (i, 0)),
            pl.BlockSpec((1, D_MODEL), const),
            pl.BlockSpec((D_MODEL, tn), col),
            pl.BlockSpec((D_MODEL, tn), col),
            pl.BlockSpec((CONV_W, tn), col),
            pl.BlockSpec((CONV_W, tn), col),
            pl.BlockSpec((1, tn), col),
            pl.BlockSpec((1, tn), col),
        ],
        out_specs=pl.BlockSpec((tm, tn), lambda i, j: (i, j)),
        out_shape=jax.ShapeDtypeStruct((t, D_FF_PAD), BF16),
        scratch_shapes=[pltpu.VMEM((tm, D_MODEL), BF16),
                        pltpu.VMEM((ncol, 8, tn), F32),
                        pltpu.VMEM((ncol, 8, tn), F32)],
        compiler_params=pltpu.CompilerParams(
            dimension_semantics=("arbitrary", "arbitrary"),
            vmem_limit_bytes=VMEM_LIMIT),
        name="ffn_up_conv_gate",
    )(xf, g.reshape(1, D_MODEL), wa, wv, cwa, cwv, cba, cbv)


def _ffn_down_body(act_ref, w_ref, x_ref, out_ref):
    out_ref[...] = x_ref[...] + jnp.dot(act_ref[...], w_ref[...], preferred_element_type=F32)


def _ffn_down(act, w_down, xf, *, tm=1024, tn=512):
    t = xf.shape[0]
    tm = min(tm, t)
    return pl.pallas_call(
        _ffn_down_body,
        grid=(t // tm, D_MODEL // tn),
        in_specs=[
            pl.BlockSpec((tm, D_FF_PAD), lambda i, j: (i, 0)),
            pl.BlockSpec((D_FF_PAD, tn), lambda i, j: (0, j)),
            pl.BlockSpec((tm, tn), lambda i, j: (i, j)),
        ],
        out_specs=pl.BlockSpec((tm, tn), lambda i, j: (i, j)),
        out_shape=jax.ShapeDtypeStruct((t, D_MODEL), F32),
        compiler_params=pltpu.CompilerParams(
            dimension_semantics=("parallel", "parallel"),
            vmem_limit_bytes=VMEM_LIMIT),
        name="ffn_down",
    )(act, w_down, xf)


def _ffn_body(x_ref, g_ref, wa_ref, wv_ref, cwa_ref, cwv_ref, cba_ref, cbv_ref, wd_ref,
              out_ref, h_scr, act_scr, acc_scr, raw_a, raw_v, carry_a, carry_v, *, batch, tt, ncol):
    i = pl.program_id(0)
    j = pl.program_id(1)
    slot = j % 2
    rows = tt * batch
    hist = (CONV_W - 1) * batch

    @pl.when(j == 0)
    def _():
        x_tb = pltpu.einshape("btn->tbn", x_ref[...]).reshape(rows, D_MODEL)
        h_scr[...] = _rms_rows(x_tb, g_ref[...]).astype(BF16)
        acc_scr[...] = jnp.zeros_like(acc_scr)

    nsub = wa_ref.shape[1] // MXU_DIM
    sub = lambda c: slice(c * MXU_DIM, (c + 1) * MXU_DIM)

    def up_dots():
        h = h_scr[...]
        for c in range(nsub):
            raw_a[:, sub(c)] = jnp.dot(h, wa_ref[:, sub(c)], preferred_element_type=F32)
            raw_v[:, sub(c)] = jnp.dot(h, wv_ref[:, sub(c)], preferred_element_type=F32)

    def down_dot():
        return jnp.dot(act_scr[1 - slot], wd_ref[...], preferred_element_type=F32)

    def gate():
        def conv(raw, c, cw_ref, cb_ref, carry):
            up = raw[:, sub(c)]
            prev = jnp.where(i == 0, 0.0, carry[j, :, sub(c)])
            carry[j, :, sub(c)] = up[rows - hist:]
            ext = jnp.concatenate([prev, up], axis=0)
            cw = cw_ref[:, sub(c)]
            out = cb_ref[:, sub(c)] + cw[CONV_W - 1:CONV_W] * up
            for tap in range(CONV_W - 1):
                out = out + cw[tap:tap + 1] * ext[tap * batch:tap * batch + rows]
            return out

        for c in range(nsub):
            a = conv(raw_a, c, cwa_ref, cba_ref, carry_a)
            val = conv(raw_v, c, cwv_ref, cbv_ref, carry_v)
            act_scr[slot, :, sub(c)] = (_gelu_tanh(a) * val).astype(BF16)

    @pl.when(j == 0)
    def _():
        up_dots()
        gate()

    @pl.when((j > 0) & (j < ncol))
    def _():
        up_dots()
        acc_scr[...] += down_dot()
        gate()

    @pl.when(j == ncol)
    def _():
        y = acc_scr[...] + down_dot()
        out_ref[...] = x_ref[...] + pltpu.einshape("tbn->btn", y.reshape(tt, batch, D_MODEL))


def _ffn(x3, g, wa, wv, cwa, cwv, cba, cbv, w_down, *, tm=512, tn=FF_TILE):
    batch, seq, _ = x3.shape
    tt = min(tm // batch, seq)
    rows = tt * batch
    ncol = D_FF_PAD // tn
    hist = (CONV_W - 1) * batch
    const = lambda i, j: (0, 0)
    col = lambda i, j: (0, jnp.minimum(j, ncol - 1))
    return pl.pallas_call(
        functools.partial(_ffn_body, batch=batch, tt=tt, ncol=ncol),
        grid=(seq // tt, ncol + 1),
        in_specs=[
            pl.BlockSpec((batch, tt, D_MODEL), lambda i, j: (0, i, 0)),
            pl.BlockSpec((1, D_MODEL), const),
            pl.BlockSpec((D_MODEL, tn), col),
            pl.BlockSpec((D_MODEL, tn), col),
            pl.BlockSpec((CONV_W, tn), col),
            pl.BlockSpec((CONV_W, tn), col),
            pl.BlockSpec((1, tn), col),
            pl.BlockSpec((1, tn), col),
            pl.BlockSpec((tn, D_MODEL), lambda i, j: (jnp.maximum(j - 1, 0), 0)),
        ],
        out_specs=pl.BlockSpec((batch, tt, D_MODEL), lambda i, j: (0, i, 0)),
        out_shape=jax.ShapeDtypeStruct((batch, seq, D_MODEL), F32),
        scratch_shapes=[pltpu.VMEM((rows, D_MODEL), BF16),
                        pltpu.VMEM((2, rows, tn), BF16),
                        pltpu.VMEM((rows, D_MODEL), F32),
                        pltpu.VMEM((rows, tn), F32), pltpu.VMEM((rows, tn), F32),
                        pltpu.VMEM((ncol, hist, tn), F32),
                        pltpu.VMEM((ncol, hist, tn), F32)],
        compiler_params=pltpu.CompilerParams(
            dimension_semantics=("arbitrary", "arbitrary"),
            vmem_limit_bytes=VMEM_LIMIT),
        name="ffn_fused",
    )(x3, g.reshape(1, D_MODEL), wa, wv, cwa, cwv, cba, cbv, w_down)


def _pad_ff_cols(a):
    return jnp.pad(a, ((0, 0), (0, D_FF_PAD - D_FF)))


def kernel(x, norm1_g, w_in, q_norm_g, k_norm_g, lambda_q1, lambda_k1, lambda_q2, lambda_k2, subln_g, ssm_a_re, ssm_a_im, ssm_log_dt, ssm_b_re, ssm_b_im, ssm_c_re, ssm_c_im, ssm_d, ssm_glu_w, ssm_glu_b, w_branch_attn, w_branch_ssm, w_out, norm2_g, ffn_w_up, ffn_conv_w, ffn_conv_b, ffn_w_down):
    batch, seq, _ = x.shape
    depth = w_in.shape[0]
    xf = x.reshape(batch * seq, D_MODEL)
    for l in range(depth):
        lam_init = 0.8 - 0.6 * math.exp(-0.3 * l)
        q, k, v, u, gates = _in_proj(xf, norm1_g[l], w_in[l].astype(BF16),
                                     q_norm_g[l], k_norm_g[l], batch, seq)
        o_att = _attention(q, k, v, lambda_q1[l], lambda_k1[l], lambda_q2[l], lambda_k2[l],
                           subln_g[l], lam_init, batch, seq)
        bblk, are, aim, cblk = _ssm_params(ssm_a_re[l], ssm_a_im[l], ssm_log_dt[l],
                                           ssm_b_re[l], ssm_b_im[l], ssm_c_re[l], ssm_c_im[l])
        y = _ssm(u.reshape(batch, seq, D_SSM), bblk, are, aim, cblk, ssm_d[l])
        xf = _mix(y.reshape(batch * seq, D_SSM), o_att, gates, xf,
                  ssm_glu_w[l].astype(BF16), ssm_glu_b[l],
                  w_branch_attn[l].astype(BF16), w_branch_ssm[l].astype(BF16),
                  w_out[l].astype(BF16), batch, seq)
        wu = ffn_w_up[l]
        cw = ffn_conv_w[l]
        cb = ffn_conv_b[l].reshape(1, 2 * D_FF)
        w_down = jnp.pad(ffn_w_down[l], ((0, D_FF_PAD - D_FF), (0, 0))).astype(BF16)
        act = _ffn_up(xf, norm2_g[l],
                      _pad_ff_cols(wu[:, :D_FF]).astype(BF16), _pad_ff_cols(wu[:, D_FF:]).astype(BF16),
                      _pad_ff_cols(cw[:, :D_FF]), _pad_ff_cols(cw[:, D_FF:]),
                      _pad_ff_cols(cb[:, :D_FF]), _pad_ff_cols(cb[:, D_FF:]), seq)
        xf = _ffn_down(act, w_down, xf)
    return xf.reshape(batch, seq, D_MODEL)
```

```python
import functools
import math

import jax
import jax.numpy as jnp
from jax import lax
from jax.experimental import pallas as pl
from jax.experimental.pallas import tpu as pltpu

F32 = jnp.float32
BF16 = jnp.bfloat16

D_MODEL = 2048
N_HEADS = 8
HEAD_DIM = 64
V_HEAD_DIM = 2 * HEAD_DIM
D_QK = N_HEADS * 2 * HEAD_DIM
D_V = N_HEADS * V_HEAD_DIM
D_SSM = D_MODEL // 2
SSM_GROUP = 16
N_GROUPS = D_SSM // SSM_GROUP
SSM_STATE = 64
D_FF = 5504
CONV_W = 3
CHUNK = 64
EPS = 1e-6
IN_COLS = 2 * D_QK + D_V + D_SSM + 2 * D_MODEL

MXU_DIM = 256
FF_TILE = 512
ROW_CHUNK = 256
D_FF_PAD = ((D_FF + FF_TILE - 1) // FF_TILE) * FF_TILE
GROUPS_PER_TILE = 8
SSM_NC = GROUPS_PER_TILE * SSM_STATE
NEG_BIG = -1e30
Q_SCALE = HEAD_DIM ** -0.5 * math.log2(math.e)
VMEM_LIMIT = 56 * 1024 * 1024


def _gelu_tanh(x):
    c = math.sqrt(2.0 / math.pi)
    return x * (0.5 * (1.0 + jnp.tanh(c * (x + 0.044715 * (x * x * x)))))


def _rms_rows(xf, g):
    ms = jnp.mean(xf * xf, axis=-1, keepdims=True)
    return xf * lax.rsqrt(ms + EPS) * g


def _in_proj_body(x_ref, g_ref, w_ref, qg_ref, kg_ref, ones_ref,
                  q_ref, k_ref, v_ref, u_ref, gate_ref, h_scr, *, tn):
    j = pl.program_id(1)
    nq = D_QK // tn

    @pl.when(j == 0)
    def _():
        h_scr[...] = _rms_rows(x_ref[...], g_ref[...]).astype(BF16)

    tm = h_scr.shape[0]
    rc = min(ROW_CHUNK, tm)

    def tile(out_ref, epilogue):
        nchunk = tm // rc
        rows = lambda q: slice(q * rc, (q + 1) * rc)
        dot_q = lambda q: jnp.dot(h_scr[rows(q), :], w_ref[...], preferred_element_type=F32)
        acc = dot_q(0)
        for q in range(1, nchunk):
            nxt = dot_q(q)
            out_ref[rows(q - 1), :] = epilogue(acc).astype(BF16)
            acc = nxt
        out_ref[rows(nchunk - 1), :] = epilogue(acc).astype(BF16)

    def head_norm(gain_ref, scale):
        def epilogue(acc):
            outs = []
            for c in range(tn // MXU_DIM):
                a = acc[:, c * MXU_DIM:(c + 1) * MXU_DIM]
                ss = jnp.dot((a * a).astype(BF16), ones_ref[...], preferred_element_type=F32)
                outs.append(a * lax.rsqrt(ss * (1.0 / HEAD_DIM) + EPS) * (gain_ref[...] * scale))
            return jnp.concatenate(outs, axis=1)
        return epilogue

    plain = lambda acc: acc

    @pl.when(j < nq)
    def _():
        tile(q_ref, head_norm(qg_ref, Q_SCALE))

    @pl.when((j >= nq) & (j < 2 * nq))
    def _():
        tile(k_ref, head_norm(kg_ref, 1.0))

    @pl.when((j >= 2 * nq) & (j < 3 * nq))
    def _():
        tile(v_ref, plain)

    @pl.when((j >= 3 * nq) & (j < 4 * nq))
    def _():
        tile(u_ref, plain)

    @pl.when(j >= 4 * nq)
    def _():
        tile(gate_ref, plain)


def _in_proj(xf, g, w_bf, qg, kg, batch, seq, *, tm=1024, tn=512):
    t = xf.shape[0]
    tm = min(tm, seq)
    nq = D_QK // tn
    ng = 2 * D_MODEL // tn
    qg_t = jnp.tile(qg.astype(F32), MXU_DIM // HEAD_DIM).reshape(1, MXU_DIM)
    kg_t = jnp.tile(kg.astype(F32), MXU_DIM // HEAD_DIM).reshape(1, MXU_DIM)
    seg = jnp.arange(MXU_DIM) // HEAD_DIM
    ones_bd = (seg[:, None] == seg[None, :]).astype(BF16)

    def cl(j, lo, n):
        return jnp.clip(j - lo, 0, n - 1)

    const = lambda i, j: (0, 0)
    return pl.pallas_call(
        functools.partial(_in_proj_body, tn=tn),
        grid=(t // tm, IN_COLS // tn),
        in_specs=[
            pl.BlockSpec((tm, D_MODEL), lambda i, j: (i, 0)),
            pl.BlockSpec((1, D_MODEL), const),
            pl.BlockSpec((D_MODEL, tn), lambda i, j: (0, j)),
            pl.BlockSpec((1, MXU_DIM), const),
            pl.BlockSpec((1, MXU_DIM), const),
            pl.BlockSpec((MXU_DIM, MXU_DIM), const),
        ],
        out_specs=[
            pl.BlockSpec((tm, tn), lambda i, j: (i, cl(j, 0, nq))),
            pl.BlockSpec((tm, tn), lambda i, j: (i, cl(j, nq, nq))),
            pl.BlockSpec((tm, tn), lambda i, j: (i, cl(j, 2 * nq, nq))),
            pl.BlockSpec((tm, tn), lambda i, j: (i, cl(j, 3 * nq, nq))),
            pl.BlockSpec((tm, tn), lambda i, j: (i, cl(j, 4 * nq, ng))),
        ],
        out_shape=[
            jax.ShapeDtypeStruct((t, D_QK), BF16),
            jax.ShapeDtypeStruct((t, D_QK), BF16),
            jax.ShapeDtypeStruct((t, D_V), BF16),
            jax.ShapeDtypeStruct((t, D_SSM), BF16),
            jax.ShapeDtypeStruct((t, 2 * D_MODEL), BF16),
        ],
        scratch_shapes=[pltpu.VMEM((tm, D_MODEL), BF16)],
        compiler_params=pltpu.CompilerParams(
            dimension_semantics=("arbitrary", "arbitrary"),
            vmem_limit_bytes=VMEM_LIMIT),
        name="in_proj",
    )(xf, g.reshape(1, D_MODEL), w_bf, qg_t, kg_t, ones_bd)


N_BIAS_ROWS = 3


def _attn_body(sl_ref, laminit_ref, q_ref, k_ref, v_ref, lq1_ref, lk1_ref, lq2_ref, lk2_ref,
               subg_ref, o_ref, kaug_scr, vt_scr, dmat_scr, *, seq, tq, hpb):
    tk = tq
    nblk = seq // tk
    hd2 = 2 * HEAD_DIM
    nch = 2 * hpb
    hg = pl.program_id(1)
    lam_init = laminit_ref[0]
    lam = (jnp.exp(jnp.sum(lq1_ref[...] * lk1_ref[...], axis=-1, keepdims=True))
           - jnp.exp(jnp.sum(lq2_ref[...] * lk2_ref[...], axis=-1, keepdims=True))
           + lam_init)
    subg = subg_ref[...] * (1.0 - lam_init)

    koff = lax.broadcasted_iota(jnp.int32, (seq, hd2), 0) % tk
    klane = lax.broadcasted_iota(jnp.int32, (seq, hd2), 1)
    kbias = jnp.where(klane < N_BIAS_ROWS, koff, 0).astype(F32).astype(BF16)
    c = lax.broadcasted_iota(jnp.int32, (tk, tq), 0)
    r = lax.broadcasted_iota(jnp.int32, (tk, tq), 1)
    visible = (c // CHUNK) <= (r // CHUNK)
    ahead = jnp.maximum(c - r, 0).astype(F32)
    brow_i = lax.broadcasted_iota(jnp.int32, (hd2, tq), 0)
    sl2 = []
    brows = []
    for hh in range(hpb):
        h = hg * hpb + hh
        s_hi, s_mid, s_lo = sl_ref[h, 0], sl_ref[h, 1], sl_ref[h, 2]
        sl2.append(s_hi + s_mid + s_lo)
        brows.append(jnp.where(brow_i == 0, s_hi, jnp.where(brow_i == 1, s_mid,
                     jnp.where(brow_i == 2, s_lo, 0.0))).astype(BF16))
        kaug_scr[hh, :, 0:hd2] = k_ref[:, hh * hd2:(hh + 1) * hd2]
        kaug_scr[hh, :, hd2:2 * hd2] = kbias
        for blk in range(nblk):
            vt_scr[hh, blk] = v_ref[blk * tk:(blk + 1) * tk,
                                    hh * hd2:(hh + 1) * hd2].astype(F32).T.astype(BF16)
        dmat_scr[hh] = jnp.where(visible, (-2.0 * sl2[hh]) * ahead, NEG_BIG)

    def q_block(iq, _):
        q0 = pl.multiple_of(iq * tq, tq)
        qas = []
        for hh in range(hpb):
            qt = q_ref[pl.ds(q0, tq), hh * hd2:(hh + 1) * hd2].astype(F32).T
            for comp in range(2):
                keep = (brow_i < HEAD_DIM) if comp == 0 else (brow_i >= HEAD_DIM)
                qas.append(jnp.concatenate([jnp.where(keep, qt, 0.0).astype(BF16), brows[hh]],
                                           axis=0))
        ss = [jnp.dot(kaug_scr[idx // 2, pl.ds(q0, tk), :], qas[idx],
                      preferred_element_type=F32) + dmat_scr[idx // 2] for idx in range(nch)]
        ps, stats = [], []
        for idx in range(nch):
            m = jnp.max(ss[idx], axis=0, keepdims=True)
            p = jnp.exp2(ss[idx] - m)
            stats.append((m, jnp.sum(p, axis=0, keepdims=True)))
            ps.append(p.astype(BF16))
        state = []
        for idx in range(nch):
            acc = jnp.dot(vt_scr[idx // 2, iq], ps[idx], preferred_element_type=F32)
            state.extend([stats[idx][0], stats[idx][1], acc])

        def kv_block(j, carry):
            k0 = pl.multiple_of(j * tk, tk)
            boff = lax.convert_element_type((j - iq) * tk, F32)
            ss = [jnp.dot(kaug_scr[idx // 2, pl.ds(k0, tk), :], qas[idx],
                          preferred_element_type=F32) for idx in range(nch)]
            ps, stats = [], []
            for idx in range(nch):
                m, l, _ = carry[3 * idx:3 * idx + 3]
                bc = sl2[idx // 2] * boff
                m_new = jnp.maximum(m, jnp.max(ss[idx], axis=0, keepdims=True) + bc)
                alpha = jnp.exp2(m - m_new)
                p = jnp.exp2(ss[idx] - (m_new - bc))
                stats.append((m_new, alpha * l + jnp.sum(p, axis=0, keepdims=True), alpha))
                ps.append(p.astype(BF16))
            out = []
            for idx in range(nch):
                m_new, l_new, alpha = stats[idx]
                acc = alpha * carry[3 * idx + 2] + jnp.dot(vt_scr[idx // 2, j], ps[idx],
                                                           preferred_element_type=F32)
                out.extend([m_new, l_new, acc])
            return tuple(out)

        state = lax.fori_loop(0, iq, kv_block, tuple(state))
        for hh in range(hpb):
            _, l0, a0, _, l1, a1 = state[6 * hh:6 * hh + 6]
            ot = a0 * (1.0 / l0) - a1 * (lam / l1)
            ot = ot * lax.rsqrt(jnp.mean(ot * ot, axis=0, keepdims=True) + EPS)
            o_ref[pl.ds(q0, tq), hh * hd2:(hh + 1) * hd2] = (ot.T * subg).astype(BF16)
        return 0

    lax.fori_loop(0, seq // tq, q_block, 0)


def _attention(q, k, v, lq1, lk1, lq2, lk2, subg, lam_init, batch, seq, *, tq=256, hpb=4):
    slopes = 2.0 ** (-8.0 * jnp.arange(1, N_HEADS + 1, dtype=F32) / N_HEADS)
    sl = slopes * math.log2(math.e)
    s_hi = sl.astype(BF16).astype(F32)
    s_mid = (sl - s_hi).astype(BF16).astype(F32)
    s_lo = (sl - s_hi - s_mid).astype(BF16).astype(F32)
    sl3 = jnp.stack([s_hi, s_mid, s_lo], axis=1)
    lam_arr = jnp.full((1,), lam_init, F32)
    smem = pl.BlockSpec(memory_space=pltpu.SMEM)
    hw = hpb * V_HEAD_DIM
    head_blk = pl.BlockSpec((seq, hw), lambda b, h: (b, h))
    vec = lambda n: pl.BlockSpec((1, n), lambda b, h: (0, 0))
    row = lambda a: a.astype(F32).reshape(1, -1)
    return pl.pallas_call(
        functools.partial(_attn_body, seq=seq, tq=tq, hpb=hpb),
        grid=(batch, N_HEADS // hpb),
        in_specs=[smem, smem, head_blk, head_blk, head_blk,
                  vec(HEAD_DIM), vec(HEAD_DIM), vec(HEAD_DIM), vec(HEAD_DIM), vec(V_HEAD_DIM)],
        out_specs=head_blk,
        out_shape=jax.ShapeDtypeStruct((batch * seq, D_V), BF16),
        scratch_shapes=[pltpu.VMEM((hpb, seq, 2 * V_HEAD_DIM), BF16),
                        pltpu.VMEM((hpb, seq // tq, V_HEAD_DIM, tq), BF16),
                        pltpu.VMEM((hpb, tq, tq), F32)],
        compiler_params=pltpu.CompilerParams(
            dimension_semantics=("parallel", "parallel"),
            vmem_limit_bytes=VMEM_LIMIT),
        name="diff_attention",
    )(sl3, lam_arr, q, k, v, row(lq1), row(lk1), row(lq2), row(lk2), row(subg))


def _ssm_body(u_ref, bblk_ref, are_ref, aim_ref, cblk_ref, d_ref, y_ref,
              x_scr, h_scr, st_scr, *, batch, tt):
    nc = st_scr.shape[-1] // 2
    gcols = u_ref.shape[-1]

    @pl.when(pl.program_id(1) == 0)
    def _():
        st_scr[...] = jnp.zeros_like(st_scr)

    u_f = pltpu.einshape("btn->tbn", u_ref[...].astype(F32)).reshape(tt * batch, gcols)
    x_scr[...] = jnp.dot(u_f.astype(BF16), bblk_ref[...], preferred_element_type=F32)
    ar = jnp.broadcast_to(are_ref[...], (batch, nc))
    ai = jnp.broadcast_to(aim_ref[...], (batch, nc))

    def step(t, carry):
        hr, hi = carry
        r0 = pl.multiple_of(t * batch, batch)
        xr = x_scr[pl.ds(r0, batch), :nc]
        xi = x_scr[pl.ds(r0, batch), nc:]
        nr = ar * hr - ai * hi + xr
        ni = ar * hi + ai * hr + xi
        h_scr[pl.ds(r0, batch), :nc] = nr.astype(BF16)
        h_scr[pl.ds(r0, batch), nc:] = ni.astype(BF16)
        return nr, ni

    hr, hi = lax.fori_loop(0, tt, step, (st_scr[:, :nc], st_scr[:, nc:]), unroll=4)
    st_scr[:, :nc] = hr
    st_scr[:, nc:] = hi
    y = jnp.dot(h_scr[...], cblk_ref[...], preferred_element_type=F32)
    y = _gelu_tanh(y + d_ref[...] * u_f)
    y_ref[...] = pltpu.einshape("tbn->btn", y.reshape(tt, batch, gcols)).astype(BF16)


def _ssm(u3, bblk, are, aim, cblk, d_skip, *, tt=128):
    batch, seq, _ = u3.shape
    gcols = GROUPS_PER_TILE * SSM_GROUP
    ngt = N_GROUPS // GROUPS_PER_TILE
    tt = min(tt, seq)
    rows = tt * batch
    return pl.pallas_call(
        functools.partial(_ssm_body, batch=batch, tt=tt),
        grid=(ngt, seq // tt),
        in_specs=[
            pl.BlockSpec((batch, tt, gcols), lambda g, t: (0, t, g)),
            pl.BlockSpec((None, gcols, 2 * SSM_NC), lambda g, t: (g, 0, 0)),
            pl.BlockSpec((None, 1, SSM_NC), lambda g, t: (g, 0, 0)),
            pl.BlockSpec((None, 1, SSM_NC), lambda g, t: (g, 0, 0)),
            pl.BlockSpec((None, 2 * SSM_NC, gcols), lambda g, t: (g, 0, 0)),
            pl.BlockSpec((1, gcols), lambda g, t: (0, g)),
        ],
        out_specs=pl.BlockSpec((batch, tt, gcols), lambda g, t: (0, t, g)),
        out_shape=jax.ShapeDtypeStruct((batch, seq, D_SSM), BF16),
        scratch_shapes=[pltpu.VMEM((rows, 2 * SSM_NC), F32),
                        pltpu.VMEM((rows, 2 * SSM_NC), BF16),
                        pltpu.VMEM((batch, 2 * SSM_NC), F32)],
        compiler_params=pltpu.CompilerParams(
            dimension_semantics=("parallel", "arbitrary"),
            vmem_limit_bytes=VMEM_LIMIT),
        name="s5_scan",
    )(u3, bblk, are, aim, cblk, d_skip.astype(F32).reshape(1, D_SSM))


def _ssm_params(a_re, a_im, log_dt, b_re, b_im, c_re, c_im):
    gpt = GROUPS_PER_TILE
    ngt = N_GROUPS // gpt
    dt = jnp.exp(log_dt)[:, None]
    mag = jnp.exp(dt * a_re)
    ab_re = mag * jnp.cos(dt * a_im)
    ab_im = mag * jnp.sin(dt * a_im)
    den = a_re * a_re + a_im * a_im
    zr = ab_re - 1.0
    zi = ab_im
    f_re = (zr * a_re + zi * a_im) / den
    f_im = (zi * a_re - zr * a_im) / den
    bb_re = f_re[..., None] * b_re - f_im[..., None] * b_im
    bb_im = f_re[..., None] * b_im + f_im[..., None] * b_re
    eye = jnp.eye(gpt, dtype=F32)

    def bdiag_in(bb):
        t = bb.reshape(ngt, gpt, SSM_STATE, SSM_GROUP)
        return jnp.einsum('tgpi,gh->tgihp', t, eye).reshape(ngt, gpt * SSM_GROUP, SSM_NC)

    def bdiag_out(cc):
        t = cc.reshape(ngt, gpt, SSM_GROUP, SSM_STATE)
        return jnp.einsum('tgop,gh->tgpho', t, eye).reshape(ngt, SSM_NC, gpt * SSM_GROUP)

    bblk = jnp.concatenate([bdiag_in(bb_re), bdiag_in(bb_im)], axis=-1).astype(BF16)
    cblk = jnp.concatenate([bdiag_out(c_re), -bdiag_out(c_im)], axis=1).astype(BF16)
    return bblk, ab_re.reshape(ngt, 1, SSM_NC), ab_im.reshape(ngt, 1, SSM_NC), cblk


def _mix_body(y_ref, oa_ref, gate_ref, x_ref, gw_ref, gb_ref, wba_ref, wbs_ref, wo_ref, out_ref):
    y = y_ref[...]
    z = jnp.dot(y, gw_ref[...], preferred_element_type=F32) + gb_ref[...]
    y2 = (y.astype(F32) * jax.nn.sigmoid(z)).astype(BF16)
    o_ssm = jnp.dot(y2, wbs_ref[...], preferred_element_type=F32)
    o_att = jnp.dot(oa_ref[...], wba_ref[...], preferred_element_type=F32)
    mixed = (jax.nn.sigmoid(gate_ref[:, :D_MODEL].astype(F32)) * o_att
             + jax.nn.sigmoid(gate_ref[:, D_MODEL:].astype(F32)) * o_ssm)
    out_ref[...] = x_ref[...] + jnp.dot(mixed.astype(BF16), wo_ref[...],
                                        preferred_element_type=F32)


def _mix(y, o_att, gates, xf, glu_w, glu_b, w_ba, w_bs, w_out, batch, seq, *, tm=256):
    t = xf.shape[0]
    tm = min(tm, seq)
    full = lambda shape: pl.BlockSpec(shape, lambda i: (0, 0), pipeline_mode=pl.Buffered(1))
    return pl.pallas_call(
        _mix_body,
        grid=(t // tm,),
        in_specs=[
            pl.BlockSpec((tm, D_SSM), lambda i: (i, 0)),
            pl.BlockSpec((tm, D_V), lambda i: (i, 0)),
            pl.BlockSpec((tm, 2 * D_MODEL), lambda i: (i, 0)),
            pl.BlockSpec((tm, D_MODEL), lambda i: (i, 0)),
            full((D_SSM, D_SSM)), full((1, D_SSM)),
            full((D_V, D_MODEL)), full((D_SSM, D_MODEL)), full((D_MODEL, D_MODEL)),
        ],
        out_specs=pl.BlockSpec((tm, D_MODEL), lambda i: (i, 0)),
        out_shape=jax.ShapeDtypeStruct((t, D_MODEL), F32),
        compiler_params=pltpu.CompilerParams(
            dimension_semantics=("parallel",),
            vmem_limit_bytes=VMEM_LIMIT),
        name="gated_merge",
    )(y, o_att, gates, xf, glu_w, glu_b.astype(F32).reshape(1, D_SSM), w_ba, w_bs, w_out)


def _shift_rows(up, prev, k):
    body = pltpu.roll(up, k, axis=0)
    top = pltpu.roll(jnp.concatenate([prev, up[:8]], axis=0), k, axis=0)[8:]
    return jnp.concatenate([top, body[8:]], axis=0)


def _ffn_up_body(x_ref, g_ref, wa_ref, wv_ref, cwa_ref, cwv_ref, cba_ref, cbv_ref,
                 act_ref, h_scr, carry_a, carry_v, *, nsb):
    i = pl.program_id(0)
    j = pl.program_id(1)

    @pl.when(j == 0)
    def _():
        h_scr[...] = _rms_rows(x_ref[...], g_ref[...]).astype(BF16)

    h = h_scr[...]
    seq_start = i % nsb == 0

    def conv(w_ref, cw_ref, cb_ref, carry):
        up = jnp.dot(h, w_ref[...], preferred_element_type=F32)
        prev = jnp.where(seq_start, 0.0, carry[j])
        carry[j] = up[up.shape[0] - 8:]
        cw = cw_ref[...]
        return (cb_ref[...] + cw[0:1] * _shift_rows(up, prev, 2)
                + cw[1:2] * _shift_rows(up, prev, 1) + cw[2:3] * up)

    a = conv(wa_ref, cwa_ref, cba_ref, carry_a)
    val = conv(wv_ref, cwv_ref, cbv_ref, carry_v)
    act_ref[...] = (_gelu_tanh(a) * val).astype(BF16)


def _ffn_up(xf, g, wa, wv, cwa, cwv, cba, cbv, seq, *, tm=1024, tn=FF_TILE):
    t = xf.shape[0]
    tm = min(tm, seq)
    ncol = D_FF_PAD // tn
    const = lambda i, j: (0, 0)
    col = lambda i, j: (0, j)
    return pl.pallas_call(
        functools.partial(_ffn_up_body, nsb=seq // tm),
        grid=(t // tm, ncol),
        in_specs=[
            pl.BlockSpec((tm, D_MODEL), lambda i, j: (i, 0)),
            pl.BlockSpec((1, D_MODEL), const),
            pl.BlockSpec((D_MODEL, tn), col),
            pl.BlockSpec((D_MODEL, tn), col),
            pl.BlockSpec((CONV_W, tn), col),
            pl.BlockSpec((CONV_W, tn), col),
            pl.BlockSpec((1, tn), col),
            pl.BlockSpec((1, tn), col),
        ],
        out_specs=pl.BlockSpec((tm, tn), lambda i, j: (i, j)),
        out_shape=jax.ShapeDtypeStruct((t, D_FF_PAD), BF16),
        scratch_shapes=[pltpu.VMEM((tm, D_MODEL), BF16),
                        pltpu.VMEM((ncol, 8, tn), F32),
                        pltpu.VMEM((ncol, 8, tn), F32)],
        compiler_params=pltpu.CompilerParams(
            dimension_semantics=("arbitrary", "arbitrary"),
            vmem_limit_bytes=VMEM_LIMIT),
        name="ffn_up_conv_gate",
    )(xf, g.reshape(1, D_MODEL), wa, wv, cwa, cwv, cba, cbv)


def _ffn_down_body(act_ref, w_ref, x_ref, out_ref):
    out_ref[...] = x_ref[...] + jnp.dot(act_ref[...], w_ref[...], preferred_element_type=F32)


def _ffn_down(act, w_down, xf, *, tm=1024, tn=512):
    t = xf.shape[0]
    tm = min(tm, t)
    return pl.pallas_call(
        _ffn_down_body,
        grid=(t // tm, D_MODEL // tn),
        in_specs=[
            pl.BlockSpec((tm, D_FF_PAD), lambda i, j: (i, 0)),
            pl.BlockSpec((D_FF_PAD, tn), lambda i, j: (0, j)),
            pl.BlockSpec((tm, tn), lambda i, j: (i, j)),
        ],
        out_specs=pl.BlockSpec((tm, tn), lambda i, j: (i, j)),
        out_shape=jax.ShapeDtypeStruct((t, D_MODEL), F32),
        compiler_params=pltpu.CompilerParams(
            dimension_semantics=("parallel", "parallel"),
            vmem_limit_bytes=VMEM_LIMIT),
        name="ffn_down",
    )(act, w_down, xf)


def _ffn_body(x_ref, g_ref, wa_ref, wv_ref, cwa_ref, cwv_ref, cba_ref, cbv_ref, wd_ref,
              out_ref, h_scr, act_scr, acc_scr, raw_a, raw_v, carry_a, carry_v, *, batch, tt, ncol):
    i = pl.program_id(0)
    j = pl.program_id(1)
    slot = j % 2
    rows = tt * batch
    hist = (CONV_W - 1) * batch

    @pl.when(j == 0)
    def _():
        x_tb = pltpu.einshape("btn->tbn", x_ref[...]).reshape(rows, D_MODEL)
        h_scr[...] = _rms_rows(x_tb, g_ref[...]).astype(BF16)
        acc_scr[...] = jnp.zeros_like(acc_scr)

    nsub = wa_ref.shape[1] // MXU_DIM
    sub = lambda c: slice(c * MXU_DIM, (c + 1) * MXU_DIM)

    def up_dots():
        h = h_scr[...]
        for c in range(nsub):
            raw_a[:, sub(c)] = jnp.dot(h, wa_ref[:, sub(c)], preferred_element_type=F32)
            raw_v[:, sub(c)] = jnp.dot(h, wv_ref[:, sub(c)], preferred_element_type=F32)

    def down_dot():
        return jnp.dot(act_scr[1 - slot], wd_ref[...], preferred_element_type=F32)

    def gate():
        def conv(raw, c, cw_ref, cb_ref, carry):
            up = raw[:, sub(c)]
            prev = jnp.where(i == 0, 0.0, carry[j, :, sub(c)])
            carry[j, :, sub(c)] = up[rows - hist:]
            ext = jnp.concatenate([prev, up], axis=0)
            cw = cw_ref[:, sub(c)]
            out = cb_ref[:, sub(c)] + cw[CONV_W - 1:CONV_W] * up
            for tap in range(CONV_W - 1):
                out = out + cw[tap:tap + 1] * ext[tap * batch:tap * batch + rows]
            return out

        for c in range(nsub):
            a = conv(raw_a, c, cwa_ref, cba_ref, carry_a)
            val = conv(raw_v, c, cwv_ref, cbv_ref, carry_v)
            act_scr[slot, :, sub(c)] = (_gelu_tanh(a) * val).astype(BF16)

    @pl.when(j == 0)
    def _():
        up_dots()
        gate()

    @pl.when((j > 0) & (j < ncol))
    def _():
        up_dots()
        acc_scr[...] += down_dot()
        gate()

    @pl.when(j == ncol)
    def _():
        y = acc_scr[...] + down_dot()
        out_ref[...] = x_ref[...] + pltpu.einshape("tbn->btn", y.reshape(tt, batch, D_MODEL))


def _ffn(x3, g, wa, wv, cwa, cwv, cba, cbv, w_down, *, tm=512, tn=FF_TILE):
    batch, seq, _ = x3.shape
    tt = min(tm // batch, seq)
    rows = tt * batch
    ncol = D_FF_PAD // tn
    hist = (CONV_W - 1) * batch
    const = lambda i, j: (0, 0)
    col = lambda i, j: (0, jnp.minimum(j, ncol - 1))
    return pl.pallas_call(
        functools.partial(_ffn_body, batch=batch, tt=tt, ncol=ncol),
        grid=(seq // tt, ncol + 1),
        in_specs=[
            pl.BlockSpec((batch, tt, D_MODEL), lambda i, j: (0, i, 0)),
            pl.BlockSpec((1, D_MODEL), const),
            pl.BlockSpec((D_MODEL, tn), col),
            pl.BlockSpec((D_MODEL, tn), col),
            pl.BlockSpec((CONV_W, tn), col),
            pl.BlockSpec((CONV_W, tn), col),
            pl.BlockSpec((1, tn), col),
            pl.BlockSpec((1, tn), col),
            pl.BlockSpec((tn, D_MODEL), lambda i, j: (jnp.maximum(j - 1, 0), 0)),
        ],
        out_specs=pl.BlockSpec((batch, tt, D_MODEL), lambda i, j: (0, i, 0)),
        out_shape=jax.ShapeDtypeStruct((batch, seq, D_MODEL), F32),
        scratch_shapes=[pltpu.VMEM((rows, D_MODEL), BF16),
                        pltpu.VMEM((2, rows, tn), BF16),
                        pltpu.VMEM((rows, D_MODEL), F32),
                        pltpu.VMEM((rows, tn), F32), pltpu.VMEM((rows, tn), F32),
                        pltpu.VMEM((ncol, hist, tn), F32),
                        pltpu.VMEM((ncol, hist, tn), F32)],
        compiler_params=pltpu.CompilerParams(
            dimension_semantics=("arbitrary", "arbitrary"),
            vmem_limit_bytes=VMEM_LIMIT),
        name="ffn_fused",
    )(x3, g.reshape(1, D_MODEL), wa, wv, cwa, cwv, cba, cbv, w_down)


def _pad_ff_cols(a):
    return jnp.pad(a, ((0, 0), (0, D_FF_PAD - D_FF)))


def kernel(x, norm1_g, w_in, q_norm_g, k_norm_g, lambda_q1, lambda_k1, lambda_q2, lambda_k2, subln_g, ssm_a_re, ssm_a_im, ssm_log_dt, ssm_b_re, ssm_b_im, ssm_c_re, ssm_c_im, ssm_d, ssm_glu_w, ssm_glu_b, w_branch_attn, w_branch_ssm, w_out, norm2_g, ffn_w_up, ffn_conv_w, ffn_conv_b, ffn_w_down):
    batch, seq, _ = x.shape
    depth = w_in.shape[0]
    xf = x.reshape(batch * seq, D_MODEL)
    for l in range(depth):
        lam_init = 0.8 - 0.6 * math.exp(-0.3 * l)
        q, k, v, u, gates = _in_proj(xf, norm1_g[l], w_in[l].astype(BF16),
                                     q_norm_g[l], k_norm_g[l], batch, seq)
        o_att = _attention(q, k, v, lambda_q1[l], lambda_k1[l], lambda_q2[l], lambda_k2[l],
                           subln_g[l], lam_init, batch, seq)
        bblk, are, aim, cblk = _ssm_params(ssm_a_re[l], ssm_a_im[l], ssm_log_dt[l],
                                           ssm_b_re[l], ssm_b_im[l], ssm_c_re[l], ssm_c_im[l])
        y = _ssm(u.reshape(batch, seq, D_SSM), bblk, are, aim, cblk, ssm_d[l])
        xf = _mix(y.reshape(batch * seq, D_SSM), o_att, gates, xf,
                  ssm_glu_w[l].astype(BF16), ssm_glu_b[l],
                  w_branch_attn[l].astype(BF16), w_branch_ssm[l].astype(BF16),
                  w_out[l].astype(BF16), batch, seq)
        wu = ffn_w_up[l]
        cw = ffn_conv_w[l]
        cb = ffn_conv_b[l].reshape(1, 2 * D_FF)
        w_down = jnp.pad(ffn_w_down[l], ((0, D_FF_PAD - D_FF), (0, 0))).astype(BF16)
        act = _ffn_up(xf, norm2_g[l],
                      _pad_ff_cols(wu[:, :D_FF]).astype(BF16), _pad_ff_cols(wu[:, D_FF:]).astype(BF16),
                      _pad_ff_cols(cw[:, :D_FF]), _pad_ff_cols(cw[:, D_FF:]),
                      _pad_ff_cols(cb[:, :D_FF]), _pad_ff_cols(cb[:, D_FF:]), seq)
        xf = _ffn_down(act, w_down, xf)
    return xf.reshape(batch, seq, D_MODEL)
```

```python
import functools
import math

import jax
import jax.numpy as jnp
from jax import lax
from jax.experimental import pallas as pl
from jax.experimental.pallas import tpu as pltpu

F32 = jnp.float32
BF16 = jnp.bfloat16

D_MODEL = 2048
N_HEADS = 8
HEAD_DIM = 64
V_HEAD_DIM = 2 * HEAD_DIM
D_QK = N_HEADS * 2 * HEAD_DIM
D_V = N_HEADS * V_HEAD_DIM
D_SSM = D_MODEL // 2
SSM_GROUP = 16
N_GROUPS = D_SSM // SSM_GROUP
SSM_STATE = 64
D_FF = 5504
CONV_W = 3
CHUNK = 64
EPS = 1e-6
IN_COLS = 2 * D_QK + D_V + D_SSM + 2 * D_MODEL

MXU_DIM = 256
FF_TILE = 512
ROW_CHUNK = 256
D_FF_PAD = ((D_FF + FF_TILE - 1) // FF_TILE) * FF_TILE
GROUPS_PER_TILE = 8
SSM_NC = GROUPS_PER_TILE * SSM_STATE
NEG_BIG = -1e30
Q_SCALE = HEAD_DIM ** -0.5 * math.log2(math.e)
VMEM_LIMIT = 56 * 1024 * 1024


def _gelu_tanh(x):
    c = math.sqrt(2.0 / math.pi)
    return x * (0.5 * (1.0 + jnp.tanh(c * (x + 0.044715 * (x * x * x)))))


def _rms_rows(xf, g):
    ms = jnp.mean(xf * xf, axis=-1, keepdims=True)
    return xf * lax.rsqrt(ms + EPS) * g


def _in_proj_body(x_ref, g_ref, w_ref, qg_ref, kg_ref, ones_ref,
                  q_ref, k_ref, v_ref, u_ref, gate_ref, h_scr, *, tn):
    j = pl.program_id(1)
    nq = D_QK // tn

    @pl.when(j == 0)
    def _():
        h_scr[...] = _rms_rows(x_ref[...], g_ref[...]).astype(BF16)

    tm = h_scr.shape[0]
    rc = min(ROW_CHUNK, tm)

    def tile(out_ref, epilogue):
        nchunk = tm // rc
        rows = lambda q: slice(q * rc, (q + 1) * rc)
        dot_q = lambda q: jnp.dot(h_scr[rows(q), :], w_ref[...], preferred_element_type=F32)
        acc = dot_q(0)
        for q in range(1, nchunk):
            nxt = dot_q(q)
            out_ref[rows(q - 1), :] = epilogue(acc).astype(BF16)
            acc = nxt
        out_ref[rows(nchunk - 1), :] = epilogue(acc).astype(BF16)

    def head_norm(gain_ref, scale):
        def epilogue(acc):
            outs = []
            for c in range(tn // MXU_DIM):
                a = acc[:, c * MXU_DIM:(c + 1) * MXU_DIM]
                ss = jnp.dot((a * a).astype(BF16), ones_ref[...], preferred_element_type=F32)
                outs.append(a * lax.rsqrt(ss * (1.0 / HEAD_DIM) + EPS) * (gain_ref[...] * scale))
            return jnp.concatenate(outs, axis=1)
        return epilogue

    plain = lambda acc: acc

    @pl.when(j < nq)
    def _():
        tile(q_ref, head_norm(qg_ref, Q_SCALE))

    @pl.when((j >= nq) & (j < 2 * nq))
    def _():
        tile(k_ref, head_norm(kg_ref, 1.0))

    @pl.when((j >= 2 * nq) & (j < 3 * nq))
    def _():
        tile(v_ref, plain)

    @pl.when((j >= 3 * nq) & (j < 4 * nq))
    def _():
        tile(u_ref, plain)

    @pl.when(j >= 4 * nq)
    def _():
        tile(gate_ref, plain)


def _in_proj(xf, g, w_bf, qg, kg, batch, seq, *, tm=1024, tn=512):
    t = xf.shape[0]
    tm = min(tm, seq)
    nq = D_QK // tn
    ng = 2 * D_MODEL // tn
    qg_t = jnp.tile(qg.astype(F32), MXU_DIM // HEAD_DIM).reshape(1, MXU_DIM)
    kg_t = jnp.tile(kg.astype(F32), MXU_DIM // HEAD_DIM).reshape(1, MXU_DIM)
    seg = jnp.arange(MXU_DIM) // HEAD_DIM
    ones_bd = (seg[:, None] == seg[None, :]).astype(BF16)

    def cl(j, lo, n):
        return jnp.clip(j - lo, 0, n - 1)

    const = lambda i, j: (0, 0)
    return pl.pallas_call(
        functools.partial(_in_proj_body, tn=tn),
        grid=(t // tm, IN_COLS // tn),
        in_specs=[
            pl.BlockSpec((tm, D_MODEL), lambda i, j: (i, 0)),
            pl.BlockSpec((1, D_MODEL), const),
            pl.BlockSpec((D_MODEL, tn), lambda i, j: (0, j)),
            pl.BlockSpec((1, MXU_DIM), const),
            pl.BlockSpec((1, MXU_DIM), const),
            pl.BlockSpec((MXU_DIM, MXU_DIM), const),
        ],
        out_specs=[
            pl.BlockSpec((tm, tn), lambda i, j: (i, cl(j, 0, nq))),
            pl.BlockSpec((tm, tn), lambda i, j: (i, cl(j, nq, nq))),
            pl.BlockSpec((tm, tn), lambda i, j: (i, cl(j, 2 * nq, nq))),
            pl.BlockSpec((tm, tn), lambda i, j: (i, cl(j, 3 * nq, nq))),
            pl.BlockSpec((tm, tn), lambda i, j: (i, cl(j, 4 * nq, ng))),
        ],
        out_shape=[
            jax.ShapeDtypeStruct((t, D_QK), BF16),
            jax.ShapeDtypeStruct((t, D_QK), BF16),
            jax.ShapeDtypeStruct((t, D_V), BF16),
            jax.ShapeDtypeStruct((t, D_SSM), BF16),
            jax.ShapeDtypeStruct((t, 2 * D_MODEL), BF16),
        ],
        scratch_shapes=[pltpu.VMEM((tm, D_MODEL), BF16)],
        compiler_params=pltpu.CompilerParams(
            dimension_semantics=("arbitrary", "arbitrary"),
            vmem_limit_bytes=VMEM_LIMIT),
        name="in_proj",
    )(xf, g.reshape(1, D_MODEL), w_bf, qg_t, kg_t, ones_bd)


N_BIAS_ROWS = 3
BIAS_LANE0 = 0


def _attn_body(sl_ref, laminit_ref, q_ref, k_ref, v_ref, lq1_ref, lk1_ref, lq2_ref, lk2_ref,
               subg_ref, o_ref, kaug_scr, vt_scr, dmat_scr, *, seq, tq, hpb):
    tk = tq
    nblk = seq // tk
    hd2 = 2 * HEAD_DIM
    nch = 2 * hpb
    hg = pl.program_id(1)
    lam_init = laminit_ref[0]
    lam = (jnp.exp(jnp.sum(lq1_ref[...] * lk1_ref[...], axis=-1, keepdims=True))
           - jnp.exp(jnp.sum(lq2_ref[...] * lk2_ref[...], axis=-1, keepdims=True))
           + lam_init)
    subg = subg_ref[...] * (1.0 - lam_init)

    koff = lax.broadcasted_iota(jnp.int32, (tk, hd2), 0)
    klane = lax.broadcasted_iota(jnp.int32, (tk, hd2), 1)
    bias_lo = BIAS_LANE0
    bias_hi = HEAD_DIM + BIAS_LANE0
    kbias = [jnp.where((klane >= lo) & (klane < lo + N_BIAS_ROWS), koff, 0).astype(F32).astype(BF16)
             for lo in (bias_hi, bias_lo)]
    c = lax.broadcasted_iota(jnp.int32, (tk, tq), 0)
    r = lax.broadcasted_iota(jnp.int32, (tk, tq), 1)
    visible = (c // CHUNK) <= (r // CHUNK)
    ahead = jnp.maximum(c - r, 0).astype(F32)
    brow_i = lax.broadcasted_iota(jnp.int32, (hd2, tq), 0)
    sl2 = []
    brows = []
    for hh in range(hpb):
        h = hg * hpb + hh
        s_hi, s_mid, s_lo = sl_ref[h, 0], sl_ref[h, 1], sl_ref[h, 2]
        sl2.append(s_hi + s_mid + s_lo)
        brows.append([jnp.where(brow_i == lo, s_hi, jnp.where(brow_i == lo + 1, s_mid,
                      jnp.where(brow_i == lo + 2, s_lo, 0.0))) for lo in (bias_hi, bias_lo)])
        for blk in range(nblk):
            blk_rows = slice(blk * tk, (blk + 1) * tk)
            k_blk = k_ref[blk_rows, hh * hd2:(hh + 1) * hd2]
            kaug_scr[hh, 0, blk_rows, :] = jnp.where(klane < HEAD_DIM, k_blk, kbias[0])
            kaug_scr[hh, 1, blk_rows, :] = jnp.where(klane >= HEAD_DIM, k_blk, kbias[1])
            vt_scr[hh, blk] = v_ref[blk_rows, hh * hd2:(hh + 1) * hd2].astype(F32).T.astype(BF16)
        dmat_scr[hh] = jnp.where(visible, (-2.0 * sl2[hh]) * ahead, NEG_BIG)

    def q_block(iq, _):
        q0 = pl.multiple_of(iq * tq, tq)
        qas = []
        for hh in range(hpb):
            qt = q_ref[pl.ds(q0, tq), hh * hd2:(hh + 1) * hd2].astype(F32).T
            for comp in range(2):
                keep = (brow_i < HEAD_DIM) if comp == 0 else (brow_i >= HEAD_DIM)
                qas.append(jnp.where(keep, qt, brows[hh][comp]).astype(BF16))
        ss = [jnp.dot(kaug_scr[idx // 2, idx % 2, pl.ds(q0, tk), :], qas[idx],
                      preferred_element_type=F32) + dmat_scr[idx // 2] for idx in range(nch)]
        ps, stats = [], []
        for idx in range(nch):
            m = jnp.max(ss[idx], axis=0, keepdims=True)
            p = jnp.exp2(ss[idx] - m)
            stats.append((m, jnp.sum(p, axis=0, keepdims=True)))
            ps.append(p.astype(BF16))
        state = []
        for idx in range(nch):
            acc = jnp.dot(vt_scr[idx // 2, iq], ps[idx], preferred_element_type=F32)
            state.extend([stats[idx][0], stats[idx][1], acc])

        def kv_block(j, carry):
            k0 = pl.multiple_of(j * tk, tk)
            boff = lax.convert_element_type((j - iq) * tk, F32)
            ss = [jnp.dot(kaug_scr[idx // 2, idx % 2, pl.ds(k0, tk), :], qas[idx],
                          preferred_element_type=F32) for idx in range(nch)]
            ps, stats = [], []
            for idx in range(nch):
                m, l, _ = carry[3 * idx:3 * idx + 3]
                bc = sl2[idx // 2] * boff
                m_new = jnp.maximum(m, jnp.max(ss[idx], axis=0, keepdims=True) + bc)
                alpha = jnp.exp2(m - m_new)
                p = jnp.exp2(ss[idx] - (m_new - bc))
                stats.append((m_new, alpha * l + jnp.sum(p, axis=0, keepdims=True), alpha))
                ps.append(p.astype(BF16))
            out = []
            for idx in range(nch):
                m_new, l_new, alpha = stats[idx]
                acc = alpha * carry[3 * idx + 2] + jnp.dot(vt_scr[idx // 2, j], ps[idx],
                                                           preferred_element_type=F32)
                out.extend([m_new, l_new, acc])
            return tuple(out)

        state = lax.fori_loop(0, iq, kv_block, tuple(state))
        for hh in range(hpb):
            _, l0, a0, _, l1, a1 = state[6 * hh:6 * hh + 6]
            ot = a0 * (1.0 / l0) - a1 * (lam / l1)
            ot = ot * lax.rsqrt(jnp.mean(ot * ot, axis=0, keepdims=True) + EPS)
            o_ref[pl.ds(q0, tq), hh * hd2:(hh + 1) * hd2] = (ot.T * subg).astype(BF16)
        return 0

    lax.fori_loop(0, seq // tq, q_block, 0)


def _attention(q, k, v, lq1, lk1, lq2, lk2, subg, lam_init, batch, seq, *, tq=256, hpb=4):
    slopes = 2.0 ** (-8.0 * jnp.arange(1, N_HEADS + 1, dtype=F32) / N_HEADS)
    sl = slopes * math.log2(math.e)
    s_hi = sl.astype(BF16).astype(F32)
    s_mid = (sl - s_hi).astype(BF16).astype(F32)
    s_lo = (sl - s_hi - s_mid).astype(BF16).astype(F32)
    sl3 = jnp.stack([s_hi, s_mid, s_lo], axis=1)
    lam_arr = jnp.full((1,), lam_init, F32)
    smem = pl.BlockSpec(memory_space=pltpu.SMEM)
    hw = hpb * V_HEAD_DIM
    head_blk = pl.BlockSpec((seq, hw), lambda b, h: (b, h))
    vec = lambda n: pl.BlockSpec((1, n), lambda b, h: (0, 0))
    row = lambda a: a.astype(F32).reshape(1, -1)
    return pl.pallas_call(
        functools.partial(_attn_body, seq=seq, tq=tq, hpb=hpb),
        grid=(batch, N_HEADS // hpb),
        in_specs=[smem, smem, head_blk, head_blk, head_blk,
                  vec(HEAD_DIM), vec(HEAD_DIM), vec(HEAD_DIM), vec(HEAD_DIM), vec(V_HEAD_DIM)],
        out_specs=head_blk,
        out_shape=jax.ShapeDtypeStruct((batch * seq, D_V), BF16),
        scratch_shapes=[pltpu.VMEM((hpb, 2, seq, V_HEAD_DIM), BF16),
                        pltpu.VMEM((hpb, seq // tq, V_HEAD_DIM, tq), BF16),
                        pltpu.VMEM((hpb, tq, tq), F32)],
        compiler_params=pltpu.CompilerParams(
            dimension_semantics=("parallel", "parallel"),
            vmem_limit_bytes=VMEM_LIMIT),
        name="diff_attention",
    )(sl3, lam_arr, q, k, v, row(lq1), row(lk1), row(lq2), row(lk2), row(subg))


def _ssm_body(u_ref, bblk_ref, are_ref, aim_ref, cblk_ref, d_ref, y_ref,
              x_scr, h_scr, st_scr, *, batch, tt):
    nc = st_scr.shape[-1] // 2
    gcols = u_ref.shape[-1]

    @pl.when(pl.program_id(1) == 0)
    def _():
        st_scr[...] = jnp.zeros_like(st_scr)

    u_f = pltpu.einshape("btn->tbn", u_ref[...].astype(F32)).reshape(tt * batch, gcols)
    x_scr[...] = jnp.dot(u_f.astype(BF16), bblk_ref[...], preferred_element_type=F32)
    ar = jnp.broadcast_to(are_ref[...], (batch, nc))
    ai = jnp.broadcast_to(aim_ref[...], (batch, nc))

    def step(t, carry):
        hr, hi = carry
        r0 = pl.multiple_of(t * batch, batch)
        xr = x_scr[pl.ds(r0, batch), :nc]
        xi = x_scr[pl.ds(r0, batch), nc:]
        nr = ar * hr - ai * hi + xr
        ni = ar * hi + ai * hr + xi
        h_scr[pl.ds(r0, batch), :nc] = nr.astype(BF16)
        h_scr[pl.ds(r0, batch), nc:] = ni.astype(BF16)
        return nr, ni

    hr, hi = lax.fori_loop(0, tt, step, (st_scr[:, :nc], st_scr[:, nc:]), unroll=4)
    st_scr[:, :nc] = hr
    st_scr[:, nc:] = hi
    y = jnp.dot(h_scr[...], cblk_ref[...], preferred_element_type=F32)
    y = _gelu_tanh(y + d_ref[...] * u_f)
    y_ref[...] = pltpu.einshape("tbn->btn", y.reshape(tt, batch, gcols)).astype(BF16)


def _ssm(u3, bblk, are, aim, cblk, d_skip, *, tt=128):
    batch, seq, _ = u3.shape
    gcols = GROUPS_PER_TILE * SSM_GROUP
    ngt = N_GROUPS // GROUPS_PER_TILE
    tt = min(tt, seq)
    rows = tt * batch
    return pl.pallas_call(
        functools.partial(_ssm_body, batch=batch, tt=tt),
        grid=(ngt, seq // tt),
        in_specs=[
            pl.BlockSpec((batch, tt, gcols), lambda g, t: (0, t, g)),
            pl.BlockSpec((None, gcols, 2 * SSM_NC), lambda g, t: (g, 0, 0)),
            pl.BlockSpec((None, 1, SSM_NC), lambda g, t: (g, 0, 0)),
            pl.BlockSpec((None, 1, SSM_NC), lambda g, t: (g, 0, 0)),
            pl.BlockSpec((None, 2 * SSM_NC, gcols), lambda g, t: (g, 0, 0)),
            pl.BlockSpec((1, gcols), lambda g, t: (0, g)),
        ],
        out_specs=pl.BlockSpec((batch, tt, gcols), lambda g, t: (0, t, g)),
        out_shape=jax.ShapeDtypeStruct((batch, seq, D_SSM), BF16),
        scratch_shapes=[pltpu.VMEM((rows, 2 * SSM_NC), F32),
                        pltpu.VMEM((rows, 2 * SSM_NC), BF16),
                        pltpu.VMEM((batch, 2 * SSM_NC), F32)],
        compiler_params=pltpu.CompilerParams(
            dimension_semantics=("parallel", "arbitrary"),
            vmem_limit_bytes=VMEM_LIMIT),
        name="s5_scan",
    )(u3, bblk, are, aim, cblk, d_skip.astype(F32).reshape(1, D_SSM))


def _ssm_params(a_re, a_im, log_dt, b_re, b_im, c_re, c_im):
    gpt = GROUPS_PER_TILE
    ngt = N_GROUPS // gpt
    dt = jnp.exp(log_dt)[:, None]
    mag = jnp.exp(dt * a_re)
    ab_re = mag * jnp.cos(dt * a_im)
    ab_im = mag * jnp.sin(dt * a_im)
    den = a_re * a_re + a_im * a_im
    zr = ab_re - 1.0
    zi = ab_im
    f_re = (zr * a_re + zi * a_im) / den
    f_im = (zi * a_re - zr * a_im) / den
    bb_re = f_re[..., None] * b_re - f_im[..., None] * b_im
    bb_im = f_re[..., None] * b_im + f_im[..., None] * b_re
    eye = jnp.eye(gpt, dtype=F32)

    def bdiag_in(bb):
        t = bb.reshape(ngt, gpt, SSM_STATE, SSM_GROUP)
        return jnp.einsum('tgpi,gh->tgihp', t, eye).reshape(ngt, gpt * SSM_GROUP, SSM_NC)

    def bdiag_out(cc):
        t = cc.reshape(ngt, gpt, SSM_GROUP, SSM_STATE)
        return jnp.einsum('tgop,gh->tgpho', t, eye).reshape(ngt, SSM_NC, gpt * SSM_GROUP)

    bblk = jnp.concatenate([bdiag_in(bb_re), bdiag_in(bb_im)], axis=-1).astype(BF16)
    cblk = jnp.concatenate([bdiag_out(c_re), -bdiag_out(c_im)], axis=1).astype(BF16)
    return bblk, ab_re.reshape(ngt, 1, SSM_NC), ab_im.reshape(ngt, 1, SSM_NC), cblk


def _mix_body(y_ref, oa_ref, gate_ref, x_ref, gw_ref, gb_ref, wba_ref, wbs_ref, wo_ref, out_ref):
    y = y_ref[...]
    z = jnp.dot(y, gw_ref[...], preferred_element_type=F32) + gb_ref[...]
    y2 = (y.astype(F32) * jax.nn.sigmoid(z)).astype(BF16)
    o_ssm = jnp.dot(y2, wbs_ref[...], preferred_element_type=F32)
    o_att = jnp.dot(oa_ref[...], wba_ref[...], preferred_element_type=F32)
    mixed = (jax.nn.sigmoid(gate_ref[:, :D_MODEL].astype(F32)) * o_att
             + jax.nn.sigmoid(gate_ref[:, D_MODEL:].astype(F32)) * o_ssm)
    out_ref[...] = x_ref[...] + jnp.dot(mixed.astype(BF16), wo_ref[...],
                                        preferred_element_type=F32)


def _mix(y, o_att, gates, xf, glu_w, glu_b, w_ba, w_bs, w_out, batch, seq, *, tm=256):
    t = xf.shape[0]
    tm = min(tm, seq)
    full = lambda shape: pl.BlockSpec(shape, lambda i: (0, 0), pipeline_mode=pl.Buffered(1))
    return pl.pallas_call(
        _mix_body,
        grid=(t // tm,),
        in_specs=[
            pl.BlockSpec((tm, D_SSM), lambda i: (i, 0)),
            pl.BlockSpec((tm, D_V), lambda i: (i, 0)),
            pl.BlockSpec((tm, 2 * D_MODEL), lambda i: (i, 0)),
            pl.BlockSpec((tm, D_MODEL), lambda i: (i, 0)),
            full((D_SSM, D_SSM)), full((1, D_SSM)),
            full((D_V, D_MODEL)), full((D_SSM, D_MODEL)), full((D_MODEL, D_MODEL)),
        ],
        out_specs=pl.BlockSpec((tm, D_MODEL), lambda i: (i, 0)),
        out_shape=jax.ShapeDtypeStruct((t, D_MODEL), F32),
        compiler_params=pltpu.CompilerParams(
            dimension_semantics=("parallel",),
            vmem_limit_bytes=VMEM_LIMIT),
        name="gated_merge",
    )(y, o_att, gates, xf, glu_w, glu_b.astype(F32).reshape(1, D_SSM), w_ba, w_bs, w_out)


def _shift_rows(up, prev, k):
    body = pltpu.roll(up, k, axis=0)
    top = pltpu.roll(jnp.concatenate([prev, up[:8]], axis=0), k, axis=0)[8:]
    return jnp.concatenate([top, body[8:]], axis=0)


def _ffn_up_body(x_ref, g_ref, wa_ref, wv_ref, cwa_ref, cwv_ref, cba_ref, cbv_ref,
                 act_ref, h_scr, carry_a, carry_v, *, nsb):
    i = pl.program_id(0)
    j = pl.program_id(1)

    @pl.when(j == 0)
    def _():
        h_scr[...] = _rms_rows(x_ref[...], g_ref[...]).astype(BF16)

    h = h_scr[...]
    seq_start = i % nsb == 0

    def conv(w_ref, cw_ref, cb_ref, carry):
        up = jnp.dot(h, w_ref[...], preferred_element_type=F32)
        prev = jnp.where(seq_start, 0.0, carry[j])
        carry[j] = up[up.shape[0] - 8:]
        cw = cw_ref[...]
        return (cb_ref[...] + cw[0:1] * _shift_rows(up, prev, 2)
                + cw[1:2] * _shift_rows(up, prev, 1) + cw[2:3] * up)

    a = conv(wa_ref, cwa_ref, cba_ref, carry_a)
    val = conv(wv_ref, cwv_ref, cbv_ref, carry_v)
    act_ref[...] = (_gelu_tanh(a) * val).astype(BF16)


def _ffn_up(xf, g, wa, wv, cwa, cwv, cba, cbv, seq, *, tm=1024, tn=FF_TILE):
    t = xf.shape[0]
    tm = min(tm, seq)
    ncol = D_FF_PAD // tn
    const = lambda i, j: (0, 0)
    col = lambda i, j: (0, j)
    return pl.pallas_call(
        functools.partial(_ffn_up_body, nsb=seq // tm),
        grid=(t // tm, ncol),
        in_specs=[
            pl.BlockSpec((tm, D_MODEL), lambda i, j: (i, 0)),
            pl.BlockSpec((1, D_MODEL), const),
            pl.BlockSpec((D_MODEL, tn), col),
            pl.BlockSpec((D_MODEL, tn), col),
            pl.BlockSpec((CONV_W, tn), col),
            pl.BlockSpec((CONV_W, tn), col),
            pl.BlockSpec((1, tn), col),
            pl.BlockSpec((1, tn), col),
        ],
        out_specs=pl.BlockSpec((tm, tn), lambda i, j: (i, j)),
        out_shape=jax.ShapeDtypeStruct((t, D_FF_PAD), BF16),
        scratch_shapes=[pltpu.VMEM((tm, D_MODEL), BF16),
                        pltpu.VMEM((ncol, 8, tn), F32),
                        pltpu.VMEM((ncol, 8, tn), F32)],
        compiler_params=pltpu.CompilerParams(
            dimension_semantics=("arbitrary", "arbitrary"),
            vmem_limit_bytes=VMEM_LIMIT),
        name="ffn_up_conv_gate",
    )(xf, g.reshape(1, D_MODEL), wa, wv, cwa, cwv, cba, cbv)


def _ffn_down_body(act_ref, w_ref, x_ref, out_ref):
    out_ref[...] = x_ref[...] + jnp.dot(act_ref[...], w_ref[...], preferred_element_type=F32)


def _ffn_down(act, w_down, xf, *, tm=1024, tn=512):
    t = xf.shape[0]
    tm = min(tm, t)
    return pl.pallas_call(
        _ffn_down_body,
        grid=(t // tm, D_MODEL // tn),
        in_specs=[
            pl.BlockSpec((tm, D_FF_PAD), lambda i, j: (i, 0)),
            pl.BlockSpec((D_FF_PAD, tn), lambda i, j: (0, j)),
            pl.BlockSpec((tm, tn), lambda i, j: (i, j)),
        ],
        out_specs=pl.BlockSpec((tm, tn), lambda i, j: (i, j)),
        out_shape=jax.ShapeDtypeStruct((t, D_MODEL), F32),
        compiler_params=pltpu.CompilerParams(
            dimension_semantics=("parallel", "parallel"),
            vmem_limit_bytes=VMEM_LIMIT),
        name="ffn_down",
    )(act, w_down, xf)


def _ffn_body(x_ref, g_ref, wa_ref, wv_ref, cwa_ref, cwv_ref, cba_ref, cbv_ref, wd_ref,
              out_ref, h_scr, act_scr, acc_scr, raw_a, raw_v, carry_a, carry_v, *, batch, tt, ncol):
    i = pl.program_id(0)
    j = pl.program_id(1)
    slot = j % 2
    rows = tt * batch
    hist = (CONV_W - 1) * batch

    @pl.when(j == 0)
    def _():
        x_tb = pltpu.einshape("btn->tbn", x_ref[...]).reshape(rows, D_MODEL)
        h_scr[...] = _rms_rows(x_tb, g_ref[...]).astype(BF16)
        acc_scr[...] = jnp.zeros_like(acc_scr)

    nsub = wa_ref.shape[1] // MXU_DIM
    sub = lambda c: slice(c * MXU_DIM, (c + 1) * MXU_DIM)

    def up_dots():
        h = h_scr[...]
        for c in range(nsub):
            raw_a[:, sub(c)] = jnp.dot(h, wa_ref[:, sub(c)], preferred_element_type=F32)
            raw_v[:, sub(c)] = jnp.dot(h, wv_ref[:, sub(c)], preferred_element_type=F32)

    def down_dot():
        return jnp.dot(act_scr[1 - slot], wd_ref[...], preferred_element_type=F32)

    def gate():
        def conv(raw, c, cw_ref, cb_ref, carry):
            up = raw[:, sub(c)]
            prev = jnp.where(i == 0, 0.0, carry[j, :, sub(c)])
            carry[j, :, sub(c)] = up[rows - hist:]
            ext = jnp.concatenate([prev, up], axis=0)
            cw = cw_ref[:, sub(c)]
            out = cb_ref[:, sub(c)] + cw[CONV_W - 1:CONV_W] * up
            for tap in range(CONV_W - 1):
                out = out + cw[tap:tap + 1] * ext[tap * batch:tap * batch + rows]
            return out

        for c in range(nsub):
            a = conv(raw_a, c, cwa_ref, cba_ref, carry_a)
            val = conv(raw_v, c, cwv_ref, cbv_ref, carry_v)
            act_scr[slot, :, sub(c)] = (_gelu_tanh(a) * val).astype(BF16)

    @pl.when(j == 0)
    def _():
        up_dots()
        gate()

    @pl.when((j > 0) & (j < ncol))
    def _():
        up_dots()
        acc_scr[...] += down_dot()
        gate()

    @pl.when(j == ncol)
    def _():
        y = acc_scr[...] + down_dot()
        out_ref[...] = x_ref[...] + pltpu.einshape("tbn->btn", y.reshape(tt, batch, D_MODEL))


def _ffn(x3, g, wa, wv, cwa, cwv, cba, cbv, w_down, *, tm=512, tn=FF_TILE):
    batch, seq, _ = x3.shape
    tt = min(tm // batch, seq)
    rows = tt * batch
    ncol = D_FF_PAD // tn
    hist = (CONV_W - 1) * batch
    const = lambda i, j: (0, 0)
    col = lambda i, j: (0, jnp.minimum(j, ncol - 1))
    return pl.pallas_call(
        functools.partial(_ffn_body, batch=batch, tt=tt, ncol=ncol),
        grid=(seq // tt, ncol + 1),
        in_specs=[
            pl.BlockSpec((batch, tt, D_MODEL), lambda i, j: (0, i, 0)),
            pl.BlockSpec((1, D_MODEL), const),
            pl.BlockSpec((D_MODEL, tn), col),
            pl.BlockSpec((D_MODEL, tn), col),
            pl.BlockSpec((CONV_W, tn), col),
            pl.BlockSpec((CONV_W, tn), col),
            pl.BlockSpec((1, tn), col),
            pl.BlockSpec((1, tn), col),
            pl.BlockSpec((tn, D_MODEL), lambda i, j: (jnp.maximum(j - 1, 0), 0)),
        ],
        out_specs=pl.BlockSpec((batch, tt, D_MODEL), lambda i, j: (0, i, 0)),
        out_shape=jax.ShapeDtypeStruct((batch, seq, D_MODEL), F32),
        scratch_shapes=[pltpu.VMEM((rows, D_MODEL), BF16),
                        pltpu.VMEM((2, rows, tn), BF16),
                        pltpu.VMEM((rows, D_MODEL), F32),
                        pltpu.VMEM((rows, tn), F32), pltpu.VMEM((rows, tn), F32),
                        pltpu.VMEM((ncol, hist, tn), F32),
                        pltpu.VMEM((ncol, hist, tn), F32)],
        compiler_params=pltpu.CompilerParams(
            dimension_semantics=("arbitrary", "arbitrary"),
            vmem_limit_bytes=VMEM_LIMIT),
        name="ffn_fused",
    )(x3, g.reshape(1, D_MODEL), wa, wv, cwa, cwv, cba, cbv, w_down)


def _pad_ff_cols(a):
    return jnp.pad(a, ((0, 0), (0, D_FF_PAD - D_FF)))


def kernel(x, norm1_g, w_in, q_norm_g, k_norm_g, lambda_q1, lambda_k1, lambda_q2, lambda_k2, subln_g, ssm_a_re, ssm_a_im, ssm_log_dt, ssm_b_re, ssm_b_im, ssm_c_re, ssm_c_im, ssm_d, ssm_glu_w, ssm_glu_b, w_branch_attn, w_branch_ssm, w_out, norm2_g, ffn_w_up, ffn_conv_w, ffn_conv_b, ffn_w_down):
    batch, seq, _ = x.shape
    depth = w_in.shape[0]
    xf = x.reshape(batch * seq, D_MODEL)
    for l in range(depth):
        lam_init = 0.8 - 0.6 * math.exp(-0.3 * l)
        q, k, v, u, gates = _in_proj(xf, norm1_g[l], w_in[l].astype(BF16),
                                     q_norm_g[l], k_norm_g[l], batch, seq)
        o_att = _attention(q, k, v, lambda_q1[l], lambda_k1[l], lambda_q2[l], lambda_k2[l],
                           subln_g[l], lam_init, batch, seq)
        bblk, are, aim, cblk = _ssm_params(ssm_a_re[l], ssm_a_im[l], ssm_log_dt[l],
                                           ssm_b_re[l], ssm_b_im[l], ssm_c_re[l], ssm_c_im[l])
        y = _ssm(u.reshape(batch, seq, D_SSM), bblk, are, aim, cblk, ssm_d[l])
        xf = _mix(y.reshape(batch * seq, D_SSM), o_att, gates, xf,
                  ssm_glu_w[l].astype(BF16), ssm_glu_b[l],
                  w_branch_attn[l].astype(BF16), w_branch_ssm[l].astype(BF16),
                  w_out[l].astype(BF16), batch, seq)
        wu = ffn_w_up[l]
        cw = ffn_conv_w[l]
        cb = ffn_conv_b[l].reshape(1, 2 * D_FF)
        w_down = jnp.pad(ffn_w_down[l], ((0, D_FF_PAD - D_FF), (0, 0))).astype(BF16)
        act = _ffn_up(xf, norm2_g[l],
                      _pad_ff_cols(wu[:, :D_FF]).astype(BF16), _pad_ff_cols(wu[:, D_FF:]).astype(BF16),
                      _pad_ff_cols(cw[:, :D_FF]), _pad_ff_cols(cw[:, D_FF:]),
                      _pad_ff_cols(cb[:, :D_FF]), _pad_ff_cols(cb[:, D_FF:]), seq)
        xf = _ffn_down(act, w_down, xf)
    return xf.reshape(batch, seq, D_MODEL)
```

```python
import functools
import math

import jax
import jax.numpy as jnp
from jax import lax
from jax.experimental import pallas as pl
from jax.experimental.pallas import tpu as pltpu

F32 = jnp.float32
BF16 = jnp.bfloat16

D_MODEL = 2048
N_HEADS = 8
HEAD_DIM = 64
V_HEAD_DIM = 2 * HEAD_DIM
D_QK = N_HEADS * 2 * HEAD_DIM
D_V = N_HEADS * V_HEAD_DIM
D_SSM = D_MODEL // 2
SSM_GROUP = 16
N_GROUPS = D_SSM // SSM_GROUP
SSM_STATE = 64
D_FF = 5504
CONV_W = 3
CHUNK = 64
EPS = 1e-6
IN_COLS = 2 * D_QK + D_V + D_SSM + 2 * D_MODEL

MXU_DIM = 256
FF_TILE = 512
ROW_CHUNK = 256
FFN_ROW_CHUNK = 1024
D_FF_PAD = ((D_FF + FF_TILE - 1) // FF_TILE) * FF_TILE
GROUPS_PER_TILE = 8
SSM_NC = GROUPS_PER_TILE * SSM_STATE
NEG_BIG = -1e30
Q_SCALE = HEAD_DIM ** -0.5 * math.log2(math.e)
VMEM_LIMIT = 56 * 1024 * 1024


def _gelu_tanh(x):
    c = math.sqrt(2.0 / math.pi)
    return x * (0.5 * (1.0 + jnp.tanh(c * (x + 0.044715 * (x * x * x)))))


def _rms_rows(xf, g):
    ms = jnp.mean(xf * xf, axis=-1, keepdims=True)
    return xf * lax.rsqrt(ms + EPS) * g


def _in_proj_body(x_ref, g_ref, w_ref, qg_ref, kg_ref, ones_ref,
                  q_ref, k_ref, v_ref, u_ref, gate_ref, h_scr, *, tn):
    j = pl.program_id(1)
    nq = D_QK // tn

    @pl.when(j == 0)
    def _():
        h_scr[...] = _rms_rows(x_ref[...], g_ref[...]).astype(BF16)

    tm = h_scr.shape[0]
    rc = min(ROW_CHUNK, tm)

    def tile(out_ref, epilogue):
        nchunk = tm // rc
        rows = lambda q: slice(q * rc, (q + 1) * rc)
        dot_q = lambda q: jnp.dot(h_scr[rows(q), :], w_ref[...], preferred_element_type=F32)
        acc = dot_q(0)
        for q in range(1, nchunk):
            nxt = dot_q(q)
            out_ref[rows(q - 1), :] = epilogue(acc).astype(BF16)
            acc = nxt
        out_ref[rows(nchunk - 1), :] = epilogue(acc).astype(BF16)

    def head_norm(gain_ref, scale):
        def epilogue(acc):
            outs = []
            for c in range(tn // MXU_DIM):
                a = acc[:, c * MXU_DIM:(c + 1) * MXU_DIM]
                ss = jnp.dot((a * a).astype(BF16), ones_ref[...], preferred_element_type=F32)
                outs.append(a * lax.rsqrt(ss * (1.0 / HEAD_DIM) + EPS) * (gain_ref[...] * scale))
            return jnp.concatenate(outs, axis=1)
        return epilogue

    plain = lambda acc: acc

    @pl.when(j < nq)
    def _():
        tile(q_ref, head_norm(qg_ref, Q_SCALE))

    @pl.when((j >= nq) & (j < 2 * nq))
    def _():
        tile(k_ref, head_norm(kg_ref, 1.0))

    @pl.when((j >= 2 * nq) & (j < 3 * nq))
    def _():
        tile(v_ref, plain)

    @pl.when((j >= 3 * nq) & (j < 4 * nq))
    def _():
        tile(u_ref, plain)

    @pl.when(j >= 4 * nq)
    def _():
        tile(gate_ref, plain)


def _in_proj(xf, g, w_bf, qg, kg, batch, seq, *, tm=1024, tn=1024):
    t = xf.shape[0]
    tm = min(tm, seq)
    nq = D_QK // tn
    ng = 2 * D_MODEL // tn
    qg_t = jnp.tile(qg.astype(F32), MXU_DIM // HEAD_DIM).reshape(1, MXU_DIM)
    kg_t = jnp.tile(kg.astype(F32), MXU_DIM // HEAD_DIM).reshape(1, MXU_DIM)
    seg = jnp.arange(MXU_DIM) // HEAD_DIM
    ones_bd = (seg[:, None] == seg[None, :]).astype(BF16)

    def cl(j, lo, n):
        return jnp.clip(j - lo, 0, n - 1)

    const = lambda i, j: (0, 0)
    return pl.pallas_call(
        functools.partial(_in_proj_body, tn=tn),
        grid=(t // tm, IN_COLS // tn),
        in_specs=[
            pl.BlockSpec((tm, D_MODEL), lambda i, j: (i, 0)),
            pl.BlockSpec((1, D_MODEL), const),
            pl.BlockSpec((D_MODEL, tn), lambda i, j: (0, j)),
            pl.BlockSpec((1, MXU_DIM), const),
            pl.BlockSpec((1, MXU_DIM), const),
            pl.BlockSpec((MXU_DIM, MXU_DIM), const),
        ],
        out_specs=[
            pl.BlockSpec((tm, tn), lambda i, j: (i, cl(j, 0, nq))),
            pl.BlockSpec((tm, tn), lambda i, j: (i, cl(j, nq, nq))),
            pl.BlockSpec((tm, tn), lambda i, j: (i, cl(j, 2 * nq, nq))),
            pl.BlockSpec((tm, tn), lambda i, j: (i, cl(j, 3 * nq, nq))),
            pl.BlockSpec((tm, tn), lambda i, j: (i, cl(j, 4 * nq, ng))),
        ],
        out_shape=[
            jax.ShapeDtypeStruct((t, D_QK), BF16),
            jax.ShapeDtypeStruct((t, D_QK), BF16),
            jax.ShapeDtypeStruct((t, D_V), BF16),
            jax.ShapeDtypeStruct((t, D_SSM), BF16),
            jax.ShapeDtypeStruct((t, 2 * D_MODEL), BF16),
        ],
        scratch_shapes=[pltpu.VMEM((tm, D_MODEL), BF16)],
        compiler_params=pltpu.CompilerParams(
            dimension_semantics=("arbitrary", "arbitrary"),
            vmem_limit_bytes=VMEM_LIMIT),
        name="in_proj",
    )(xf, g.reshape(1, D_MODEL), w_bf, qg_t, kg_t, ones_bd)


N_BIAS_ROWS = 3
BIAS_LANE0 = 0


def _attn_body(sl_ref, laminit_ref, q_ref, k_ref, v_ref, lq1_ref, lk1_ref, lq2_ref, lk2_ref,
               subg_ref, o_ref, kaug_scr, vt_scr, dmat_scr, *, seq, tq, hpb):
    tk = tq
    nblk = seq // tk
    hd2 = 2 * HEAD_DIM
    nch = 2 * hpb
    hg = pl.program_id(1)
    lam_init = laminit_ref[0]
    lam = (jnp.exp(jnp.sum(lq1_ref[...] * lk1_ref[...], axis=-1, keepdims=True))
           - jnp.exp(jnp.sum(lq2_ref[...] * lk2_ref[...], axis=-1, keepdims=True))
           + lam_init)
    subg = subg_ref[...] * (1.0 - lam_init)

    koff = lax.broadcasted_iota(jnp.int32, (tk, hd2), 0)
    klane = lax.broadcasted_iota(jnp.int32, (tk, hd2), 1)
    bias_lo = BIAS_LANE0
    bias_hi = HEAD_DIM + BIAS_LANE0
    kbias = [jnp.where((klane >= lo) & (klane < lo + N_BIAS_ROWS), koff, 0).astype(F32).astype(BF16)
             for lo in (bias_hi, bias_lo)]
    c = lax.broadcasted_iota(jnp.int32, (tk, tq), 0)
    r = lax.broadcasted_iota(jnp.int32, (tk, tq), 1)
    visible = (c // CHUNK) <= (r // CHUNK)
    ahead = jnp.maximum(c - r, 0).astype(F32)
    brow_i = lax.broadcasted_iota(jnp.int32, (hd2, tq), 0)
    sl2 = []
    brows = []
    for hh in range(hpb):
        h = hg * hpb + hh
        s_hi, s_mid, s_lo = sl_ref[h, 0], sl_ref[h, 1], sl_ref[h, 2]
        sl2.append(s_hi + s_mid + s_lo)
        brows.append([jnp.where(brow_i == lo, s_hi, jnp.where(brow_i == lo + 1, s_mid,
                      jnp.where(brow_i == lo + 2, s_lo, 0.0))) for lo in (bias_hi, bias_lo)])
        for blk in range(nblk):
            blk_rows = slice(blk * tk, (blk + 1) * tk)
            k_blk = k_ref[blk_rows, hh * hd2:(hh + 1) * hd2]
            kaug_scr[hh, 0, blk_rows, :] = jnp.where(klane < HEAD_DIM, k_blk, kbias[0])
            kaug_scr[hh, 1, blk_rows, :] = jnp.where(klane >= HEAD_DIM, k_blk, kbias[1])
            vt_scr[hh, blk] = v_ref[blk_rows, hh * hd2:(hh + 1) * hd2].astype(F32).T.astype(BF16)
        dmat_scr[hh] = jnp.where(visible, (-2.0 * sl2[hh]) * ahead, NEG_BIG)

    def q_block(iq, _):
        q0 = pl.multiple_of(iq * tq, tq)
        qas = []
        for hh in range(hpb):
            qt = q_ref[pl.ds(q0, tq), hh * hd2:(hh + 1) * hd2].astype(F32).T
            for comp in range(2):
                keep = (brow_i < HEAD_DIM) if comp == 0 else (brow_i >= HEAD_DIM)
                qas.append(jnp.where(keep, qt, brows[hh][comp]).astype(BF16))
        ss = [jnp.dot(kaug_scr[idx // 2, idx % 2, pl.ds(q0, tk), :], qas[idx],
                      preferred_element_type=F32) + dmat_scr[idx // 2] for idx in range(nch)]
        ps, stats = [], []
        for idx in range(nch):
            m = jnp.max(ss[idx], axis=0, keepdims=True)
            p = jnp.exp2(ss[idx] - m)
            stats.append((m, jnp.sum(p, axis=0, keepdims=True)))
            ps.append(p.astype(BF16))
        state = []
        for idx in range(nch):
            acc = jnp.dot(vt_scr[idx // 2, iq], ps[idx], preferred_element_type=F32)
            state.extend([stats[idx][0], stats[idx][1], acc])

        def kv_block(j, carry):
            k0 = pl.multiple_of(j * tk, tk)
            boff = lax.convert_element_type((j - iq) * tk, F32)
            ss = [jnp.dot(kaug_scr[idx // 2, idx % 2, pl.ds(k0, tk), :], qas[idx],
                          preferred_element_type=F32) for idx in range(nch)]
            ps, stats = [], []
            for idx in range(nch):
                m, l, _ = carry[3 * idx:3 * idx + 3]
                bc = sl2[idx // 2] * boff
                m_new = jnp.maximum(m, jnp.max(ss[idx], axis=0, keepdims=True) + bc)
                alpha = jnp.exp2(m - m_new)
                p = jnp.exp2(ss[idx] - (m_new - bc))
                stats.append((m_new, alpha * l + jnp.sum(p, axis=0, keepdims=True), alpha))
                ps.append(p.astype(BF16))
            out = []
            for idx in range(nch):
                m_new, l_new, alpha = stats[idx]
                acc = alpha * carry[3 * idx + 2] + jnp.dot(vt_scr[idx // 2, j], ps[idx],
                                                           preferred_element_type=F32)
                out.extend([m_new, l_new, acc])
            return tuple(out)

        state = lax.fori_loop(0, iq, kv_block, tuple(state))
        for hh in range(hpb):
            _, l0, a0, _, l1, a1 = state[6 * hh:6 * hh + 6]
            ot = a0 * (1.0 / l0) - a1 * (lam / l1)
            ot = ot * lax.rsqrt(jnp.mean(ot * ot, axis=0, keepdims=True) + EPS)
            o_ref[pl.ds(q0, tq), hh * hd2:(hh + 1) * hd2] = (ot.T * subg).astype(BF16)
        return 0

    lax.fori_loop(0, seq // tq, q_block, 0)


def _attention(q, k, v, lq1, lk1, lq2, lk2, subg, lam_init, batch, seq, *, tq=256, hpb=4):
    slopes = 2.0 ** (-8.0 * jnp.arange(1, N_HEADS + 1, dtype=F32) / N_HEADS)
    sl = slopes * math.log2(math.e)
    s_hi = sl.astype(BF16).astype(F32)
    s_mid = (sl - s_hi).astype(BF16).astype(F32)
    s_lo = (sl - s_hi - s_mid).astype(BF16).astype(F32)
    sl3 = jnp.stack([s_hi, s_mid, s_lo], axis=1)
    lam_arr = jnp.full((1,), lam_init, F32)
    smem = pl.BlockSpec(memory_space=pltpu.SMEM)
    hw = hpb * V_HEAD_DIM
    head_blk = pl.BlockSpec((seq, hw), lambda b, h: (b, h))
    vec = lambda n: pl.BlockSpec((1, n), lambda b, h: (0, 0))
    row = lambda a: a.astype(F32).reshape(1, -1)
    return pl.pallas_call(
        functools.partial(_attn_body, seq=seq, tq=tq, hpb=hpb),
        grid=(batch, N_HEADS // hpb),
        in_specs=[smem, smem, head_blk, head_blk, head_blk,
                  vec(HEAD_DIM), vec(HEAD_DIM), vec(HEAD_DIM), vec(HEAD_DIM), vec(V_HEAD_DIM)],
        out_specs=head_blk,
        out_shape=jax.ShapeDtypeStruct((batch * seq, D_V), BF16),
        scratch_shapes=[pltpu.VMEM((hpb, 2, seq, V_HEAD_DIM), BF16),
                        pltpu.VMEM((hpb, seq // tq, V_HEAD_DIM, tq), BF16),
                        pltpu.VMEM((hpb, tq, tq), F32)],
        compiler_params=pltpu.CompilerParams(
            dimension_semantics=("parallel", "parallel"),
            vmem_limit_bytes=VMEM_LIMIT),
        name="diff_attention",
    )(sl3, lam_arr, q, k, v, row(lq1), row(lk1), row(lq2), row(lk2), row(subg))


def _ssm_body(u_ref, bblk_ref, are_ref, aim_ref, cblk_ref, d_ref, y_ref,
              x_scr, h_scr, st_scr, *, batch, tt):
    nc = st_scr.shape[-1] // 2
    gcols = u_ref.shape[-1]

    @pl.when(pl.program_id(1) == 0)
    def _():
        st_scr[...] = jnp.zeros_like(st_scr)

    u_f = pltpu.einshape("btn->tbn", u_ref[...].astype(F32)).reshape(tt * batch, gcols)
    x_scr[...] = jnp.dot(u_f.astype(BF16), bblk_ref[...], preferred_element_type=F32)
    ar = jnp.broadcast_to(are_ref[...], (batch, nc))
    ai = jnp.broadcast_to(aim_ref[...], (batch, nc))

    def step(t, carry):
        hr, hi = carry
        r0 = pl.multiple_of(t * batch, batch)
        xr = x_scr[pl.ds(r0, batch), :nc]
        xi = x_scr[pl.ds(r0, batch), nc:]
        nr = ar * hr - ai * hi + xr
        ni = ar * hi + ai * hr + xi
        h_scr[pl.ds(r0, batch), :nc] = nr.astype(BF16)
        h_scr[pl.ds(r0, batch), nc:] = ni.astype(BF16)
        return nr, ni

    hr, hi = lax.fori_loop(0, tt, step, (st_scr[:, :nc], st_scr[:, nc:]), unroll=4)
    st_scr[:, :nc] = hr
    st_scr[:, nc:] = hi
    y = jnp.dot(h_scr[...], cblk_ref[...], preferred_element_type=F32)
    y = _gelu_tanh(y + d_ref[...] * u_f)
    y_ref[...] = pltpu.einshape("tbn->btn", y.reshape(tt, batch, gcols)).astype(BF16)


def _ssm(u3, bblk, are, aim, cblk, d_skip, *, tt=128):
    batch, seq, _ = u3.shape
    gcols = GROUPS_PER_TILE * SSM_GROUP
    ngt = N_GROUPS // GROUPS_PER_TILE
    tt = min(tt, seq)
    rows = tt * batch
    return pl.pallas_call(
        functools.partial(_ssm_body, batch=batch, tt=tt),
        grid=(ngt, seq // tt),
        in_specs=[
            pl.BlockSpec((batch, tt, gcols), lambda g, t: (0, t, g)),
            pl.BlockSpec((None, gcols, 2 * SSM_NC), lambda g, t: (g, 0, 0)),
            pl.BlockSpec((None, 1, SSM_NC), lambda g, t: (g, 0, 0)),
            pl.BlockSpec((None, 1, SSM_NC), lambda g, t: (g, 0, 0)),
            pl.BlockSpec((None, 2 * SSM_NC, gcols), lambda g, t: (g, 0, 0)),
            pl.BlockSpec((1, gcols), lambda g, t: (0, g)),
        ],
        out_specs=pl.BlockSpec((batch, tt, gcols), lambda g, t: (0, t, g)),
        out_shape=jax.ShapeDtypeStruct((batch, seq, D_SSM), BF16),
        scratch_shapes=[pltpu.VMEM((rows, 2 * SSM_NC), F32),
                        pltpu.VMEM((rows, 2 * SSM_NC), BF16),
                        pltpu.VMEM((batch, 2 * SSM_NC), F32)],
        compiler_params=pltpu.CompilerParams(
            dimension_semantics=("parallel", "arbitrary"),
            vmem_limit_bytes=VMEM_LIMIT),
        name="s5_scan",
    )(u3, bblk, are, aim, cblk, d_skip.astype(F32).reshape(1, D_SSM))


def _ssm_params(a_re, a_im, log_dt, b_re, b_im, c_re, c_im):
    gpt = GROUPS_PER_TILE
    ngt = N_GROUPS // gpt
    dt = jnp.exp(log_dt)[:, None]
    mag = jnp.exp(dt * a_re)
    ab_re = mag * jnp.cos(dt * a_im)
    ab_im = mag * jnp.sin(dt * a_im)
    den = a_re * a_re + a_im * a_im
    zr = ab_re - 1.0
    zi = ab_im
    f_re = (zr * a_re + zi * a_im) / den
    f_im = (zi * a_re - zr * a_im) / den
    bb_re = f_re[..., None] * b_re - f_im[..., None] * b_im
    bb_im = f_re[..., None] * b_im + f_im[..., None] * b_re
    eye = jnp.eye(gpt, dtype=F32)

    def bdiag_in(bb):
        t = bb.reshape(ngt, gpt, SSM_STATE, SSM_GROUP)
        return jnp.einsum('tgpi,gh->tgihp', t, eye).reshape(ngt, gpt * SSM_GROUP, SSM_NC)

    def bdiag_out(cc):
        t = cc.reshape(ngt, gpt, SSM_GROUP, SSM_STATE)
        return jnp.einsum('tgop,gh->tgpho', t, eye).reshape(ngt, SSM_NC, gpt * SSM_GROUP)

    bblk = jnp.concatenate([bdiag_in(bb_re), bdiag_in(bb_im)], axis=-1).astype(BF16)
    cblk = jnp.concatenate([bdiag_out(c_re), -bdiag_out(c_im)], axis=1).astype(BF16)
    return bblk, ab_re.reshape(ngt, 1, SSM_NC), ab_im.reshape(ngt, 1, SSM_NC), cblk


def _mix_body(y_ref, oa_ref, gate_ref, x_ref, gw_ref, gb_ref, wba_ref, wbs_ref, wo_ref, out_ref):
    y = y_ref[...]
    z = jnp.dot(y, gw_ref[...], preferred_element_type=F32) + gb_ref[...]
    y2 = (y.astype(F32) * jax.nn.sigmoid(z)).astype(BF16)
    o_ssm = jnp.dot(y2, wbs_ref[...], preferred_element_type=F32)
    o_att = jnp.dot(oa_ref[...], wba_ref[...], preferred_element_type=F32)
    mixed = (jax.nn.sigmoid(gate_ref[:, :D_MODEL].astype(F32)) * o_att
             + jax.nn.sigmoid(gate_ref[:, D_MODEL:].astype(F32)) * o_ssm)
    out_ref[...] = x_ref[...] + jnp.dot(mixed.astype(BF16), wo_ref[...],
                                        preferred_element_type=F32)


def _mix(y, o_att, gates, xf, glu_w, glu_b, w_ba, w_bs, w_out, batch, seq, *, tm=256):
    t = xf.shape[0]
    tm = min(tm, seq)
    full = lambda shape: pl.BlockSpec(shape, lambda i: (0, 0), pipeline_mode=pl.Buffered(1))
    return pl.pallas_call(
        _mix_body,
        grid=(t // tm,),
        in_specs=[
            pl.BlockSpec((tm, D_SSM), lambda i: (i, 0)),
            pl.BlockSpec((tm, D_V), lambda i: (i, 0)),
            pl.BlockSpec((tm, 2 * D_MODEL), lambda i: (i, 0)),
            pl.BlockSpec((tm, D_MODEL), lambda i: (i, 0)),
            full((D_SSM, D_SSM)), full((1, D_SSM)),
            full((D_V, D_MODEL)), full((D_SSM, D_MODEL)), full((D_MODEL, D_MODEL)),
        ],
        out_specs=pl.BlockSpec((tm, D_MODEL), lambda i: (i, 0)),
        out_shape=jax.ShapeDtypeStruct((t, D_MODEL), F32),
        compiler_params=pltpu.CompilerParams(
            dimension_semantics=("parallel",),
            vmem_limit_bytes=VMEM_LIMIT),
        name="gated_merge",
    )(y, o_att, gates, xf, glu_w, glu_b.astype(F32).reshape(1, D_SSM), w_ba, w_bs, w_out)


def _shift_rows(up, prev, k):
    body = pltpu.roll(up, k, axis=0)
    top = pltpu.roll(jnp.concatenate([prev, up[:8]], axis=0), k, axis=0)[8:]
    return jnp.concatenate([top, body[8:]], axis=0)


def _ffn_up_body(x_ref, g_ref, wa_ref, wv_ref, cwa_ref, cwv_ref, cba_ref, cbv_ref,
                 act_ref, h_scr, carry_a, carry_v, *, nsb):
    i = pl.program_id(0)
    j = pl.program_id(1)

    @pl.when(j == 0)
    def _():
        h_scr[...] = _rms_rows(x_ref[...], g_ref[...]).astype(BF16)

    seq_start = i % nsb == 0
    tm = h_scr.shape[0]
    rc = min(FFN_ROW_CHUNK, tm)
    prev_a = jnp.where(seq_start, 0.0, carry_a[j])
    prev_v = jnp.where(seq_start, 0.0, carry_v[j])

    def conv(up, prev, cw_ref, cb_ref):
        cw = cw_ref[...]
        return (cb_ref[...] + cw[0:1] * _shift_rows(up, prev, 2)
                + cw[1:2] * _shift_rows(up, prev, 1) + cw[2:3] * up)

    for q in range(tm // rc):
        h = h_scr[q * rc:(q + 1) * rc, :]
        up_a = jnp.dot(h, wa_ref[...], preferred_element_type=F32)
        up_v = jnp.dot(h, wv_ref[...], preferred_element_type=F32)
        a = conv(up_a, prev_a, cwa_ref, cba_ref)
        val = conv(up_v, prev_v, cwv_ref, cbv_ref)
        act_ref[q * rc:(q + 1) * rc, :] = (_gelu_tanh(a) * val).astype(BF16)
        prev_a, prev_v = up_a[rc - 8:], up_v[rc - 8:]
    carry_a[j] = prev_a
    carry_v[j] = prev_v


def _ffn_up(xf, g, wa, wv, cwa, cwv, cba, cbv, seq, *, tm=1024, tn=FF_TILE):
    t = xf.shape[0]
    tm = min(tm, seq)
    ncol = D_FF_PAD // tn
    const = lambda i, j: (0, 0)
    col = lambda i, j: (0, j)
    return pl.pallas_call(
        functools.partial(_ffn_up_body, nsb=seq // tm),
        grid=(t // tm, ncol),
        in_specs=[
            pl.BlockSpec((tm, D_MODEL), lambda i, j: (i, 0)),
            pl.BlockSpec((1, D_MODEL), const),
            pl.BlockSpec((D_MODEL, tn), col),
            pl.BlockSpec((D_MODEL, tn), col),
            pl.BlockSpec((CONV_W, tn), col),
            pl.BlockSpec((CONV_W, tn), col),
            pl.BlockSpec((1, tn), col),
            pl.BlockSpec((1, tn), col),
        ],
        out_specs=pl.BlockSpec((tm, tn), lambda i, j: (i, j)),
        out_shape=jax.ShapeDtypeStruct((t, D_FF_PAD), BF16),
        scratch_shapes=[pltpu.VMEM((tm, D_MODEL), BF16),
                        pltpu.VMEM((ncol, 8, tn), F32),
                        pltpu.VMEM((ncol, 8, tn), F32)],
        compiler_params=pltpu.CompilerParams(
            dimension_semantics=("arbitrary", "arbitrary"),
            vmem_limit_bytes=VMEM_LIMIT),
        name="ffn_up_conv_gate",
    )(xf, g.reshape(1, D_MODEL), wa, wv, cwa, cwv, cba, cbv)


def _ffn_down_body(act_ref, w_ref, x_ref, out_ref):
    out_ref[...] = x_ref[...] + jnp.dot(act_ref[...], w_ref[...], preferred_element_type=F32)


def _ffn_down(act, w_down, xf, *, tm=1024, tn=512):
    t = xf.shape[0]
    tm = min(tm, t)
    return pl.pallas_call(
        _ffn_down_body,
        grid=(t // tm, D_MODEL // tn),
        in_specs=[
            pl.BlockSpec((tm, D_FF_PAD), lambda i, j: (i, 0)),
            pl.BlockSpec((D_FF_PAD, tn), lambda i, j: (0, j)),
            pl.BlockSpec((tm, tn), lambda i, j: (i, j)),
        ],
        out_specs=pl.BlockSpec((tm, tn), lambda i, j: (i, j)),
        out_shape=jax.ShapeDtypeStruct((t, D_MODEL), F32),
        compiler_params=pltpu.CompilerParams(
            dimension_semantics=("parallel", "parallel"),
            vmem_limit_bytes=VMEM_LIMIT),
        name="ffn_down",
    )(act, w_down, xf)


def _pad_ff_cols(a):
    return jnp.pad(a, ((0, 0), (0, D_FF_PAD - D_FF)))


def kernel(x, norm1_g, w_in, q_norm_g, k_norm_g, lambda_q1, lambda_k1, lambda_q2, lambda_k2, subln_g, ssm_a_re, ssm_a_im, ssm_log_dt, ssm_b_re, ssm_b_im, ssm_c_re, ssm_c_im, ssm_d, ssm_glu_w, ssm_glu_b, w_branch_attn, w_branch_ssm, w_out, norm2_g, ffn_w_up, ffn_conv_w, ffn_conv_b, ffn_w_down):
    batch, seq, _ = x.shape
    depth = w_in.shape[0]
    xf = x.reshape(batch * seq, D_MODEL)
    for l in range(depth):
        lam_init = 0.8 - 0.6 * math.exp(-0.3 * l)
        q, k, v, u, gates = _in_proj(xf, norm1_g[l], w_in[l].astype(BF16),
                                     q_norm_g[l], k_norm_g[l], batch, seq)
        o_att = _attention(q, k, v, lambda_q1[l], lambda_k1[l], lambda_q2[l], lambda_k2[l],
                           subln_g[l], lam_init, batch, seq)
        bblk, are, aim, cblk = _ssm_params(ssm_a_re[l], ssm_a_im[l], ssm_log_dt[l],
                                           ssm_b_re[l], ssm_b_im[l], ssm_c_re[l], ssm_c_im[l])
        y = _ssm(u.reshape(batch, seq, D_SSM), bblk, are, aim, cblk, ssm_d[l])
        xf = _mix(y.reshape(batch * seq, D_SSM), o_att, gates, xf,
                  ssm_glu_w[l].astype(BF16), ssm_glu_b[l],
                  w_branch_attn[l].astype(BF16), w_branch_ssm[l].astype(BF16),
                  w_out[l].astype(BF16), batch, seq)
        wu = ffn_w_up[l]
        cw = ffn_conv_w[l]
        cb = ffn_conv_b[l].reshape(1, 2 * D_FF)
        w_down = jnp.pad(ffn_w_down[l], ((0, D_FF_PAD - D_FF), (0, 0))).astype(BF16)
        act = _ffn_up(xf, norm2_g[l],
                      _pad_ff_cols(wu[:, :D_FF]).astype(BF16), _pad_ff_cols(wu[:, D_FF:]).astype(BF16),
                      _pad_ff_cols(cw[:, :D_FF]), _pad_ff_cols(cw[:, D_FF:]),
                      _pad_ff_cols(cb[:, :D_FF]), _pad_ff_cols(cb[:, D_FF:]), seq)
        xf = _ffn_down(act, w_down, xf)
    return xf.reshape(batch, seq, D_MODEL)
```

```python
import functools
import math

import jax
import jax.numpy as jnp
from jax import lax
from jax.experimental import pallas as pl
from jax.experimental.pallas import tpu as pltpu

F32 = jnp.float32
BF16 = jnp.bfloat16

D_MODEL = 2048
N_HEADS = 8
HEAD_DIM = 64
V_HEAD_DIM = 2 * HEAD_DIM
D_QK = N_HEADS * 2 * HEAD_DIM
D_V = N_HEADS * V_HEAD_DIM
D_SSM = D_MODEL // 2
SSM_GROUP = 16
N_GROUPS = D_SSM // SSM_GROUP
SSM_STATE = 64
D_FF = 5504
CONV_W = 3
CHUNK = 64
EPS = 1e-6
IN_COLS = 2 * D_QK + D_V + D_SSM + 2 * D_MODEL

MXU_DIM = 256
FF_TILE = 512
ROW_CHUNK = 256
D_FF_PAD = ((D_FF + FF_TILE - 1) // FF_TILE) * FF_TILE
GROUPS_PER_TILE = 8
SSM_NC = GROUPS_PER_TILE * SSM_STATE
NEG_BIG = -1e30
Q_SCALE = HEAD_DIM ** -0.5 * math.log2(math.e)
VMEM_LIMIT = 56 * 1024 * 1024


def _gelu_tanh(x):
    c = math.sqrt(2.0 / math.pi)
    return x * (0.5 * (1.0 + jnp.tanh(c * (x + 0.044715 * (x * x * x)))))


def _rms_rows(xf, g):
    ms = jnp.mean(xf * xf, axis=-1, keepdims=True)
    return xf * lax.rsqrt(ms + EPS) * g


def _in_proj_body(x_ref, g_ref, w_ref, qg_ref, kg_ref, ones_ref,
                  q_ref, k_ref, v_ref, u_ref, gate_ref, h_scr, *, tn):
    j = pl.program_id(1)
    nq = D_QK // tn

    @pl.when(j == 0)
    def _():
        h_scr[...] = _rms_rows(x_ref[...], g_ref[...]).astype(BF16)

    tm = h_scr.shape[0]
    rc = min(ROW_CHUNK, tm)

    def tile(out_ref, epilogue):
        nchunk = tm // rc
        rows = lambda q: slice(q * rc, (q + 1) * rc)
        dot_q = lambda q: jnp.dot(h_scr[rows(q), :], w_ref[...], preferred_element_type=F32)
        acc = dot_q(0)
        for q in range(1, nchunk):
            nxt = dot_q(q)
            out_ref[rows(q - 1), :] = epilogue(acc).astype(BF16)
            acc = nxt
        out_ref[rows(nchunk - 1), :] = epilogue(acc).astype(BF16)

    def head_norm(gain_ref, scale):
        def epilogue(acc):
            outs = []
            for c in range(tn // MXU_DIM):
                a = acc[:, c * MXU_DIM:(c + 1) * MXU_DIM]
                ss = jnp.dot((a * a).astype(BF16), ones_ref[...], preferred_element_type=F32)
                outs.append(a * lax.rsqrt(ss * (1.0 / HEAD_DIM) + EPS) * (gain_ref[...] * scale))
            return jnp.concatenate(outs, axis=1)
        return epilogue

    plain = lambda acc: acc

    @pl.when(j < nq)
    def _():
        tile(q_ref, head_norm(qg_ref, Q_SCALE))

    @pl.when((j >= nq) & (j < 2 * nq))
    def _():
        tile(k_ref, head_norm(kg_ref, 1.0))

    @pl.when((j >= 2 * nq) & (j < 3 * nq))
    def _():
        tile(v_ref, plain)

    @pl.when((j >= 3 * nq) & (j < 4 * nq))
    def _():
        tile(u_ref, plain)

    @pl.when(j >= 4 * nq)
    def _():
        tile(gate_ref, plain)


def _in_proj(xf, g, w_bf, qg, kg, batch, seq, *, tm=1024, tn=1024):
    t = xf.shape[0]
    tm = min(tm, seq)
    nq = D_QK // tn
    ng = 2 * D_MODEL // tn
    qg_t = jnp.tile(qg.astype(F32), MXU_DIM // HEAD_DIM).reshape(1, MXU_DIM)
    kg_t = jnp.tile(kg.astype(F32), MXU_DIM // HEAD_DIM).reshape(1, MXU_DIM)
    seg = jnp.arange(MXU_DIM) // HEAD_DIM
    ones_bd = (seg[:, None] == seg[None, :]).astype(BF16)

    def cl(j, lo, n):
        return jnp.clip(j - lo, 0, n - 1)

    const = lambda i, j: (0, 0)
    return pl.pallas_call(
        functools.partial(_in_proj_body, tn=tn),
        grid=(t // tm, IN_COLS // tn),
        in_specs=[
            pl.BlockSpec((tm, D_MODEL), lambda i, j: (i, 0)),
            pl.BlockSpec((1, D_MODEL), const),
            pl.BlockSpec((D_MODEL, tn), lambda i, j: (0, j)),
            pl.BlockSpec((1, MXU_DIM), const),
            pl.BlockSpec((1, MXU_DIM), const),
            pl.BlockSpec((MXU_DIM, MXU_DIM), const),
        ],
        out_specs=[
            pl.BlockSpec((tm, tn), lambda i, j: (i, cl(j, 0, nq))),
            pl.BlockSpec((tm, tn), lambda i, j: (i, cl(j, nq, nq))),
            pl.BlockSpec((tm, tn), lambda i, j: (i, cl(j, 2 * nq, nq))),
            pl.BlockSpec((tm, tn), lambda i, j: (i, cl(j, 3 * nq, nq))),
            pl.BlockSpec((tm, tn), lambda i, j: (i, cl(j, 4 * nq, ng))),
        ],
        out_shape=[
            jax.ShapeDtypeStruct((t, D_QK), BF16),
            jax.ShapeDtypeStruct((t, D_QK), BF16),
            jax.ShapeDtypeStruct((t, D_V), BF16),
            jax.ShapeDtypeStruct((t, D_SSM), BF16),
            jax.ShapeDtypeStruct((t, 2 * D_MODEL), BF16),
        ],
        scratch_shapes=[pltpu.VMEM((tm, D_MODEL), BF16)],
        compiler_params=pltpu.CompilerParams(
            dimension_semantics=("arbitrary", "arbitrary"),
            vmem_limit_bytes=VMEM_LIMIT),
        name="in_proj",
    )(xf, g.reshape(1, D_MODEL), w_bf, qg_t, kg_t, ones_bd)


N_BIAS_ROWS = 3
OFFSET_RADIX = 256
BIAS_LANE0 = 0


def _attn_body(sl_ref, laminit_ref, q_ref, k_ref, v_ref, lq1_ref, lk1_ref, lq2_ref, lk2_ref,
               subg_ref, o_ref, kaug_scr, vt_scr, dmat_scr, *, seq, tq, hpb):
    tk = tq
    nblk = seq // tk
    hd2 = 2 * HEAD_DIM
    nch = 2 * hpb
    hg = pl.program_id(1)
    lam_init = laminit_ref[0]
    lam = (jnp.exp(jnp.sum(lq1_ref[...] * lk1_ref[...], axis=-1, keepdims=True))
           - jnp.exp(jnp.sum(lq2_ref[...] * lk2_ref[...], axis=-1, keepdims=True))
           + lam_init)
    subg = subg_ref[...] * (1.0 - lam_init)

    koff = lax.broadcasted_iota(jnp.int32, (tk, hd2), 0)
    klane = lax.broadcasted_iota(jnp.int32, (tk, hd2), 1)
    bias_lo = BIAS_LANE0
    bias_hi = HEAD_DIM + BIAS_LANE0
    kbias = [jnp.where((klane >= lo) & (klane < lo + N_BIAS_ROWS), koff % OFFSET_RADIX,
                       jnp.where((klane >= lo + N_BIAS_ROWS) & (klane < lo + 2 * N_BIAS_ROWS),
                                 koff // OFFSET_RADIX, 0)).astype(F32).astype(BF16)
             for lo in (bias_hi, bias_lo)]
    c = lax.broadcasted_iota(jnp.int32, (tk, tq), 0)
    r = lax.broadcasted_iota(jnp.int32, (tk, tq), 1)
    visible = (c // CHUNK) <= (r // CHUNK)
    ahead = jnp.maximum(c - r, 0).astype(F32)
    brow_i = lax.broadcasted_iota(jnp.int32, (hd2, tq), 0)
    sl2 = []
    brows = []
    for hh in range(hpb):
        h = hg * hpb + hh
        s_hi, s_mid, s_lo = sl_ref[h, 0], sl_ref[h, 1], sl_ref[h, 2]
        sl2.append(s_hi + s_mid + s_lo)
        radix = float(OFFSET_RADIX)
        pieces = (s_hi, s_mid, s_lo, radix * s_hi, radix * s_mid, radix * s_lo)

        def bias_rows(lo):
            rows = jnp.zeros((hd2, tq), F32)
            for n, piece in enumerate(pieces):
                rows = jnp.where(brow_i == lo + n, piece, rows)
            return rows

        brows.append([bias_rows(bias_hi), bias_rows(bias_lo)])
        for blk in range(nblk):
            blk_rows = slice(blk * tk, (blk + 1) * tk)
            k_blk = k_ref[blk_rows, hh * hd2:(hh + 1) * hd2]
            kaug_scr[hh, 0, blk_rows, :] = jnp.where(klane < HEAD_DIM, k_blk, kbias[0])
            kaug_scr[hh, 1, blk_rows, :] = jnp.where(klane >= HEAD_DIM, k_blk, kbias[1])
            vt_scr[hh, blk] = v_ref[blk_rows, hh * hd2:(hh + 1) * hd2].astype(F32).T.astype(BF16)
        dmat_scr[hh] = jnp.where(visible, (-2.0 * sl2[hh]) * ahead, NEG_BIG)

    def q_block(iq, _):
        q0 = pl.multiple_of(iq * tq, tq)
        qas = []
        for hh in range(hpb):
            qt = q_ref[pl.ds(q0, tq), hh * hd2:(hh + 1) * hd2].astype(F32).T
            for comp in range(2):
                keep = (brow_i < HEAD_DIM) if comp == 0 else (brow_i >= HEAD_DIM)
                qas.append(jnp.where(keep, qt, brows[hh][comp]).astype(BF16))
        ss = [jnp.dot(kaug_scr[idx // 2, idx % 2, pl.ds(q0, tk), :], qas[idx],
                      preferred_element_type=F32) + dmat_scr[idx // 2] for idx in range(nch)]
        ps, stats = [], []
        for idx in range(nch):
            m = jnp.max(ss[idx], axis=0, keepdims=True)
            p = jnp.exp2(ss[idx] - m)
            stats.append((m, jnp.sum(p, axis=0, keepdims=True)))
            ps.append(p.astype(BF16))
        state = []
        for idx in range(nch):
            acc = jnp.dot(vt_scr[idx // 2, iq], ps[idx], preferred_element_type=F32)
            state.extend([stats[idx][0], stats[idx][1], acc])

        def kv_block(j, carry):
            k0 = pl.multiple_of(j * tk, tk)
            boff = lax.convert_element_type((j - iq) * tk, F32)
            ss = [jnp.dot(kaug_scr[idx // 2, idx % 2, pl.ds(k0, tk), :], qas[idx],
                          preferred_element_type=F32) for idx in range(nch)]
            ps, stats = [], []
            for idx in range(nch):
                m, l, _ = carry[3 * idx:3 * idx + 3]
                bc = sl2[idx // 2] * boff
                m_new = jnp.maximum(m, jnp.max(ss[idx], axis=0, keepdims=True) + bc)
                alpha = jnp.exp2(m - m_new)
                p = jnp.exp2(ss[idx] - (m_new - bc))
                stats.append((m_new, alpha * l + jnp.sum(p, axis=0, keepdims=True), alpha))
                ps.append(p.astype(BF16))
            out = []
            for idx in range(nch):
                m_new, l_new, alpha = stats[idx]
                acc = alpha * carry[3 * idx + 2] + jnp.dot(vt_scr[idx // 2, j], ps[idx],
                                                           preferred_element_type=F32)
                out.extend([m_new, l_new, acc])
            return tuple(out)

        state = lax.fori_loop(0, iq, kv_block, tuple(state))
        for hh in range(hpb):
            _, l0, a0, _, l1, a1 = state[6 * hh:6 * hh + 6]
            ot = a0 * (1.0 / l0) - a1 * (lam / l1)
            ot = ot * lax.rsqrt(jnp.mean(ot * ot, axis=0, keepdims=True) + EPS)
            o_ref[pl.ds(q0, tq), hh * hd2:(hh + 1) * hd2] = (ot.T * subg).astype(BF16)
        return 0

    lax.fori_loop(0, seq // tq, q_block, 0)


def _attention(q, k, v, lq1, lk1, lq2, lk2, subg, lam_init, batch, seq, *, tq=512, hpb=4):
    slopes = 2.0 ** (-8.0 * jnp.arange(1, N_HEADS + 1, dtype=F32) / N_HEADS)
    sl = slopes * math.log2(math.e)
    s_hi = sl.astype(BF16).astype(F32)
    s_mid = (sl - s_hi).astype(BF16).astype(F32)
    s_lo = (sl - s_hi - s_mid).astype(BF16).astype(F32)
    sl3 = jnp.stack([s_hi, s_mid, s_lo], axis=1)
    lam_arr = jnp.full((1,), lam_init, F32)
    smem = pl.BlockSpec(memory_space=pltpu.SMEM)
    hw = hpb * V_HEAD_DIM
    head_blk = pl.BlockSpec((seq, hw), lambda b, h: (b, h))
    vec = lambda n: pl.BlockSpec((1, n), lambda b, h: (0, 0))
    row = lambda a: a.astype(F32).reshape(1, -1)
    return pl.pallas_call(
        functools.partial(_attn_body, seq=seq, tq=tq, hpb=hpb),
        grid=(batch, N_HEADS // hpb),
        in_specs=[smem, smem, head_blk, head_blk, head_blk,
                  vec(HEAD_DIM), vec(HEAD_DIM), vec(HEAD_DIM), vec(HEAD_DIM), vec(V_HEAD_DIM)],
        out_specs=head_blk,
        out_shape=jax.ShapeDtypeStruct((batch * seq, D_V), BF16),
        scratch_shapes=[pltpu.VMEM((hpb, 2, seq, V_HEAD_DIM), BF16),
                        pltpu.VMEM((hpb, seq // tq, V_HEAD_DIM, tq), BF16),
                        pltpu.VMEM((hpb, tq, tq), F32)],
        compiler_params=pltpu.CompilerParams(
            dimension_semantics=("parallel", "parallel"),
            vmem_limit_bytes=VMEM_LIMIT),
        name="diff_attention",
    )(sl3, lam_arr, q, k, v, row(lq1), row(lk1), row(lq2), row(lk2), row(subg))


def _ssm_body(u_ref, bblk_ref, are_ref, aim_ref, cblk_ref, d_ref, y_ref,
              x_scr, h_scr, st_scr, *, batch, tt):
    nc = st_scr.shape[-1] // 2
    gcols = u_ref.shape[-1]

    @pl.when(pl.program_id(1) == 0)
    def _():
        st_scr[...] = jnp.zeros_like(st_scr)

    u_f = pltpu.einshape("btn->tbn", u_ref[...].astype(F32)).reshape(tt * batch, gcols)
    x_scr[...] = jnp.dot(u_f.astype(BF16), bblk_ref[...], preferred_element_type=F32)
    ar = jnp.broadcast_to(are_ref[...], (batch, nc))
    ai = jnp.broadcast_to(aim_ref[...], (batch, nc))

    def step(t, carry):
        hr, hi = carry
        r0 = pl.multiple_of(t * batch, batch)
        xr = x_scr[pl.ds(r0, batch), :nc]
        xi = x_scr[pl.ds(r0, batch), nc:]
        nr = ar * hr - ai * hi + xr
        ni = ar * hi + ai * hr + xi
        h_scr[pl.ds(r0, batch), :nc] = nr.astype(BF16)
        h_scr[pl.ds(r0, batch), nc:] = ni.astype(BF16)
        return nr, ni

    hr, hi = lax.fori_loop(0, tt, step, (st_scr[:, :nc], st_scr[:, nc:]), unroll=4)
    st_scr[:, :nc] = hr
    st_scr[:, nc:] = hi
    y = jnp.dot(h_scr[...], cblk_ref[...], preferred_element_type=F32)
    y = _gelu_tanh(y + d_ref[...] * u_f)
    y_ref[...] = pltpu.einshape("tbn->btn", y.reshape(tt, batch, gcols)).astype(BF16)


def _ssm(u3, bblk, are, aim, cblk, d_skip, *, tt=128):
    batch, seq, _ = u3.shape
    gcols = GROUPS_PER_TILE * SSM_GROUP
    ngt = N_GROUPS // GROUPS_PER_TILE
    tt = min(tt, seq)
    rows = tt * batch
    return pl.pallas_call(
        functools.partial(_ssm_body, batch=batch, tt=tt),
        grid=(ngt, seq // tt),
        in_specs=[
            pl.BlockSpec((batch, tt, gcols), lambda g, t: (0, t, g)),
            pl.BlockSpec((None, gcols, 2 * SSM_NC), lambda g, t: (g, 0, 0)),
            pl.BlockSpec((None, 1, SSM_NC), lambda g, t: (g, 0, 0)),
            pl.BlockSpec((None, 1, SSM_NC), lambda g, t: (g, 0, 0)),
            pl.BlockSpec((None, 2 * SSM_NC, gcols), lambda g, t: (g, 0, 0)),
            pl.BlockSpec((1, gcols), lambda g, t: (0, g)),
        ],
        out_specs=pl.BlockSpec((batch, tt, gcols), lambda g, t: (0, t, g)),
        out_shape=jax.ShapeDtypeStruct((batch, seq, D_SSM), BF16),
        scratch_shapes=[pltpu.VMEM((rows, 2 * SSM_NC), F32),
                        pltpu.VMEM((rows, 2 * SSM_NC), BF16),
                        pltpu.VMEM((batch, 2 * SSM_NC), F32)],
        compiler_params=pltpu.CompilerParams(
            dimension_semantics=("parallel", "arbitrary"),
            vmem_limit_bytes=VMEM_LIMIT),
        name="s5_scan",
    )(u3, bblk, are, aim, cblk, d_skip.astype(F32).reshape(1, D_SSM))


def _ssm_params(a_re, a_im, log_dt, b_re, b_im, c_re, c_im):
    gpt = GROUPS_PER_TILE
    ngt = N_GROUPS // gpt
    dt = jnp.exp(log_dt)[:, None]
    mag = jnp.exp(dt * a_re)
    ab_re = mag * jnp.cos(dt * a_im)
    ab_im = mag * jnp.sin(dt * a_im)
    den = a_re * a_re + a_im * a_im
    zr = ab_re - 1.0
    zi = ab_im
    f_re = (zr * a_re + zi * a_im) / den
    f_im = (zi * a_re - zr * a_im) / den
    bb_re = f_re[..., None] * b_re - f_im[..., None] * b_im
    bb_im = f_re[..., None] * b_im + f_im[..., None] * b_re
    eye = jnp.eye(gpt, dtype=F32)

    def bdiag_in(bb):
        t = bb.reshape(ngt, gpt, SSM_STATE, SSM_GROUP)
        return jnp.einsum('tgpi,gh->tgihp', t, eye).reshape(ngt, gpt * SSM_GROUP, SSM_NC)

    def bdiag_out(cc):
        t = cc.reshape(ngt, gpt, SSM_GROUP, SSM_STATE)
        return jnp.einsum('tgop,gh->tgpho', t, eye).reshape(ngt, SSM_NC, gpt * SSM_GROUP)

    bblk = jnp.concatenate([bdiag_in(bb_re), bdiag_in(bb_im)], axis=-1).astype(BF16)
    cblk = jnp.concatenate([bdiag_out(c_re), -bdiag_out(c_im)], axis=1).astype(BF16)
    return bblk, ab_re.reshape(ngt, 1, SSM_NC), ab_im.reshape(ngt, 1, SSM_NC), cblk


def _mix_body(y_ref, oa_ref, gate_ref, x_ref, gw_ref, gb_ref, wba_ref, wbs_ref, wo_ref, out_ref):
    y = y_ref[...]
    z = jnp.dot(y, gw_ref[...], preferred_element_type=F32) + gb_ref[...]
    y2 = (y.astype(F32) * jax.nn.sigmoid(z)).astype(BF16)
    o_ssm = jnp.dot(y2, wbs_ref[...], preferred_element_type=F32)
    o_att = jnp.dot(oa_ref[...], wba_ref[...], preferred_element_type=F32)
    mixed = (jax.nn.sigmoid(gate_ref[:, :D_MODEL].astype(F32)) * o_att
             + jax.nn.sigmoid(gate_ref[:, D_MODEL:].astype(F32)) * o_ssm)
    out_ref[...] = x_ref[...] + jnp.dot(mixed.astype(BF16), wo_ref[...],
                                        preferred_element_type=F32)


def _mix(y, o_att, gates, xf, glu_w, glu_b, w_ba, w_bs, w_out, batch, seq, *, tm=256):
    t = xf.shape[0]
    tm = min(tm, seq)
    full = lambda shape: pl.BlockSpec(shape, lambda i: (0, 0), pipeline_mode=pl.Buffered(1))
    return pl.pallas_call(
        _mix_body,
        grid=(t // tm,),
        in_specs=[
            pl.BlockSpec((tm, D_SSM), lambda i: (i, 0)),
            pl.BlockSpec((tm, D_V), lambda i: (i, 0)),
            pl.BlockSpec((tm, 2 * D_MODEL), lambda i: (i, 0)),
            pl.BlockSpec((tm, D_MODEL), lambda i: (i, 0)),
            full((D_SSM, D_SSM)), full((1, D_SSM)),
            full((D_V, D_MODEL)), full((D_SSM, D_MODEL)), full((D_MODEL, D_MODEL)),
        ],
        out_specs=pl.BlockSpec((tm, D_MODEL), lambda i: (i, 0)),
        out_shape=jax.ShapeDtypeStruct((t, D_MODEL), F32),
        compiler_params=pltpu.CompilerParams(
            dimension_semantics=("parallel",),
            vmem_limit_bytes=VMEM_LIMIT),
        name="gated_merge",
    )(y, o_att, gates, xf, glu_w, glu_b.astype(F32).reshape(1, D_SSM), w_ba, w_bs, w_out)


def _shift_rows(up, prev, k):
    body = pltpu.roll(up, k, axis=0)
    top = pltpu.roll(jnp.concatenate([prev, up[:8]], axis=0), k, axis=0)[8:]
    return jnp.concatenate([top, body[8:]], axis=0)


def _ffn_up_body(x_ref, g_ref, w_ref, cwa_ref, cwv_ref, cba_ref, cbv_ref,
                 act_ref, h_scr, carry_a, carry_v, *, nsb):
    i = pl.program_id(0)
    j = pl.program_id(1)
    tn = act_ref.shape[1]

    @pl.when(j == 0)
    def _():
        h_scr[...] = _rms_rows(x_ref[...], g_ref[...]).astype(BF16)

    seq_start = i % nsb == 0
    up = jnp.dot(h_scr[...], w_ref[...], preferred_element_type=F32)

    def conv(up_half, cw_ref, cb_ref, carry):
        prev = jnp.where(seq_start, 0.0, carry[j])
        carry[j] = up_half[up_half.shape[0] - 8:]
        cw = cw_ref[...]
        return (cb_ref[...] + cw[0:1] * _shift_rows(up_half, prev, 2)
                + cw[1:2] * _shift_rows(up_half, prev, 1) + cw[2:3] * up_half)

    a = conv(up[:, :tn], cwa_ref, cba_ref, carry_a)
    val = conv(up[:, tn:], cwv_ref, cbv_ref, carry_v)
    act_ref[...] = (_gelu_tanh(a) * val).astype(BF16)


def _ffn_up(xf, g, w_cat, cwa, cwv, cba, cbv, seq, *, tm=1024, tn=FF_TILE):
    t = xf.shape[0]
    tm = min(tm, seq)
    ncol = D_FF_PAD // tn
    const = lambda i, j: (0, 0)
    col = lambda i, j: (0, j)
    return pl.pallas_call(
        functools.partial(_ffn_up_body, nsb=seq // tm),
        grid=(t // tm, ncol),
        in_specs=[
            pl.BlockSpec((tm, D_MODEL), lambda i, j: (i, 0)),
            pl.BlockSpec((1, D_MODEL), const),
            pl.BlockSpec((D_MODEL, 2 * tn), col),
            pl.BlockSpec((CONV_W, tn), col),
            pl.BlockSpec((CONV_W, tn), col),
            pl.BlockSpec((1, tn), col),
            pl.BlockSpec((1, tn), col),
        ],
        out_specs=pl.BlockSpec((tm, tn), lambda i, j: (i, j)),
        out_shape=jax.ShapeDtypeStruct((t, D_FF_PAD), BF16),
        scratch_shapes=[pltpu.VMEM((tm, D_MODEL), BF16),
                        pltpu.VMEM((ncol, 8, tn), F32),
                        pltpu.VMEM((ncol, 8, tn), F32)],
        compiler_params=pltpu.CompilerParams(
            dimension_semantics=("arbitrary", "arbitrary"),
            vmem_limit_bytes=VMEM_LIMIT),
        name="ffn_up_conv_gate",
    )(xf, g.reshape(1, D_MODEL), w_cat, cwa, cwv, cba, cbv)


def _ffn_up_weights(w_up):
    ncol = D_FF_PAD // FF_TILE
    halves = [_pad_ff_cols(w_up[:, :D_FF]), _pad_ff_cols(w_up[:, D_FF:])]
    tiles = [h.astype(BF16).reshape(D_MODEL, ncol, FF_TILE) for h in halves]
    return jnp.concatenate(tiles, axis=2).reshape(D_MODEL, 2 * D_FF_PAD)


def _ffn_down_body(act_ref, w_ref, x_ref, out_ref):
    out_ref[...] = x_ref[...] + jnp.dot(act_ref[...], w_ref[...], preferred_element_type=F32)


def _ffn_down(act, w_down, xf, *, tm=1024, tn=512):
    t = xf.shape[0]
    tm = min(tm, t)
    return pl.pallas_call(
        _ffn_down_body,
        grid=(t // tm, D_MODEL // tn),
        in_specs=[
            pl.BlockSpec((tm, D_FF_PAD), lambda i, j: (i, 0)),
            pl.BlockSpec((D_FF_PAD, tn), lambda i, j: (0, j)),
            pl.BlockSpec((tm, tn), lambda i, j: (i, j)),
        ],
        out_specs=pl.BlockSpec((tm, tn), lambda i, j: (i, j)),
        out_shape=jax.ShapeDtypeStruct((t, D_MODEL), F32),
        compiler_params=pltpu.CompilerParams(
            dimension_semantics=("parallel", "parallel"),
            vmem_limit_bytes=VMEM_LIMIT),
        name="ffn_down",
    )(act, w_down, xf)


def _pad_ff_cols(a):
    return jnp.pad(a, ((0, 0), (0, D_FF_PAD - D_FF)))


def kernel(x, norm1_g, w_in, q_norm_g, k_norm_g, lambda_q1, lambda_k1, lambda_q2, lambda_k2, subln_g, ssm_a_re, ssm_a_im, ssm_log_dt, ssm_b_re, ssm_b_im, ssm_c_re, ssm_c_im, ssm_d, ssm_glu_w, ssm_glu_b, w_branch_attn, w_branch_ssm, w_out, norm2_g, ffn_w_up, ffn_conv_w, ffn_conv_b, ffn_w_down):
    batch, seq, _ = x.shape
    depth = w_in.shape[0]
    xf = x.reshape(batch * seq, D_MODEL)
    for l in range(depth):
        lam_init = 0.8 - 0.6 * math.exp(-0.3 * l)
        q, k, v, u, gates = _in_proj(xf, norm1_g[l], w_in[l].astype(BF16),
                                     q_norm_g[l], k_norm_g[l], batch, seq)
        o_att = _attention(q, k, v, lambda_q1[l], lambda_k1[l], lambda_q2[l], lambda_k2[l],
                           subln_g[l], lam_init, batch, seq)
        bblk, are, aim, cblk = _ssm_params(ssm_a_re[l], ssm_a_im[l], ssm_log_dt[l],
                                           ssm_b_re[l], ssm_b_im[l], ssm_c_re[l], ssm_c_im[l])
        y = _ssm(u.reshape(batch, seq, D_SSM), bblk, are, aim, cblk, ssm_d[l])
        xf = _mix(y.reshape(batch * seq, D_SSM), o_att, gates, xf,
                  ssm_glu_w[l].astype(BF16), ssm_glu_b[l],
                  w_branch_attn[l].astype(BF16), w_branch_ssm[l].astype(BF16),
                  w_out[l].astype(BF16), batch, seq)
        wu = ffn_w_up[l]
        cw = ffn_conv_w[l]
        cb = ffn_conv_b[l].reshape(1, 2 * D_FF)
        w_down = jnp.pad(ffn_w_down[l], ((0, D_FF_PAD - D_FF), (0, 0))).astype(BF16)
        act = _ffn_up(xf, norm2_g[l], _ffn_up_weights(wu),
                      _pad_ff_cols(cw[:, :D_FF]), _pad_ff_cols(cw[:, D_FF:]),
                      _pad_ff_cols(cb[:, :D_FF]), _pad_ff_cols(cb[:, D_FF:]), seq)
        xf = _ffn_down(act, w_down, xf)
    return xf.reshape(batch, seq, D_MODEL)
```

```python
import functools
import math

import jax
import jax.numpy as jnp
from jax import lax
from jax.experimental import pallas as pl
from jax.experimental.pallas import tpu as pltpu

F32 = jnp.float32
BF16 = jnp.bfloat16

D_MODEL = 2048
N_HEADS = 8
HEAD_DIM = 64
V_HEAD_DIM = 2 * HEAD_DIM
D_QK = N_HEADS * 2 * HEAD_DIM
D_V = N_HEADS * V_HEAD_DIM
D_SSM = D_MODEL // 2
SSM_GROUP = 16
N_GROUPS = D_SSM // SSM_GROUP
SSM_STATE = 64
D_FF = 5504
CONV_W = 3
CHUNK = 64
EPS = 1e-6
IN_COLS = 2 * D_QK + D_V + D_SSM + 2 * D_MODEL

MXU_DIM = 256
FF_TILE = 512
ROW_CHUNK = 256
D_FF_PAD = ((D_FF + FF_TILE - 1) // FF_TILE) * FF_TILE
GROUPS_PER_TILE = 8
SSM_NC = GROUPS_PER_TILE * SSM_STATE
NEG_BIG = -1e30
Q_SCALE = HEAD_DIM ** -0.5 * math.log2(math.e)
VMEM_LIMIT = 56 * 1024 * 1024


def _gelu_tanh(x):
    c = math.sqrt(2.0 / math.pi)
    return x * (0.5 * (1.0 + jnp.tanh(c * (x + 0.044715 * (x * x * x)))))


def _rms_rows(xf, g):
    ms = jnp.mean(xf * xf, axis=-1, keepdims=True)
    return xf * lax.rsqrt(ms + EPS) * g


def _in_proj_body(x_ref, g_ref, w_ref, qg_ref, kg_ref, ones_ref,
                  q_ref, k_ref, v_ref, u_ref, gate_ref, h_scr, *, tn):
    j = pl.program_id(1)
    nq = D_QK // tn

    @pl.when(j == 0)
    def _():
        h_scr[...] = _rms_rows(x_ref[...], g_ref[...]).astype(BF16)

    tm = h_scr.shape[0]
    rc = min(ROW_CHUNK, tm)

    def tile(out_ref, epilogue):
        nchunk = tm // rc
        rows = lambda q: slice(q * rc, (q + 1) * rc)
        dot_q = lambda q: jnp.dot(h_scr[rows(q), :], w_ref[...], preferred_element_type=F32)
        acc = dot_q(0)
        for q in range(1, nchunk):
            nxt = dot_q(q)
            out_ref[rows(q - 1), :] = epilogue(acc).astype(BF16)
            acc = nxt
        out_ref[rows(nchunk - 1), :] = epilogue(acc).astype(BF16)

    def head_norm(gain_ref, scale):
        def epilogue(acc):
            outs = []
            for c in range(tn // MXU_DIM):
                a = acc[:, c * MXU_DIM:(c + 1) * MXU_DIM]
                ss = jnp.dot((a * a).astype(BF16), ones_ref[...], preferred_element_type=F32)
                outs.append(a * lax.rsqrt(ss * (1.0 / HEAD_DIM) + EPS) * (gain_ref[...] * scale))
            return jnp.concatenate(outs, axis=1)
        return epilogue

    plain = lambda acc: acc

    @pl.when(j < nq)
    def _():
        tile(q_ref, head_norm(qg_ref, Q_SCALE))

    @pl.when((j >= nq) & (j < 2 * nq))
    def _():
        tile(k_ref, head_norm(kg_ref, 1.0))

    @pl.when((j >= 2 * nq) & (j < 3 * nq))
    def _():
        tile(v_ref, plain)

    @pl.when((j >= 3 * nq) & (j < 4 * nq))
    def _():
        tile(u_ref, plain)

    @pl.when(j >= 4 * nq)
    def _():
        tile(gate_ref, plain)


def _in_proj(xf, g, w_bf, qg, kg, batch, seq, *, tm=1024, tn=1024):
    t = xf.shape[0]
    tm = min(tm, seq)
    nq = D_QK // tn
    ng = 2 * D_MODEL // tn
    qg_t = jnp.tile(qg.astype(F32), MXU_DIM // HEAD_DIM).reshape(1, MXU_DIM)
    kg_t = jnp.tile(kg.astype(F32), MXU_DIM // HEAD_DIM).reshape(1, MXU_DIM)
    seg = jnp.arange(MXU_DIM) // HEAD_DIM
    ones_bd = (seg[:, None] == seg[None, :]).astype(BF16)

    def cl(j, lo, n):
        return jnp.clip(j - lo, 0, n - 1)

    const = lambda i, j: (0, 0)
    return pl.pallas_call(
        functools.partial(_in_proj_body, tn=tn),
        grid=(t // tm, IN_COLS // tn),
        in_specs=[
            pl.BlockSpec((tm, D_MODEL), lambda i, j: (i, 0)),
            pl.BlockSpec((1, D_MODEL), const),
            pl.BlockSpec((D_MODEL, tn), lambda i, j: (0, j)),
            pl.BlockSpec((1, MXU_DIM), const),
            pl.BlockSpec((1, MXU_DIM), const),
            pl.BlockSpec((MXU_DIM, MXU_DIM), const),
        ],
        out_specs=[
            pl.BlockSpec((tm, tn), lambda i, j: (i, cl(j, 0, nq))),
            pl.BlockSpec((tm, tn), lambda i, j: (i, cl(j, nq, nq))),
            pl.BlockSpec((tm, tn), lambda i, j: (i, cl(j, 2 * nq, nq))),
            pl.BlockSpec((tm, tn), lambda i, j: (i, cl(j, 3 * nq, nq))),
            pl.BlockSpec((tm, tn), lambda i, j: (i, cl(j, 4 * nq, ng))),
        ],
        out_shape=[
            jax.ShapeDtypeStruct((t, D_QK), BF16),
            jax.ShapeDtypeStruct((t, D_QK), BF16),
            jax.ShapeDtypeStruct((t, D_V), BF16),
            jax.ShapeDtypeStruct((t, D_SSM), BF16),
            jax.ShapeDtypeStruct((t, 2 * D_MODEL), BF16),
        ],
        scratch_shapes=[pltpu.VMEM((tm, D_MODEL), BF16)],
        compiler_params=pltpu.CompilerParams(
            dimension_semantics=("arbitrary", "arbitrary"),
            vmem_limit_bytes=VMEM_LIMIT),
        name="in_proj",
    )(xf, g.reshape(1, D_MODEL), w_bf, qg_t, kg_t, ones_bd)


N_BIAS_ROWS = 3
OFFSET_RADIX = 256
BIAS_LANE0 = 0


def _attn_body(sl_ref, laminit_ref, q_ref, k_ref, v_ref, lq1_ref, lk1_ref, lq2_ref, lk2_ref,
               subg_ref, o_ref, kaug_scr, vt_scr, dmat_scr, *, seq, tq, hpb):
    tk = tq
    nblk = seq // tk
    hd2 = 2 * HEAD_DIM
    nch = 2 * hpb
    hg = pl.program_id(1)
    lam_init = laminit_ref[0]
    lam = (jnp.exp(jnp.sum(lq1_ref[...] * lk1_ref[...], axis=-1, keepdims=True))
           - jnp.exp(jnp.sum(lq2_ref[...] * lk2_ref[...], axis=-1, keepdims=True))
           + lam_init)
    subg = subg_ref[...] * (1.0 - lam_init)

    koff = lax.broadcasted_iota(jnp.int32, (tk, hd2), 0)
    klane = lax.broadcasted_iota(jnp.int32, (tk, hd2), 1)
    bias_lo = BIAS_LANE0
    bias_hi = HEAD_DIM + BIAS_LANE0
    kbias = [jnp.where((klane >= lo) & (klane < lo + N_BIAS_ROWS), koff % OFFSET_RADIX,
                       jnp.where((klane >= lo + N_BIAS_ROWS) & (klane < lo + 2 * N_BIAS_ROWS),
                                 koff // OFFSET_RADIX, 0)).astype(F32).astype(BF16)
             for lo in (bias_hi, bias_lo)]
    c = lax.broadcasted_iota(jnp.int32, (tk, tq), 0)
    r = lax.broadcasted_iota(jnp.int32, (tk, tq), 1)
    visible = (c // CHUNK) <= (r // CHUNK)
    ahead = jnp.maximum(c - r, 0).astype(F32)
    brow_i = lax.broadcasted_iota(jnp.int32, (hd2, tq), 0)
    sl2 = []
    brows = []
    for hh in range(hpb):
        h = hg * hpb + hh
        s_hi, s_mid, s_lo = sl_ref[h, 0], sl_ref[h, 1], sl_ref[h, 2]
        sl2.append(s_hi + s_mid + s_lo)
        radix = float(OFFSET_RADIX)
        pieces = (s_hi, s_mid, s_lo, radix * s_hi, radix * s_mid, radix * s_lo)

        def bias_rows(lo):
            rows = jnp.zeros((hd2, tq), F32)
            for n, piece in enumerate(pieces):
                rows = jnp.where(brow_i == lo + n, piece, rows)
            return rows

        brows.append([bias_rows(bias_hi), bias_rows(bias_lo)])
        for blk in range(nblk):
            blk_rows = slice(blk * tk, (blk + 1) * tk)
            k_blk = k_ref[blk_rows, hh * hd2:(hh + 1) * hd2]
            kaug_scr[hh, 0, blk_rows, :] = jnp.where(klane < HEAD_DIM, k_blk, kbias[0])
            kaug_scr[hh, 1, blk_rows, :] = jnp.where(klane >= HEAD_DIM, k_blk, kbias[1])
            vt_scr[hh, blk] = v_ref[blk_rows, hh * hd2:(hh + 1) * hd2].astype(F32).T.astype(BF16)
        dmat_scr[hh] = jnp.where(visible, (-2.0 * sl2[hh]) * ahead, NEG_BIG)

    def q_block(iq, _):
        q0 = pl.multiple_of(iq * tq, tq)
        qas = []
        for hh in range(hpb):
            qt = q_ref[pl.ds(q0, tq), hh * hd2:(hh + 1) * hd2].astype(F32).T
            for comp in range(2):
                keep = (brow_i < HEAD_DIM) if comp == 0 else (brow_i >= HEAD_DIM)
                qas.append(jnp.where(keep, qt, brows[hh][comp]).astype(BF16))
        ss = [jnp.dot(kaug_scr[idx // 2, idx % 2, pl.ds(q0, tk), :], qas[idx],
                      preferred_element_type=F32) + dmat_scr[idx // 2] for idx in range(nch)]
        ps, stats = [], []
        for idx in range(nch):
            m = jnp.max(ss[idx], axis=0, keepdims=True)
            p = jnp.exp2(ss[idx] - m)
            stats.append((m, jnp.sum(p, axis=0, keepdims=True)))
            ps.append(p.astype(BF16))
        state = []
        for idx in range(nch):
            acc = jnp.dot(vt_scr[idx // 2, iq], ps[idx], preferred_element_type=F32)
            state.extend([stats[idx][0], stats[idx][1], acc])

        def kv_block(j, carry):
            k0 = pl.multiple_of(j * tk, tk)
            boff = lax.convert_element_type((j - iq) * tk, F32)
            ss = [jnp.dot(kaug_scr[idx // 2, idx % 2, pl.ds(k0, tk), :], qas[idx],
                          preferred_element_type=F32) for idx in range(nch)]
            ps, stats = [], []
            for idx in range(nch):
                m, l, _ = carry[3 * idx:3 * idx + 3]
                bc = sl2[idx // 2] * boff
                m_new = jnp.maximum(m, jnp.max(ss[idx], axis=0, keepdims=True) + bc)
                alpha = jnp.exp2(m - m_new)
                p = jnp.exp2(ss[idx] - (m_new - bc))
                stats.append((m_new, alpha * l + jnp.sum(p, axis=0, keepdims=True), alpha))
                ps.append(p.astype(BF16))
            out = []
            for idx in range(nch):
                m_new, l_new, alpha = stats[idx]
                acc = alpha * carry[3 * idx + 2] + jnp.dot(vt_scr[idx // 2, j], ps[idx],
                                                           preferred_element_type=F32)
                out.extend([m_new, l_new, acc])
            return tuple(out)

        state = lax.fori_loop(0, iq, kv_block, tuple(state))
        for hh in range(hpb):
            _, l0, a0, _, l1, a1 = state[6 * hh:6 * hh + 6]
            ot = a0 * (1.0 / l0) - a1 * (lam / l1)
            ot = ot * lax.rsqrt(jnp.mean(ot * ot, axis=0, keepdims=True) + EPS)
            o_ref[pl.ds(q0, tq), hh * hd2:(hh + 1) * hd2] = (ot.T * subg).astype(BF16)
        return 0

    lax.fori_loop(0, seq // tq, q_block, 0)


def _attention(q, k, v, lq1, lk1, lq2, lk2, subg, lam_init, batch, seq, *, tq=512, hpb=4):
    slopes = 2.0 ** (-8.0 * jnp.arange(1, N_HEADS + 1, dtype=F32) / N_HEADS)
    sl = slopes * math.log2(math.e)
    s_hi = sl.astype(BF16).astype(F32)
    s_mid = (sl - s_hi).astype(BF16).astype(F32)
    s_lo = (sl - s_hi - s_mid).astype(BF16).astype(F32)
    sl3 = jnp.stack([s_hi, s_mid, s_lo], axis=1)
    lam_arr = jnp.full((1,), lam_init, F32)
    smem = pl.BlockSpec(memory_space=pltpu.SMEM)
    hw = hpb * V_HEAD_DIM
    head_blk = pl.BlockSpec((seq, hw), lambda b, h: (b, h))
    vec = lambda n: pl.BlockSpec((1, n), lambda b, h: (0, 0))
    row = lambda a: a.astype(F32).reshape(1, -1)
    return pl.pallas_call(
        functools.partial(_attn_body, seq=seq, tq=tq, hpb=hpb),
        grid=(batch, N_HEADS // hpb),
        in_specs=[smem, smem, head_blk, head_blk, head_blk,
                  vec(HEAD_DIM), vec(HEAD_DIM), vec(HEAD_DIM), vec(HEAD_DIM), vec(V_HEAD_DIM)],
        out_specs=head_blk,
        out_shape=jax.ShapeDtypeStruct((batch * seq, D_V), BF16),
        scratch_shapes=[pltpu.VMEM((hpb, 2, seq, V_HEAD_DIM), BF16),
                        pltpu.VMEM((hpb, seq // tq, V_HEAD_DIM, tq), BF16),
                        pltpu.VMEM((hpb, tq, tq), F32)],
        compiler_params=pltpu.CompilerParams(
            dimension_semantics=("parallel", "parallel"),
            vmem_limit_bytes=VMEM_LIMIT),
        name="diff_attention",
    )(sl3, lam_arr, q, k, v, row(lq1), row(lk1), row(lq2), row(lk2), row(subg))


def _ssm_body(u_ref, bblk_ref, are_ref, aim_ref, cblk_ref, d_ref, y_ref,
              x_scr, h_scr, st_scr, *, batch, tt):
    nc = st_scr.shape[-1] // 2
    gcols = u_ref.shape[-1]

    @pl.when(pl.program_id(1) == 0)
    def _():
        st_scr[...] = jnp.zeros_like(st_scr)

    u_f = pltpu.einshape("btn->tbn", u_ref[...].astype(F32)).reshape(tt * batch, gcols)
    x_scr[...] = jnp.dot(u_f.astype(BF16), bblk_ref[...], preferred_element_type=F32)
    ar = jnp.broadcast_to(are_ref[...], (batch, nc))
    ai = jnp.broadcast_to(aim_ref[...], (batch, nc))

    def step(t, carry):
        hr, hi = carry
        r0 = pl.multiple_of(t * batch, batch)
        xr = x_scr[pl.ds(r0, batch), :nc]
        xi = x_scr[pl.ds(r0, batch), nc:]
        nr = ar * hr - ai * hi + xr
        ni = ar * hi + ai * hr + xi
        h_scr[pl.ds(r0, batch), :nc] = nr.astype(BF16)
        h_scr[pl.ds(r0, batch), nc:] = ni.astype(BF16)
        return nr, ni

    hr, hi = lax.fori_loop(0, tt, step, (st_scr[:, :nc], st_scr[:, nc:]), unroll=4)
    st_scr[:, :nc] = hr
    st_scr[:, nc:] = hi
    y = jnp.dot(h_scr[...], cblk_ref[...], preferred_element_type=F32)
    y = _gelu_tanh(y + d_ref[...] * u_f)
    y_ref[...] = pltpu.einshape("tbn->btn", y.reshape(tt, batch, gcols)).astype(BF16)


def _ssm(u3, bblk, are, aim, cblk, d_skip, *, tt=256):
    batch, seq, _ = u3.shape
    gcols = GROUPS_PER_TILE * SSM_GROUP
    ngt = N_GROUPS // GROUPS_PER_TILE
    tt = min(tt, seq)
    rows = tt * batch
    return pl.pallas_call(
        functools.partial(_ssm_body, batch=batch, tt=tt),
        grid=(ngt, seq // tt),
        in_specs=[
            pl.BlockSpec((batch, tt, gcols), lambda g, t: (0, t, g)),
            pl.BlockSpec((None, gcols, 2 * SSM_NC), lambda g, t: (g, 0, 0)),
            pl.BlockSpec((None, 1, SSM_NC), lambda g, t: (g, 0, 0)),
            pl.BlockSpec((None, 1, SSM_NC), lambda g, t: (g, 0, 0)),
            pl.BlockSpec((None, 2 * SSM_NC, gcols), lambda g, t: (g, 0, 0)),
            pl.BlockSpec((1, gcols), lambda g, t: (0, g)),
        ],
        out_specs=pl.BlockSpec((batch, tt, gcols), lambda g, t: (0, t, g)),
        out_shape=jax.ShapeDtypeStruct((batch, seq, D_SSM), BF16),
        scratch_shapes=[pltpu.VMEM((rows, 2 * SSM_NC), F32),
                        pltpu.VMEM((rows, 2 * SSM_NC), BF16),
                        pltpu.VMEM((batch, 2 * SSM_NC), F32)],
        compiler_params=pltpu.CompilerParams(
            dimension_semantics=("parallel", "arbitrary"),
            vmem_limit_bytes=VMEM_LIMIT),
        name="s5_scan",
    )(u3, bblk, are, aim, cblk, d_skip.astype(F32).reshape(1, D_SSM))


def _ssm_params(a_re, a_im, log_dt, b_re, b_im, c_re, c_im):
    gpt = GROUPS_PER_TILE
    ngt = N_GROUPS // gpt
    dt = jnp.exp(log_dt)[:, None]
    mag = jnp.exp(dt * a_re)
    ab_re = mag * jnp.cos(dt * a_im)
    ab_im = mag * jnp.sin(dt * a_im)
    den = a_re * a_re + a_im * a_im
    zr = ab_re - 1.0
    zi = ab_im
    f_re = (zr * a_re + zi * a_im) / den
    f_im = (zi * a_re - zr * a_im) / den
    bb_re = f_re[..., None] * b_re - f_im[..., None] * b_im
    bb_im = f_re[..., None] * b_im + f_im[..., None] * b_re
    eye = jnp.eye(gpt, dtype=F32)

    def bdiag_in(bb):
        t = bb.reshape(ngt, gpt, SSM_STATE, SSM_GROUP)
        return jnp.einsum('tgpi,gh->tgihp', t, eye).reshape(ngt, gpt * SSM_GROUP, SSM_NC)

    def bdiag_out(cc):
        t = cc.reshape(ngt, gpt, SSM_GROUP, SSM_STATE)
        return jnp.einsum('tgop,gh->tgpho', t, eye).reshape(ngt, SSM_NC, gpt * SSM_GROUP)

    bblk = jnp.concatenate([bdiag_in(bb_re), bdiag_in(bb_im)], axis=-1).astype(BF16)
    cblk = jnp.concatenate([bdiag_out(c_re), -bdiag_out(c_im)], axis=1).astype(BF16)
    return bblk, ab_re.reshape(ngt, 1, SSM_NC), ab_im.reshape(ngt, 1, SSM_NC), cblk


def _mix_body(y_ref, oa_ref, gate_ref, x_ref, gw_ref, gb_ref, wba_ref, wbs_ref, wo_ref, out_ref):
    y = y_ref[...]
    z = jnp.dot(y, gw_ref[...], preferred_element_type=F32) + gb_ref[...]
    y2 = (y.astype(F32) * jax.nn.sigmoid(z)).astype(BF16)
    o_ssm = jnp.dot(y2, wbs_ref[...], preferred_element_type=F32)
    o_att = jnp.dot(oa_ref[...], wba_ref[...], preferred_element_type=F32)
    mixed = (jax.nn.sigmoid(gate_ref[:, :D_MODEL].astype(F32)) * o_att
             + jax.nn.sigmoid(gate_ref[:, D_MODEL:].astype(F32)) * o_ssm)
    out_ref[...] = x_ref[...] + jnp.dot(mixed.astype(BF16), wo_ref[...],
                                        preferred_element_type=F32)


def _mix(y, o_att, gates, xf, glu_w, glu_b, w_ba, w_bs, w_out, batch, seq, *, tm=256):
    t = xf.shape[0]
    tm = min(tm, seq)
    full = lambda shape: pl.BlockSpec(shape, lambda i: (0, 0), pipeline_mode=pl.Buffered(1))
    return pl.pallas_call(
        _mix_body,
        grid=(t // tm,),
        in_specs=[
            pl.BlockSpec((tm, D_SSM), lambda i: (i, 0)),
            pl.BlockSpec((tm, D_V), lambda i: (i, 0)),
            pl.BlockSpec((tm, 2 * D_MODEL), lambda i: (i, 0)),
            pl.BlockSpec((tm, D_MODEL), lambda i: (i, 0)),
            full((D_SSM, D_SSM)), full((1, D_SSM)),
            full((D_V, D_MODEL)), full((D_SSM, D_MODEL)), full((D_MODEL, D_MODEL)),
        ],
        out_specs=pl.BlockSpec((tm, D_MODEL), lambda i: (i, 0)),
        out_shape=jax.ShapeDtypeStruct((t, D_MODEL), F32),
        compiler_params=pltpu.CompilerParams(
            dimension_semantics=("parallel",),
            vmem_limit_bytes=VMEM_LIMIT),
        name="gated_merge",
    )(y, o_att, gates, xf, glu_w, glu_b.astype(F32).reshape(1, D_SSM), w_ba, w_bs, w_out)


def _shift_rows(up, prev, k):
    body = pltpu.roll(up, k, axis=0)
    top = pltpu.roll(jnp.concatenate([prev, up[:8]], axis=0), k, axis=0)[8:]
    return jnp.concatenate([top, body[8:]], axis=0)


def _ffn_up_body(x_ref, g_ref, wa_ref, wv_ref, cwa_ref, cwv_ref, cba_ref, cbv_ref,
                 act_ref, h_scr, carry_a, carry_v, *, nsb):
    i = pl.program_id(0)
    j = pl.program_id(1)

    @pl.when(j == 0)
    def _():
        h_scr[...] = _rms_rows(x_ref[...], g_ref[...]).astype(BF16)

    h = h_scr[...]
    seq_start = i % nsb == 0

    def conv(w_ref, cw_ref, cb_ref, carry):
        up = jnp.dot(h, w_ref[...], preferred_element_type=F32)
        prev = jnp.where(seq_start, 0.0, carry[j])
        carry[j] = up[up.shape[0] - 8:]
        cw = cw_ref[...]
        return (cb_ref[...] + cw[0:1] * _shift_rows(up, prev, 2)
                + cw[1:2] * _shift_rows(up, prev, 1) + cw[2:3] * up)

    a = conv(wa_ref, cwa_ref, cba_ref, carry_a)
    val = conv(wv_ref, cwv_ref, cbv_ref, carry_v)
    act_ref[...] = (_gelu_tanh(a) * val).astype(BF16)


def _ffn_up(xf, g, wa, wv, cwa, cwv, cba, cbv, seq, *, tm=1024, tn=FF_TILE):
    t = xf.shape[0]
    tm = min(tm, seq)
    ncol = D_FF_PAD // tn
    const = lambda i, j: (0, 0)
    col = lambda i, j: (0, j)
    return pl.pallas_call(
        functools.partial(_ffn_up_body, nsb=seq // tm),
        grid=(t // tm, ncol),
        in_specs=[
            pl.BlockSpec((tm, D_MODEL), lambda i, j: (i, 0)),
            pl.BlockSpec((1, D_MODEL), const),
            pl.BlockSpec((D_MODEL, tn), col),
            pl.BlockSpec((D_MODEL, tn), col),
            pl.BlockSpec((CONV_W, tn), col),
            pl.BlockSpec((CONV_W, tn), col),
            pl.BlockSpec((1, tn), col),
            pl.BlockSpec((1, tn), col),
        ],
        out_specs=pl.BlockSpec((tm, tn), lambda i, j: (i, j)),
        out_shape=jax.ShapeDtypeStruct((t, D_FF_PAD), BF16),
        scratch_shapes=[pltpu.VMEM((tm, D_MODEL), BF16),
                        pltpu.VMEM((ncol, 8, tn), F32),
                        pltpu.VMEM((ncol, 8, tn), F32)],
        compiler_params=pltpu.CompilerParams(
            dimension_semantics=("arbitrary", "arbitrary"),
            vmem_limit_bytes=VMEM_LIMIT),
        name="ffn_up_conv_gate",
    )(xf, g.reshape(1, D_MODEL), wa, wv, cwa, cwv, cba, cbv)


def _ffn_down_body(act_ref, w_ref, x_ref, out_ref):
    out_ref[...] = x_ref[...] + jnp.dot(act_ref[...], w_ref[...], preferred_element_type=F32)


def _ffn_down(act, w_down, xf, *, tm=1024, tn=512):
    t = xf.shape[0]
    tm = min(tm, t)
    return pl.pallas_call(
        _ffn_down_body,
        grid=(t // tm, D_MODEL // tn),
        in_specs=[
            pl.BlockSpec((tm, D_FF_PAD), lambda i, j: (i, 0)),
            pl.BlockSpec((D_FF_PAD, tn), lambda i, j: (0, j)),
            pl.BlockSpec((tm, tn), lambda i, j: (i, j)),
        ],
        out_specs=pl.BlockSpec((tm, tn), lambda i, j: (i, j)),
        out_shape=jax.ShapeDtypeStruct((t, D_MODEL), F32),
        compiler_params=pltpu.CompilerParams(
            dimension_semantics=("parallel", "parallel"),
            vmem_limit_bytes=VMEM_LIMIT),
        name="ffn_down",
    )(act, w_down, xf)


def _pad_ff_cols(a):
    return jnp.pad(a, ((0, 0), (0, D_FF_PAD - D_FF)))


def kernel(x, norm1_g, w_in, q_norm_g, k_norm_g, lambda_q1, lambda_k1, lambda_q2, lambda_k2, subln_g, ssm_a_re, ssm_a_im, ssm_log_dt, ssm_b_re, ssm_b_im, ssm_c_re, ssm_c_im, ssm_d, ssm_glu_w, ssm_glu_b, w_branch_attn, w_branch_ssm, w_out, norm2_g, ffn_w_up, ffn_conv_w, ffn_conv_b, ffn_w_down):
    batch, seq, _ = x.shape
    depth = w_in.shape[0]
    xf = x.reshape(batch * seq, D_MODEL)
    for l in range(depth):
        lam_init = 0.8 - 0.6 * math.exp(-0.3 * l)
        q, k, v, u, gates = _in_proj(xf, norm1_g[l], w_in[l].astype(BF16),
                                     q_norm_g[l], k_norm_g[l], batch, seq)
        o_att = _attention(q, k, v, lambda_q1[l], lambda_k1[l], lambda_q2[l], lambda_k2[l],
                           subln_g[l], lam_init, batch, seq)
        bblk, are, aim, cblk = _ssm_params(ssm_a_re[l], ssm_a_im[l], ssm_log_dt[l],
                                           ssm_b_re[l], ssm_b_im[l], ssm_c_re[l], ssm_c_im[l])
        y = _ssm(u.reshape(batch, seq, D_SSM), bblk, are, aim, cblk, ssm_d[l])
        xf = _mix(y.reshape(batch * seq, D_SSM), o_att, gates, xf,
                  ssm_glu_w[l].astype(BF16), ssm_glu_b[l],
                  w_branch_attn[l].astype(BF16), w_branch_ssm[l].astype(BF16),
                  w_out[l].astype(BF16), batch, seq)
        wu = ffn_w_up[l]
        cw = ffn_conv_w[l]
        cb = ffn_conv_b[l].reshape(1, 2 * D_FF)
        w_down = jnp.pad(ffn_w_down[l], ((0, D_FF_PAD - D_FF), (0, 0))).astype(BF16)
        act = _ffn_up(xf, norm2_g[l],
                      _pad_ff_cols(wu[:, :D_FF]).astype(BF16), _pad_ff_cols(wu[:, D_FF:]).astype(BF16),
                      _pad_ff_cols(cw[:, :D_FF]), _pad_ff_cols(cw[:, D_FF:]),
                      _pad_ff_cols(cb[:, :D_FF]), _pad_ff_cols(cb[:, D_FF:]), seq)
        xf = _ffn_down(act, w_down, xf)
    return xf.reshape(batch, seq, D_MODEL)
```

```python
import functools
import math

import jax
import jax.numpy as jnp
from jax import lax
from jax.experimental import pallas as pl
from jax.experimental.pallas import tpu as pltpu

F32 = jnp.float32
BF16 = jnp.bfloat16

D_MODEL = 2048
N_HEADS = 8
HEAD_DIM = 64
V_HEAD_DIM = 2 * HEAD_DIM
D_QK = N_HEADS * 2 * HEAD_DIM
D_V = N_HEADS * V_HEAD_DIM
D_SSM = D_MODEL // 2
SSM_GROUP = 16
N_GROUPS = D_SSM // SSM_GROUP
SSM_STATE = 64
D_FF = 5504
CONV_W = 3
CHUNK = 64
EPS = 1e-6
IN_COLS = 2 * D_QK + D_V + D_SSM + 2 * D_MODEL

MXU_DIM = 256
FF_TILE = 512
ROW_CHUNK = 256
D_FF_PAD = ((D_FF + FF_TILE - 1) // FF_TILE) * FF_TILE
GROUPS_PER_TILE = 8
SSM_NC = GROUPS_PER_TILE * SSM_STATE
NEG_BIG = -1e30
Q_SCALE = HEAD_DIM ** -0.5 * math.log2(math.e)
VMEM_LIMIT = 56 * 1024 * 1024


def _gelu_tanh(x):
    c = math.sqrt(2.0 / math.pi)
    return x * (0.5 * (1.0 + jnp.tanh(c * (x + 0.044715 * (x * x * x)))))


def _rms_rows(xf, g):
    ms = jnp.mean(xf * xf, axis=-1, keepdims=True)
    return xf * lax.rsqrt(ms + EPS) * g


def _in_proj_body(x_ref, g_ref, w_ref, qg_ref, kg_ref, ones_ref,
                  q_ref, k_ref, v_ref, u_ref, gate_ref, h_scr, *, tn):
    j = pl.program_id(1)
    nq = D_QK // tn

    @pl.when(j == 0)
    def _():
        h_scr[...] = _rms_rows(x_ref[...], g_ref[...]).astype(BF16)

    tm = h_scr.shape[0]
    rc = min(ROW_CHUNK, tm)

    def tile(out_ref, epilogue):
        nchunk = tm // rc
        rows = lambda q: slice(q * rc, (q + 1) * rc)
        dot_q = lambda q: jnp.dot(h_scr[rows(q), :], w_ref[...], preferred_element_type=F32)
        acc = dot_q(0)
        for q in range(1, nchunk):
            nxt = dot_q(q)
            out_ref[rows(q - 1), :] = epilogue(acc).astype(BF16)
            acc = nxt
        out_ref[rows(nchunk - 1), :] = epilogue(acc).astype(BF16)

    def head_norm(gain_ref, scale):
        def epilogue(acc):
            outs = []
            for c in range(tn // MXU_DIM):
                a = acc[:, c * MXU_DIM:(c + 1) * MXU_DIM]
                ss = jnp.dot((a * a).astype(BF16), ones_ref[...], preferred_element_type=F32)
                outs.append(a * lax.rsqrt(ss * (1.0 / HEAD_DIM) + EPS) * (gain_ref[...] * scale))
            return jnp.concatenate(outs, axis=1)
        return epilogue

    plain = lambda acc: acc

    @pl.when(j < nq)
    def _():
        tile(q_ref, head_norm(qg_ref, Q_SCALE))

    @pl.when((j >= nq) & (j < 2 * nq))
    def _():
        tile(k_ref, head_norm(kg_ref, 1.0))

    @pl.when((j >= 2 * nq) & (j < 3 * nq))
    def _():
        tile(v_ref, plain)

    @pl.when((j >= 3 * nq) & (j < 4 * nq))
    def _():
        tile(u_ref, plain)

    @pl.when(j >= 4 * nq)
    def _():
        tile(gate_ref, plain)


def _in_proj(xf, g, w_bf, qg, kg, batch, seq, *, tm=1024, tn=1024):
    t = xf.shape[0]
    tm = min(tm, seq)
    nq = D_QK // tn
    ng = 2 * D_MODEL // tn
    qg_t = jnp.tile(qg.astype(F32), MXU_DIM // HEAD_DIM).reshape(1, MXU_DIM)
    kg_t = jnp.tile(kg.astype(F32), MXU_DIM // HEAD_DIM).reshape(1, MXU_DIM)
    seg = jnp.arange(MXU_DIM) // HEAD_DIM
    ones_bd = (seg[:, None] == seg[None, :]).astype(BF16)

    def cl(j, lo, n):
        return jnp.clip(j - lo, 0, n - 1)

    const = lambda i, j: (0, 0)
    return pl.pallas_call(
        functools.partial(_in_proj_body, tn=tn),
        grid=(t // tm, IN_COLS // tn),
        in_specs=[
            pl.BlockSpec((tm, D_MODEL), lambda i, j: (i, 0)),
            pl.BlockSpec((1, D_MODEL), const),
            pl.BlockSpec((D_MODEL, tn), lambda i, j: (0, j)),
            pl.BlockSpec((1, MXU_DIM), const),
            pl.BlockSpec((1, MXU_DIM), const),
            pl.BlockSpec((MXU_DIM, MXU_DIM), const),
        ],
        out_specs=[
            pl.BlockSpec((tm, tn), lambda i, j: (i, cl(j, 0, nq))),
            pl.BlockSpec((tm, tn), lambda i, j: (i, cl(j, nq, nq))),
            pl.BlockSpec((tm, tn), lambda i, j: (i, cl(j, 2 * nq, nq))),
            pl.BlockSpec((tm, tn), lambda i, j: (i, cl(j, 3 * nq, nq))),
            pl.BlockSpec((tm, tn), lambda i, j: (i, cl(j, 4 * nq, ng))),
        ],
        out_shape=[
            jax.ShapeDtypeStruct((t, D_QK), BF16),
            jax.ShapeDtypeStruct((t, D_QK), BF16),
            jax.ShapeDtypeStruct((t, D_V), BF16),
            jax.ShapeDtypeStruct((t, D_SSM), BF16),
            jax.ShapeDtypeStruct((t, 2 * D_MODEL), BF16),
        ],
        scratch_shapes=[pltpu.VMEM((tm, D_MODEL), BF16)],
        compiler_params=pltpu.CompilerParams(
            dimension_semantics=("arbitrary", "arbitrary"),
            vmem_limit_bytes=VMEM_LIMIT),
        name="in_proj",
    )(xf, g.reshape(1, D_MODEL), w_bf, qg_t, kg_t, ones_bd)


N_BIAS_ROWS = 3
OFFSET_RADIX = 256
BIAS_LANE0 = 0


def _attn_body(sl_ref, laminit_ref, q_ref, k_ref, v_ref, lq1_ref, lk1_ref, lq2_ref, lk2_ref,
               subg_ref, o_ref, kaug_scr, vt_scr, dmat_scr, *, seq, tq, hpb):
    tk = tq
    nblk = seq // tk
    hd2 = 2 * HEAD_DIM
    nch = 2 * hpb
    hg = pl.program_id(1)
    lam_init = laminit_ref[0]
    lam = (jnp.exp(jnp.sum(lq1_ref[...] * lk1_ref[...], axis=-1, keepdims=True))
           - jnp.exp(jnp.sum(lq2_ref[...] * lk2_ref[...], axis=-1, keepdims=True))
           + lam_init)
    subg = subg_ref[...] * (1.0 - lam_init)

    koff = lax.broadcasted_iota(jnp.int32, (tk, hd2), 0)
    klane = lax.broadcasted_iota(jnp.int32, (tk, hd2), 1)
    bias_lo = BIAS_LANE0
    bias_hi = HEAD_DIM + BIAS_LANE0
    kbias = [jnp.where((klane >= lo) & (klane < lo + N_BIAS_ROWS), koff % OFFSET_RADIX,
                       jnp.where((klane >= lo + N_BIAS_ROWS) & (klane < lo + 2 * N_BIAS_ROWS),
                                 koff // OFFSET_RADIX, 0)).astype(F32).astype(BF16)
             for lo in (bias_hi, bias_lo)]
    c = lax.broadcasted_iota(jnp.int32, (tk, tq), 0)
    r = lax.broadcasted_iota(jnp.int32, (tk, tq), 1)
    visible = (c // CHUNK) <= (r // CHUNK)
    ahead = jnp.maximum(c - r, 0).astype(F32)
    brow_i = lax.broadcasted_iota(jnp.int32, (hd2, tq), 0)
    sl2 = []
    brows = []
    for hh in range(hpb):
        h = hg * hpb + hh
        s_hi, s_mid, s_lo = sl_ref[h, 0], sl_ref[h, 1], sl_ref[h, 2]
        sl2.append(s_hi + s_mid + s_lo)
        radix = float(OFFSET_RADIX)
        pieces = (s_hi, s_mid, s_lo, radix * s_hi, radix * s_mid, radix * s_lo)

        def bias_rows(lo):
            rows = jnp.zeros((hd2, tq), F32)
            for n, piece in enumerate(pieces):
                rows = jnp.where(brow_i == lo + n, piece, rows)
            return rows

        brows.append([bias_rows(bias_hi), bias_rows(bias_lo)])
        for blk in range(nblk):
            blk_rows = slice(blk * tk, (blk + 1) * tk)
            k_blk = k_ref[blk_rows, hh * hd2:(hh + 1) * hd2]
            kaug_scr[hh, 0, blk_rows, :] = jnp.where(klane < HEAD_DIM, k_blk, kbias[0])
            kaug_scr[hh, 1, blk_rows, :] = jnp.where(klane >= HEAD_DIM, k_blk, kbias[1])
            vt_scr[hh, blk] = v_ref[blk_rows, hh * hd2:(hh + 1) * hd2].astype(F32).T.astype(BF16)
        dmat_scr[hh] = jnp.where(visible, (-2.0 * sl2[hh]) * ahead, NEG_BIG)

    def q_block(iq, _):
        q0 = pl.multiple_of(iq * tq, tq)
        qas = []
        for hh in range(hpb):
            qt = q_ref[pl.ds(q0, tq), hh * hd2:(hh + 1) * hd2].astype(F32).T
            for comp in range(2):
                keep = (brow_i < HEAD_DIM) if comp == 0 else (brow_i >= HEAD_DIM)
                qas.append(jnp.where(keep, qt, brows[hh][comp]).astype(BF16))
        ss = [jnp.dot(kaug_scr[idx // 2, idx % 2, pl.ds(q0, tk), :], qas[idx],
                      preferred_element_type=F32) + dmat_scr[idx // 2] for idx in range(nch)]
        ps, stats = [], []
        for idx in range(nch):
            m = jnp.max(ss[idx], axis=0, keepdims=True)
            p = jnp.exp2(ss[idx] - m)
            stats.append((m, jnp.sum(p, axis=0, keepdims=True)))
            ps.append(p.astype(BF16))
        state = []
        for idx in range(nch):
            acc = jnp.dot(vt_scr[idx // 2, iq], ps[idx], preferred_element_type=F32)
            state.extend([stats[idx][0], stats[idx][1], acc])

        def kv_block(j, carry):
            k0 = pl.multiple_of(j * tk, tk)
            boff = lax.convert_element_type((j - iq) * tk, F32)
            ss = [jnp.dot(kaug_scr[idx // 2, idx % 2, pl.ds(k0, tk), :], qas[idx],
                          preferred_element_type=F32) for idx in range(nch)]
            ps, stats = [], []
            for idx in range(nch):
                m, l, _ = carry[3 * idx:3 * idx + 3]
                bc = sl2[idx // 2] * boff
                m_new = jnp.maximum(m, jnp.max(ss[idx], axis=0, keepdims=True) + bc)
                alpha = jnp.exp2(m - m_new)
                p = jnp.exp2(ss[idx] - (m_new - bc))
                stats.append((m_new, alpha * l + jnp.sum(p, axis=0, keepdims=True), alpha))
                ps.append(p.astype(BF16))
            out = []
            for idx in range(nch):
                m_new, l_new, alpha = stats[idx]
                acc = alpha * carry[3 * idx + 2] + jnp.dot(vt_scr[idx // 2, j], ps[idx],
                                                           preferred_element_type=F32)
                out.extend([m_new, l_new, acc])
            return tuple(out)

        state = lax.fori_loop(0, iq, kv_block, tuple(state))
        for hh in range(hpb):
            _, l0, a0, _, l1, a1 = state[6 * hh:6 * hh + 6]
            ot = a0 * (1.0 / l0) - a1 * (lam / l1)
            ot = ot * lax.rsqrt(jnp.mean(ot * ot, axis=0, keepdims=True) + EPS)
            o_ref[pl.ds(q0, tq), hh * hd2:(hh + 1) * hd2] = (ot.T * subg).astype(BF16)
        return 0

    lax.fori_loop(0, seq // tq, q_block, 0)


def _attention(q, k, v, lq1, lk1, lq2, lk2, subg, lam_init, batch, seq, *, tq=512, hpb=4):
    slopes = 2.0 ** (-8.0 * jnp.arange(1, N_HEADS + 1, dtype=F32) / N_HEADS)
    sl = slopes * math.log2(math.e)
    s_hi = sl.astype(BF16).astype(F32)
    s_mid = (sl - s_hi).astype(BF16).astype(F32)
    s_lo = (sl - s_hi - s_mid).astype(BF16).astype(F32)
    sl3 = jnp.stack([s_hi, s_mid, s_lo], axis=1)
    lam_arr = jnp.full((1,), lam_init, F32)
    smem = pl.BlockSpec(memory_space=pltpu.SMEM)
    hw = hpb * V_HEAD_DIM
    head_blk = pl.BlockSpec((seq, hw), lambda b, h: (b, h))
    vec = lambda n: pl.BlockSpec((1, n), lambda b, h: (0, 0))
    row = lambda a: a.astype(F32).reshape(1, -1)
    return pl.pallas_call(
        functools.partial(_attn_body, seq=seq, tq=tq, hpb=hpb),
        grid=(batch, N_HEADS // hpb),
        in_specs=[smem, smem, head_blk, head_blk, head_blk,
                  vec(HEAD_DIM), vec(HEAD_DIM), vec(HEAD_DIM), vec(HEAD_DIM), vec(V_HEAD_DIM)],
        out_specs=head_blk,
        out_shape=jax.ShapeDtypeStruct((batch * seq, D_V), BF16),
        scratch_shapes=[pltpu.VMEM((hpb, 2, seq, V_HEAD_DIM), BF16),
                        pltpu.VMEM((hpb, seq // tq, V_HEAD_DIM, tq), BF16),
                        pltpu.VMEM((hpb, tq, tq), F32)],
        compiler_params=pltpu.CompilerParams(
            dimension_semantics=("parallel", "parallel"),
            vmem_limit_bytes=VMEM_LIMIT),
        name="diff_attention",
    )(sl3, lam_arr, q, k, v, row(lq1), row(lk1), row(lq2), row(lk2), row(subg))


def _ssm_body(u_ref, bblk_ref, are_ref, aim_ref, cblk_ref, d_ref, y_ref,
              x_scr, h_scr, st_scr, *, batch, tt):
    nc = st_scr.shape[-1] // 2
    gcols = u_ref.shape[-1]

    @pl.when(pl.program_id(1) == 0)
    def _():
        st_scr[...] = jnp.zeros_like(st_scr)

    u_f = pltpu.einshape("btn->tbn", u_ref[...].astype(F32)).reshape(tt * batch, gcols)
    x_scr[...] = jnp.dot(u_f.astype(BF16), bblk_ref[...], preferred_element_type=F32)
    ar = jnp.broadcast_to(are_ref[...], (batch, nc))
    ai = jnp.broadcast_to(aim_ref[...], (batch, nc))

    def step(t, carry):
        hr, hi = carry
        r0 = pl.multiple_of(t * batch, batch)
        xr = x_scr[pl.ds(r0, batch), :nc]
        xi = x_scr[pl.ds(r0, batch), nc:]
        nr = ar * hr - ai * hi + xr
        ni = ar * hi + ai * hr + xi
        h_scr[pl.ds(r0, batch), :nc] = nr.astype(BF16)
        h_scr[pl.ds(r0, batch), nc:] = ni.astype(BF16)
        return nr, ni

    hr, hi = lax.fori_loop(0, tt, step, (st_scr[:, :nc], st_scr[:, nc:]), unroll=4)
    st_scr[:, :nc] = hr
    st_scr[:, nc:] = hi
    y = jnp.dot(h_scr[...], cblk_ref[...], preferred_element_type=F32)
    y = _gelu_tanh(y + d_ref[...] * u_f)
    y_ref[...] = pltpu.einshape("tbn->btn", y.reshape(tt, batch, gcols)).astype(BF16)


def _ssm(u3, bblk, are, aim, cblk, d_skip, *, tt=256):
    batch, seq, _ = u3.shape
    gcols = GROUPS_PER_TILE * SSM_GROUP
    ngt = N_GROUPS // GROUPS_PER_TILE
    tt = min(tt, seq)
    rows = tt * batch
    return pl.pallas_call(
        functools.partial(_ssm_body, batch=batch, tt=tt),
        grid=(ngt, seq // tt),
        in_specs=[
            pl.BlockSpec((batch, tt, gcols), lambda g, t: (0, t, g)),
            pl.BlockSpec((None, gcols, 2 * SSM_NC), lambda g, t: (g, 0, 0)),
            pl.BlockSpec((None, 1, SSM_NC), lambda g, t: (g, 0, 0)),
            pl.BlockSpec((None, 1, SSM_NC), lambda g, t: (g, 0, 0)),
            pl.BlockSpec((None, 2 * SSM_NC, gcols), lambda g, t: (g, 0, 0)),
            pl.BlockSpec((1, gcols), lambda g, t: (0, g)),
        ],
        out_specs=pl.BlockSpec((batch, tt, gcols), lambda g, t: (0, t, g)),
        out_shape=jax.ShapeDtypeStruct((batch, seq, D_SSM), BF16),
        scratch_shapes=[pltpu.VMEM((rows, 2 * SSM_NC), F32),
                        pltpu.VMEM((rows, 2 * SSM_NC), BF16),
                        pltpu.VMEM((batch, 2 * SSM_NC), F32)],
        compiler_params=pltpu.CompilerParams(
            dimension_semantics=("parallel", "arbitrary"),
            vmem_limit_bytes=VMEM_LIMIT),
        name="s5_scan",
    )(u3, bblk, are, aim, cblk, d_skip.astype(F32).reshape(1, D_SSM))


def _ssm_params(a_re, a_im, log_dt, b_re, b_im, c_re, c_im):
    gpt = GROUPS_PER_TILE
    ngt = N_GROUPS // gpt
    dt = jnp.exp(log_dt)[:, None]
    mag = jnp.exp(dt * a_re)
    ab_re = mag * jnp.cos(dt * a_im)
    ab_im = mag * jnp.sin(dt * a_im)
    den = a_re * a_re + a_im * a_im
    zr = ab_re - 1.0
    zi = ab_im
    f_re = (zr * a_re + zi * a_im) / den
    f_im = (zi * a_re - zr * a_im) / den
    bb_re = f_re[..., None] * b_re - f_im[..., None] * b_im
    bb_im = f_re[..., None] * b_im + f_im[..., None] * b_re
    eye = jnp.eye(gpt, dtype=F32)

    def bdiag_in(bb):
        t = bb.reshape(ngt, gpt, SSM_STATE, SSM_GROUP)
        return jnp.einsum('tgpi,gh->tgihp', t, eye).reshape(ngt, gpt * SSM_GROUP, SSM_NC)

    def bdiag_out(cc):
        t = cc.reshape(ngt, gpt, SSM_GROUP, SSM_STATE)
        return jnp.einsum('tgop,gh->tgpho', t, eye).reshape(ngt, SSM_NC, gpt * SSM_GROUP)

    bblk = jnp.concatenate([bdiag_in(bb_re), bdiag_in(bb_im)], axis=-1).astype(BF16)
    cblk = jnp.concatenate([bdiag_out(c_re), -bdiag_out(c_im)], axis=1).astype(BF16)
    return bblk, ab_re.reshape(ngt, 1, SSM_NC), ab_im.reshape(ngt, 1, SSM_NC), cblk


def _mix_body(y_ref, oa_ref, gate_ref, x_ref, gw_ref, gb_ref, wba_ref, wbs_ref, wo_ref, out_ref):
    y = y_ref[...]
    z = jnp.dot(y, gw_ref[...], preferred_element_type=F32) + gb_ref[...]
    y2 = (y.astype(F32) * jax.nn.sigmoid(z)).astype(BF16)
    o_ssm = jnp.dot(y2, wbs_ref[...], preferred_element_type=F32)
    o_att = jnp.dot(oa_ref[...], wba_ref[...], preferred_element_type=F32)
    mixed = (jax.nn.sigmoid(gate_ref[:, :D_MODEL].astype(F32)) * o_att
             + jax.nn.sigmoid(gate_ref[:, D_MODEL:].astype(F32)) * o_ssm)
    out_ref[...] = x_ref[...] + jnp.dot(mixed.astype(BF16), wo_ref[...],
                                        preferred_element_type=F32)


def _mix(y, o_att, gates, xf, glu_w, glu_b, w_ba, w_bs, w_out, batch, seq, *, tm=512):
    t = xf.shape[0]
    tm = min(tm, seq)
    full = lambda shape: pl.BlockSpec(shape, lambda i: (0, 0), pipeline_mode=pl.Buffered(1))
    return pl.pallas_call(
        _mix_body,
        grid=(t // tm,),
        in_specs=[
            pl.BlockSpec((tm, D_SSM), lambda i: (i, 0)),
            pl.BlockSpec((tm, D_V), lambda i: (i, 0)),
            pl.BlockSpec((tm, 2 * D_MODEL), lambda i: (i, 0)),
            pl.BlockSpec((tm, D_MODEL), lambda i: (i, 0)),
            full((D_SSM, D_SSM)), full((1, D_SSM)),
            full((D_V, D_MODEL)), full((D_SSM, D_MODEL)), full((D_MODEL, D_MODEL)),
        ],
        out_specs=pl.BlockSpec((tm, D_MODEL), lambda i: (i, 0)),
        out_shape=jax.ShapeDtypeStruct((t, D_MODEL), F32),
        compiler_params=pltpu.CompilerParams(
            dimension_semantics=("parallel",),
            vmem_limit_bytes=VMEM_LIMIT),
        name="gated_merge",
    )(y, o_att, gates, xf, glu_w, glu_b.astype(F32).reshape(1, D_SSM), w_ba, w_bs, w_out)


def _shift_rows(up, prev, k):
    body = pltpu.roll(up, k, axis=0)
    top = pltpu.roll(jnp.concatenate([prev, up[:8]], axis=0), k, axis=0)[8:]
    return jnp.concatenate([top, body[8:]], axis=0)


def _ffn_up_body(x_ref, g_ref, wa_ref, wv_ref, cwa_ref, cwv_ref, cba_ref, cbv_ref,
                 act_ref, h_scr, carry_a, carry_v, *, nsb):
    i = pl.program_id(0)
    j = pl.program_id(1)

    @pl.when(j == 0)
    def _():
        h_scr[...] = _rms_rows(x_ref[...], g_ref[...]).astype(BF16)

    h = h_scr[...]
    seq_start = i % nsb == 0

    def conv(w_ref, cw_ref, cb_ref, carry):
        up = jnp.dot(h, w_ref[...], preferred_element_type=F32)
        prev = jnp.where(seq_start, 0.0, carry[j])
        carry[j] = up[up.shape[0] - 8:]
        cw = cw_ref[...]
        return (cb_ref[...] + cw[0:1] * _shift_rows(up, prev, 2)
                + cw[1:2] * _shift_rows(up, prev, 1) + cw[2:3] * up)

    a = conv(wa_ref, cwa_ref, cba_ref, carry_a)
    val = conv(wv_ref, cwv_ref, cbv_ref, carry_v)
    act_ref[...] = (_gelu_tanh(a) * val).astype(BF16)


def _ffn_up(xf, g, wa, wv, cwa, cwv, cba, cbv, seq, *, tm=1024, tn=FF_TILE):
    t = xf.shape[0]
    tm = min(tm, seq)
    ncol = D_FF_PAD // tn
    const = lambda i, j: (0, 0)
    col = lambda i, j: (0, j)
    return pl.pallas_call(
        functools.partial(_ffn_up_body, nsb=seq // tm),
        grid=(t // tm, ncol),
        in_specs=[
            pl.BlockSpec((tm, D_MODEL), lambda i, j: (i, 0)),
            pl.BlockSpec((1, D_MODEL), const),
            pl.BlockSpec((D_MODEL, tn), col),
            pl.BlockSpec((D_MODEL, tn), col),
            pl.BlockSpec((CONV_W, tn), col),
            pl.BlockSpec((CONV_W, tn), col),
            pl.BlockSpec((1, tn), col),
            pl.BlockSpec((1, tn), col),
        ],
        out_specs=pl.BlockSpec((tm, tn), lambda i, j: (i, j)),
        out_shape=jax.ShapeDtypeStruct((t, D_FF_PAD), BF16),
        scratch_shapes=[pltpu.VMEM((tm, D_MODEL), BF16),
                        pltpu.VMEM((ncol, 8, tn), F32),
                        pltpu.VMEM((ncol, 8, tn), F32)],
        compiler_params=pltpu.CompilerParams(
            dimension_semantics=("arbitrary", "arbitrary"),
            vmem_limit_bytes=VMEM_LIMIT),
        name="ffn_up_conv_gate",
    )(xf, g.reshape(1, D_MODEL), wa, wv, cwa, cwv, cba, cbv)


def _ffn_down_body(act_ref, w_ref, x_ref, out_ref):
    out_ref[...] = x_ref[...] + jnp.dot(act_ref[...], w_ref[...], preferred_element_type=F32)


def _ffn_down(act, w_down, xf, *, tm=1024, tn=512):
    t = xf.shape[0]
    tm = min(tm, t)
    return pl.pallas_call(
        _ffn_down_body,
        grid=(t // tm, D_MODEL // tn),
        in_specs=[
            pl.BlockSpec((tm, D_FF_PAD), lambda i, j: (i, 0)),
            pl.BlockSpec((D_FF_PAD, tn), lambda i, j: (0, j)),
            pl.BlockSpec((tm, tn), lambda i, j: (i, j)),
        ],
        out_specs=pl.BlockSpec((tm, tn), lambda i, j: (i, j)),
        out_shape=jax.ShapeDtypeStruct((t, D_MODEL), F32),
        compiler_params=pltpu.CompilerParams(
            dimension_semantics=("parallel", "parallel"),
            vmem_limit_bytes=VMEM_LIMIT),
        name="ffn_down",
    )(act, w_down, xf)


def _pad_ff_cols(a):
    return jnp.pad(a, ((0, 0), (0, D_FF_PAD - D_FF)))


def kernel(x, norm1_g, w_in, q_norm_g, k_norm_g, lambda_q1, lambda_k1, lambda_q2, lambda_k2, subln_g, ssm_a_re, ssm_a_im, ssm_log_dt, ssm_b_re, ssm_b_im, ssm_c_re, ssm_c_im, ssm_d, ssm_glu_w, ssm_glu_b, w_branch_attn, w_branch_ssm, w_out, norm2_g, ffn_w_up, ffn_conv_w, ffn_conv_b, ffn_w_down):
    batch, seq, _ = x.shape
    depth = w_in.shape[0]
    xf = x.reshape(batch * seq, D_MODEL)
    for l in range(depth):
        lam_init = 0.8 - 0.6 * math.exp(-0.3 * l)
        q, k, v, u, gates = _in_proj(xf, norm1_g[l], w_in[l].astype(BF16),
                                     q_norm_g[l], k_norm_g[l], batch, seq)
        o_att = _attention(q, k, v, lambda_q1[l], lambda_k1[l], lambda_q2[l], lambda_k2[l],
                           subln_g[l], lam_init, batch, seq)
        bblk, are, aim, cblk = _ssm_params(ssm_a_re[l], ssm_a_im[l], ssm_log_dt[l],
                                           ssm_b_re[l], ssm_b_im[l], ssm_c_re[l], ssm_c_im[l])
        y = _ssm(u.reshape(batch, seq, D_SSM), bblk, are, aim, cblk, ssm_d[l])
        xf = _mix(y.reshape(batch * seq, D_SSM), o_att, gates, xf,
                  ssm_glu_w[l].astype(BF16), ssm_glu_b[l],
                  w_branch_attn[l].astype(BF16), w_branch_ssm[l].astype(BF16),
                  w_out[l].astype(BF16), batch, seq)
        wu = ffn_w_up[l]
        cw = ffn_conv_w[l]
        cb = ffn_conv_b[l].reshape(1, 2 * D_FF)
        w_down = jnp.pad(ffn_w_down[l], ((0, D_FF_PAD - D_FF), (0, 0))).astype(BF16)
        act = _ffn_up(xf, norm2_g[l],
                      _pad_ff_cols(wu[:, :D_FF]).astype(BF16), _pad_ff_cols(wu[:, D_FF:]).astype(BF16),
                      _pad_ff_cols(cw[:, :D_FF]), _pad_ff_cols(cw[:, D_FF:]),
                      _pad_ff_cols(cb[:, :D_FF]), _pad_ff_cols(cb[:, D_FF:]), seq)
        xf = _ffn_down(act, w_down, xf)
    return xf.reshape(batch, seq, D_MODEL)
```

```python
import functools
import math

import jax
import jax.numpy as jnp
from jax import lax
from jax.experimental import pallas as pl
from jax.experimental.pallas import tpu as pltpu

F32 = jnp.float32
BF16 = jnp.bfloat16

D_MODEL = 2048
N_HEADS = 8
HEAD_DIM = 64
V_HEAD_DIM = 2 * HEAD_DIM
D_QK = N_HEADS * 2 * HEAD_DIM
D_V = N_HEADS * V_HEAD_DIM
D_SSM = D_MODEL // 2
SSM_GROUP = 16
N_GROUPS = D_SSM // SSM_GROUP
SSM_STATE = 64
D_FF = 5504
CONV_W = 3
CHUNK = 64
EPS = 1e-6
IN_COLS = 2 * D_QK + D_V + D_SSM + 2 * D_MODEL

MXU_DIM = 256
FF_TILE = 512
ROW_CHUNK = 256
D_FF_PAD = ((D_FF + FF_TILE - 1) // FF_TILE) * FF_TILE
GROUPS_PER_TILE = 8
SSM_NC = GROUPS_PER_TILE * SSM_STATE
NEG_BIG = -1e30
Q_SCALE = HEAD_DIM ** -0.5 * math.log2(math.e)
VMEM_LIMIT = 56 * 1024 * 1024


def _gelu_tanh(x):
    c = math.sqrt(2.0 / math.pi)
    return x * (0.5 * (1.0 + jnp.tanh(c * (x + 0.044715 * (x * x * x)))))


def _rms_rows(xf, g):
    ms = jnp.mean(xf * xf, axis=-1, keepdims=True)
    return xf * lax.rsqrt(ms + EPS) * g


def _in_proj_body(x_ref, g_ref, w_ref, qg_ref, kg_ref, ones_ref,
                  q_ref, k_ref, v_ref, u_ref, gate_ref, h_scr, *, tn):
    j = pl.program_id(1)
    nq = D_QK // tn

    @pl.when(j == 0)
    def _():
        h_scr[...] = _rms_rows(x_ref[...], g_ref[...]).astype(BF16)

    tm = h_scr.shape[0]
    rc = min(ROW_CHUNK, tm)

    def tile(out_ref, epilogue):
        nchunk = tm // rc
        rows = lambda q: slice(q * rc, (q + 1) * rc)
        dot_q = lambda q: jnp.dot(h_scr[rows(q), :], w_ref[...], preferred_element_type=F32)
        acc = dot_q(0)
        for q in range(1, nchunk):
            nxt = dot_q(q)
            out_ref[rows(q - 1), :] = epilogue(acc).astype(BF16)
            acc = nxt
        out_ref[rows(nchunk - 1), :] = epilogue(acc).astype(BF16)

    def head_norm(gain_ref, scale):
        def epilogue(acc):
            outs = []
            for c in range(tn // MXU_DIM):
                a = acc[:, c * MXU_DIM:(c + 1) * MXU_DIM]
                ss = jnp.dot((a * a).astype(BF16), ones_ref[...], preferred_element_type=F32)
                outs.append(a * lax.rsqrt(ss * (1.0 / HEAD_DIM) + EPS) * (gain_ref[...] * scale))
            return jnp.concatenate(outs, axis=1)
        return epilogue

    plain = lambda acc: acc

    @pl.when(j < nq)
    def _():
        tile(q_ref, head_norm(qg_ref, Q_SCALE))

    @pl.when((j >= nq) & (j < 2 * nq))
    def _():
        tile(k_ref, head_norm(kg_ref, 1.0))

    @pl.when((j >= 2 * nq) & (j < 3 * nq))
    def _():
        tile(v_ref, plain)

    @pl.when((j >= 3 * nq) & (j < 4 * nq))
    def _():
        tile(u_ref, plain)

    @pl.when(j >= 4 * nq)
    def _():
        tile(gate_ref, plain)


def _in_proj(xf, g, w_bf, qg, kg, batch, seq, *, tm=1024, tn=1024):
    t = xf.shape[0]
    tm = min(tm, seq)
    nq = D_QK // tn
    ng = 2 * D_MODEL // tn
    qg_t = jnp.tile(qg.astype(F32), MXU_DIM // HEAD_DIM).reshape(1, MXU_DIM)
    kg_t = jnp.tile(kg.astype(F32), MXU_DIM // HEAD_DIM).reshape(1, MXU_DIM)
    seg = jnp.arange(MXU_DIM) // HEAD_DIM
    ones_bd = (seg[:, None] == seg[None, :]).astype(BF16)

    def cl(j, lo, n):
        return jnp.clip(j - lo, 0, n - 1)

    const = lambda i, j: (0, 0)
    return pl.pallas_call(
        functools.partial(_in_proj_body, tn=tn),
        grid=(t // tm, IN_COLS // tn),
        in_specs=[
            pl.BlockSpec((tm, D_MODEL), lambda i, j: (i, 0)),
            pl.BlockSpec((1, D_MODEL), const),
            pl.BlockSpec((D_MODEL, tn), lambda i, j: (0, j)),
            pl.BlockSpec((1, MXU_DIM), const),
            pl.BlockSpec((1, MXU_DIM), const),
            pl.BlockSpec((MXU_DIM, MXU_DIM), const),
        ],
        out_specs=[
            pl.BlockSpec((tm, tn), lambda i, j: (i, cl(j, 0, nq))),
            pl.BlockSpec((tm, tn), lambda i, j: (i, cl(j, nq, nq))),
            pl.BlockSpec((tm, tn), lambda i, j: (i, cl(j, 2 * nq, nq))),
            pl.BlockSpec((tm, tn), lambda i, j: (i, cl(j, 3 * nq, nq))),
            pl.BlockSpec((tm, tn), lambda i, j: (i, cl(j, 4 * nq, ng))),
        ],
        out_shape=[
            jax.ShapeDtypeStruct((t, D_QK), BF16),
            jax.ShapeDtypeStruct((t, D_QK), BF16),
            jax.ShapeDtypeStruct((t, D_V), BF16),
            jax.ShapeDtypeStruct((t, D_SSM), BF16),
            jax.ShapeDtypeStruct((t, 2 * D_MODEL), BF16),
        ],
        scratch_shapes=[pltpu.VMEM((tm, D_MODEL), BF16)],
        compiler_params=pltpu.CompilerParams(
            dimension_semantics=("arbitrary", "arbitrary"),
            vmem_limit_bytes=VMEM_LIMIT),
        name="in_proj",
    )(xf, g.reshape(1, D_MODEL), w_bf, qg_t, kg_t, ones_bd)


N_BIAS_ROWS = 3
OFFSET_RADIX = 256
BIAS_LANE0 = 0


def _attn_body(sl_ref, laminit_ref, q_ref, k_ref, v_ref, lq1_ref, lk1_ref, lq2_ref, lk2_ref,
               subg_ref, o_ref, kaug_scr, vt_scr, dmat_scr, *, seq, tq, hpb):
    tk = tq
    nblk = seq // tk
    hd2 = 2 * HEAD_DIM
    nch = 2 * hpb
    hg = pl.program_id(1)
    lam_init = laminit_ref[0]
    lam = (jnp.exp(jnp.sum(lq1_ref[...] * lk1_ref[...], axis=-1, keepdims=True))
           - jnp.exp(jnp.sum(lq2_ref[...] * lk2_ref[...], axis=-1, keepdims=True))
           + lam_init)
    subg = subg_ref[...] * (1.0 - lam_init)

    koff = lax.broadcasted_iota(jnp.int32, (tk, hd2), 0)
    klane = lax.broadcasted_iota(jnp.int32, (tk, hd2), 1)
    bias_lo = BIAS_LANE0
    bias_hi = HEAD_DIM + BIAS_LANE0
    kbias = [jnp.where((klane >= lo) & (klane < lo + N_BIAS_ROWS), koff % OFFSET_RADIX,
                       jnp.where((klane >= lo + N_BIAS_ROWS) & (klane < lo + 2 * N_BIAS_ROWS),
                                 koff // OFFSET_RADIX, 0)).astype(F32).astype(BF16)
             for lo in (bias_hi, bias_lo)]
    c = lax.broadcasted_iota(jnp.int32, (tk, tq), 0)
    r = lax.broadcasted_iota(jnp.int32, (tk, tq), 1)
    visible = (c // CHUNK) <= (r // CHUNK)
    ahead = jnp.maximum(c - r, 0).astype(F32)
    brow_i = lax.broadcasted_iota(jnp.int32, (hd2, tq), 0)
    sl2 = []
    brows = []
    for hh in range(hpb):
        h = hg * hpb + hh
        s_hi, s_mid, s_lo = sl_ref[h, 0], sl_ref[h, 1], sl_ref[h, 2]
        sl2.append(s_hi + s_mid + s_lo)
        radix = float(OFFSET_RADIX)
        pieces = (s_hi, s_mid, s_lo, radix * s_hi, radix * s_mid, radix * s_lo)

        def bias_rows(lo):
            rows = jnp.zeros((hd2, tq), F32)
            for n, piece in enumerate(pieces):
                rows = jnp.where(brow_i == lo + n, piece, rows)
            return rows

        brows.append([bias_rows(bias_hi), bias_rows(bias_lo)])
        for blk in range(nblk):
            blk_rows = slice(blk * tk, (blk + 1) * tk)
            k_blk = k_ref[blk_rows, hh * hd2:(hh + 1) * hd2]
            kaug_scr[hh, 0, blk_rows, :] = jnp.where(klane < HEAD_DIM, k_blk, kbias[0])
            kaug_scr[hh, 1, blk_rows, :] = jnp.where(klane >= HEAD_DIM, k_blk, kbias[1])
            vt_scr[hh, blk] = v_ref[blk_rows, hh * hd2:(hh + 1) * hd2].astype(F32).T.astype(BF16)
        dmat_scr[hh] = jnp.where(visible, (-2.0 * sl2[hh]) * ahead, NEG_BIG)

    def q_block(iq, _):
        q0 = pl.multiple_of(iq * tq, tq)
        qas = []
        for hh in range(hpb):
            qt = q_ref[pl.ds(q0, tq), hh * hd2:(hh + 1) * hd2].astype(F32).T
            for comp in range(2):
                keep = (brow_i < HEAD_DIM) if comp == 0 else (brow_i >= HEAD_DIM)
                qas.append(jnp.where(keep, qt, brows[hh][comp]).astype(BF16))
        ss = [jnp.dot(kaug_scr[idx // 2, idx % 2, pl.ds(q0, tk), :], qas[idx],
                      preferred_element_type=F32) + dmat_scr[idx // 2] for idx in range(nch)]
        ps, stats = [], []
        for idx in range(nch):
            m = jnp.max(ss[idx], axis=0, keepdims=True)
            p = jnp.exp2(ss[idx] - m)
            stats.append((m, jnp.sum(p, axis=0, keepdims=True)))
            ps.append(p.astype(BF16))
        state = []
        for idx in range(nch):
            acc = jnp.dot(vt_scr[idx // 2, iq], ps[idx], preferred_element_type=F32)
            state.extend([stats[idx][0], stats[idx][1], acc])

        def kv_block(j, carry):
            k0 = pl.multiple_of(j * tk, tk)
            boff = lax.convert_element_type((j - iq) * tk, F32)
            ss = [jnp.dot(kaug_scr[idx // 2, idx % 2, pl.ds(k0, tk), :], qas[idx],
                          preferred_element_type=F32) for idx in range(nch)]
            ps, stats = [], []
            for idx in range(nch):
                m, l, _ = carry[3 * idx:3 * idx + 3]
                bc = sl2[idx // 2] * boff
                m_new = jnp.maximum(m, jnp.max(ss[idx], axis=0, keepdims=True) + bc)
                alpha = jnp.exp2(m - m_new)
                p = jnp.exp2(ss[idx] - (m_new - bc))
                stats.append((m_new, alpha * l + jnp.sum(p, axis=0, keepdims=True), alpha))
                ps.append(p.astype(BF16))
            out = []
            for idx in range(nch):
                m_new, l_new, alpha = stats[idx]
                acc = alpha * carry[3 * idx + 2] + jnp.dot(vt_scr[idx // 2, j], ps[idx],
                                                           preferred_element_type=F32)
                out.extend([m_new, l_new, acc])
            return tuple(out)

        state = lax.fori_loop(0, iq, kv_block, tuple(state))
        for hh in range(hpb):
            _, l0, a0, _, l1, a1 = state[6 * hh:6 * hh + 6]
            ot = a0 * (1.0 / l0) - a1 * (lam / l1)
            ot = ot * lax.rsqrt(jnp.mean(ot * ot, axis=0, keepdims=True) + EPS)
            o_ref[pl.ds(q0, tq), hh * hd2:(hh + 1) * hd2] = (ot.T * subg).astype(BF16)
        return 0

    lax.fori_loop(0, seq // tq, q_block, 0)


def _attention(q, k, v, lq1, lk1, lq2, lk2, subg, lam_init, batch, seq, *, tq=512, hpb=4):
    slopes = 2.0 ** (-8.0 * jnp.arange(1, N_HEADS + 1, dtype=F32) / N_HEADS)
    sl = slopes * math.log2(math.e)
    s_hi = sl.astype(BF16).astype(F32)
    s_mid = (sl - s_hi).astype(BF16).astype(F32)
    s_lo = (sl - s_hi - s_mid).astype(BF16).astype(F32)
    sl3 = jnp.stack([s_hi, s_mid, s_lo], axis=1)
    lam_arr = jnp.full((1,), lam_init, F32)
    smem = pl.BlockSpec(memory_space=pltpu.SMEM)
    hw = hpb * V_HEAD_DIM
    head_blk = pl.BlockSpec((seq, hw), lambda b, h: (b, h))
    vec = lambda n: pl.BlockSpec((1, n), lambda b, h: (0, 0))
    row = lambda a: a.astype(F32).reshape(1, -1)
    return pl.pallas_call(
        functools.partial(_attn_body, seq=seq, tq=tq, hpb=hpb),
        grid=(batch, N_HEADS // hpb),
        in_specs=[smem, smem, head_blk, head_blk, head_blk,
                  vec(HEAD_DIM), vec(HEAD_DIM), vec(HEAD_DIM), vec(HEAD_DIM), vec(V_HEAD_DIM)],
        out_specs=head_blk,
        out_shape=jax.ShapeDtypeStruct((batch * seq, D_V), BF16),
        scratch_shapes=[pltpu.VMEM((hpb, 2, seq, V_HEAD_DIM), BF16),
                        pltpu.VMEM((hpb, seq // tq, V_HEAD_DIM, tq), BF16),
                        pltpu.VMEM((hpb, tq, tq), F32)],
        compiler_params=pltpu.CompilerParams(
            dimension_semantics=("parallel", "parallel"),
            vmem_limit_bytes=VMEM_LIMIT),
        name="diff_attention",
    )(sl3, lam_arr, q, k, v, row(lq1), row(lk1), row(lq2), row(lk2), row(subg))


def _ssm_body(u_ref, bblk_ref, are_ref, aim_ref, cblk_ref, d_ref, y_ref,
              x_scr, h_scr, st_scr, *, batch, tt):
    nc = st_scr.shape[-1] // 2
    gcols = u_ref.shape[-1]

    @pl.when(pl.program_id(1) == 0)
    def _():
        st_scr[...] = jnp.zeros_like(st_scr)

    u_f = pltpu.einshape("btn->tbn", u_ref[...].astype(F32)).reshape(tt * batch, gcols)
    x_scr[...] = jnp.dot(u_f.astype(BF16), bblk_ref[...], preferred_element_type=F32)
    ar = jnp.broadcast_to(are_ref[...], (batch, nc))
    ai = jnp.broadcast_to(aim_ref[...], (batch, nc))

    def step(t, carry):
        hr, hi = carry
        r0 = pl.multiple_of(t * batch, batch)
        xr = x_scr[pl.ds(r0, batch), :nc]
        xi = x_scr[pl.ds(r0, batch), nc:]
        nr = ar * hr - ai * hi + xr
        ni = ar * hi + ai * hr + xi
        h_scr[pl.ds(r0, batch), :nc] = nr.astype(BF16)
        h_scr[pl.ds(r0, batch), nc:] = ni.astype(BF16)
        return nr, ni

    hr, hi = lax.fori_loop(0, tt, step, (st_scr[:, :nc], st_scr[:, nc:]), unroll=4)
    st_scr[:, :nc] = hr
    st_scr[:, nc:] = hi
    y = jnp.dot(h_scr[...], cblk_ref[...], preferred_element_type=F32)
    y = _gelu_tanh(y + d_ref[...] * u_f)
    y_ref[...] = pltpu.einshape("tbn->btn", y.reshape(tt, batch, gcols)).astype(BF16)


def _ssm(u3, bblk, are, aim, cblk, d_skip, *, tt=256):
    batch, seq, _ = u3.shape
    gcols = GROUPS_PER_TILE * SSM_GROUP
    ngt = N_GROUPS // GROUPS_PER_TILE
    tt = min(tt, seq)
    rows = tt * batch
    return pl.pallas_call(
        functools.partial(_ssm_body, batch=batch, tt=tt),
        grid=(ngt, seq // tt),
        in_specs=[
            pl.BlockSpec((batch, tt, gcols), lambda g, t: (0, t, g)),
            pl.BlockSpec((None, gcols, 2 * SSM_NC), lambda g, t: (g, 0, 0)),
            pl.BlockSpec((None, 1, SSM_NC), lambda g, t: (g, 0, 0)),
            pl.BlockSpec((None, 1, SSM_NC), lambda g, t: (g, 0, 0)),
            pl.BlockSpec((None, 2 * SSM_NC, gcols), lambda g, t: (g, 0, 0)),
            pl.BlockSpec((1, gcols), lambda g, t: (0, g)),
        ],
        out_specs=pl.BlockSpec((batch, tt, gcols), lambda g, t: (0, t, g)),
        out_shape=jax.ShapeDtypeStruct((batch, seq, D_SSM), BF16),
        scratch_shapes=[pltpu.VMEM((rows, 2 * SSM_NC), F32),
                        pltpu.VMEM((rows, 2 * SSM_NC), BF16),
                        pltpu.VMEM((batch, 2 * SSM_NC), F32)],
        compiler_params=pltpu.CompilerParams(
            dimension_semantics=("parallel", "arbitrary"),
            vmem_limit_bytes=VMEM_LIMIT),
        name="s5_scan",
    )(u3, bblk, are, aim, cblk, d_skip.astype(F32).reshape(1, D_SSM))


def _ssm_params(a_re, a_im, log_dt, b_re, b_im, c_re, c_im):
    gpt = GROUPS_PER_TILE
    ngt = N_GROUPS // gpt
    dt = jnp.exp(log_dt)[:, None]
    mag = jnp.exp(dt * a_re)
    ab_re = mag * jnp.cos(dt * a_im)
    ab_im = mag * jnp.sin(dt * a_im)
    den = a_re * a_re + a_im * a_im
    zr = ab_re - 1.0
    zi = ab_im
    f_re = (zr * a_re + zi * a_im) / den
    f_im = (zi * a_re - zr * a_im) / den
    bb_re = f_re[..., None] * b_re - f_im[..., None] * b_im
    bb_im = f_re[..., None] * b_im + f_im[..., None] * b_re
    eye = jnp.eye(gpt, dtype=F32)

    def bdiag_in(bb):
        t = bb.reshape(ngt, gpt, SSM_STATE, SSM_GROUP)
        return jnp.einsum('tgpi,gh->tgihp', t, eye).reshape(ngt, gpt * SSM_GROUP, SSM_NC)

    def bdiag_out(cc):
        t = cc.reshape(ngt, gpt, SSM_GROUP, SSM_STATE)
        return jnp.einsum('tgop,gh->tgpho', t, eye).reshape(ngt, SSM_NC, gpt * SSM_GROUP)

    bblk = jnp.concatenate([bdiag_in(bb_re), bdiag_in(bb_im)], axis=-1).astype(BF16)
    cblk = jnp.concatenate([bdiag_out(c_re), -bdiag_out(c_im)], axis=1).astype(BF16)
    return bblk, ab_re.reshape(ngt, 1, SSM_NC), ab_im.reshape(ngt, 1, SSM_NC), cblk


def _mix_body(y_ref, oa_ref, gate_ref, x_ref, gw_ref, gb_ref, wba_ref, wbs_ref, wo_ref, g2_ref,
              out_ref, h_ref):
    y = y_ref[...]
    z = jnp.dot(y, gw_ref[...], preferred_element_type=F32) + gb_ref[...]
    y2 = (y.astype(F32) * jax.nn.sigmoid(z)).astype(BF16)
    o_ssm = jnp.dot(y2, wbs_ref[...], preferred_element_type=F32)
    o_att = jnp.dot(oa_ref[...], wba_ref[...], preferred_element_type=F32)
    mixed = (jax.nn.sigmoid(gate_ref[:, :D_MODEL].astype(F32)) * o_att
             + jax.nn.sigmoid(gate_ref[:, D_MODEL:].astype(F32)) * o_ssm)
    x_new = x_ref[...] + jnp.dot(mixed.astype(BF16), wo_ref[...], preferred_element_type=F32)
    out_ref[...] = x_new
    h_ref[...] = _rms_rows(x_new, g2_ref[...]).astype(BF16)


def _mix(y, o_att, gates, xf, glu_w, glu_b, w_ba, w_bs, w_out, g2, batch, seq, *, tm=256):
    t = xf.shape[0]
    tm = min(tm, seq)
    full = lambda shape: pl.BlockSpec(shape, lambda i: (0, 0), pipeline_mode=pl.Buffered(1))
    return pl.pallas_call(
        _mix_body,
        grid=(t // tm,),
        in_specs=[
            pl.BlockSpec((tm, D_SSM), lambda i: (i, 0)),
            pl.BlockSpec((tm, D_V), lambda i: (i, 0)),
            pl.BlockSpec((tm, 2 * D_MODEL), lambda i: (i, 0)),
            pl.BlockSpec((tm, D_MODEL), lambda i: (i, 0)),
            full((D_SSM, D_SSM)), full((1, D_SSM)),
            full((D_V, D_MODEL)), full((D_SSM, D_MODEL)), full((D_MODEL, D_MODEL)),
            full((1, D_MODEL)),
        ],
        out_specs=[pl.BlockSpec((tm, D_MODEL), lambda i: (i, 0)),
                   pl.BlockSpec((tm, D_MODEL), lambda i: (i, 0))],
        out_shape=[jax.ShapeDtypeStruct((t, D_MODEL), F32),
                   jax.ShapeDtypeStruct((t, D_MODEL), BF16)],
        compiler_params=pltpu.CompilerParams(
            dimension_semantics=("parallel",),
            vmem_limit_bytes=VMEM_LIMIT),
        name="gated_merge",
    )(y, o_att, gates, xf, glu_w, glu_b.astype(F32).reshape(1, D_SSM), w_ba, w_bs, w_out,
      g2.astype(F32).reshape(1, D_MODEL))


def _shift_rows(up, prev, k):
    body = pltpu.roll(up, k, axis=0)
    top = pltpu.roll(jnp.concatenate([prev, up[:8]], axis=0), k, axis=0)[8:]
    return jnp.concatenate([top, body[8:]], axis=0)


def _ffn_up_body(h_ref, wa_ref, wv_ref, cwa_ref, cwv_ref, cba_ref, cbv_ref,
                 act_ref, carry_a, carry_v, *, nsb):
    i = pl.program_id(0)
    j = pl.program_id(1)
    h = h_ref[...]
    seq_start = i % nsb == 0

    def conv(w_ref, cw_ref, cb_ref, carry):
        up = jnp.dot(h, w_ref[...], preferred_element_type=F32)
        prev = jnp.where(seq_start, 0.0, carry[j])
        carry[j] = up[up.shape[0] - 8:]
        cw = cw_ref[...]
        return (cb_ref[...] + cw[0:1] * _shift_rows(up, prev, 2)
                + cw[1:2] * _shift_rows(up, prev, 1) + cw[2:3] * up)

    a = conv(wa_ref, cwa_ref, cba_ref, carry_a)
    val = conv(wv_ref, cwv_ref, cbv_ref, carry_v)
    act_ref[...] = (_gelu_tanh(a) * val).astype(BF16)


def _ffn_up(h, wa, wv, cwa, cwv, cba, cbv, seq, *, tm=1024, tn=FF_TILE):
    t = h.shape[0]
    tm = min(tm, seq)
    ncol = D_FF_PAD // tn
    col = lambda i, j: (0, j)
    return pl.pallas_call(
        functools.partial(_ffn_up_body, nsb=seq // tm),
        grid=(t // tm, ncol),
        in_specs=[
            pl.BlockSpec((tm, D_MODEL), lambda i, j: (i, 0)),
            pl.BlockSpec((D_MODEL, tn), col),
            pl.BlockSpec((D_MODEL, tn), col),
            pl.BlockSpec((CONV_W, tn), col),
            pl.BlockSpec((CONV_W, tn), col),
            pl.BlockSpec((1, tn), col),
            pl.BlockSpec((1, tn), col),
        ],
        out_specs=pl.BlockSpec((tm, tn), lambda i, j: (i, j)),
        out_shape=jax.ShapeDtypeStruct((t, D_FF_PAD), BF16),
        scratch_shapes=[pltpu.VMEM((ncol, 8, tn), F32),
                        pltpu.VMEM((ncol, 8, tn), F32)],
        compiler_params=pltpu.CompilerParams(
            dimension_semantics=("arbitrary", "arbitrary"),
            vmem_limit_bytes=VMEM_LIMIT),
        name="ffn_up_conv_gate",
    )(h, wa, wv, cwa, cwv, cba, cbv)


def _ffn_down_body(act_ref, w_ref, x_ref, out_ref):
    out_ref[...] = x_ref[...] + jnp.dot(act_ref[...], w_ref[...], preferred_element_type=F32)


def _ffn_down(act, w_down, xf, *, tm=1024, tn=512):
    t = xf.shape[0]
    tm = min(tm, t)
    return pl.pallas_call(
        _ffn_down_body,
        grid=(t // tm, D_MODEL // tn),
        in_specs=[
            pl.BlockSpec((tm, D_FF_PAD), lambda i, j: (i, 0)),
            pl.BlockSpec((D_FF_PAD, tn), lambda i, j: (0, j)),
            pl.BlockSpec((tm, tn), lambda i, j: (i, j)),
        ],
        out_specs=pl.BlockSpec((tm, tn), lambda i, j: (i, j)),
        out_shape=jax.ShapeDtypeStruct((t, D_MODEL), F32),
        compiler_params=pltpu.CompilerParams(
            dimension_semantics=("parallel", "parallel"),
            vmem_limit_bytes=VMEM_LIMIT),
        name="ffn_down",
    )(act, w_down, xf)


def _pad_ff_cols(a):
    return jnp.pad(a, ((0, 0), (0, D_FF_PAD - D_FF)))


def kernel(x, norm1_g, w_in, q_norm_g, k_norm_g, lambda_q1, lambda_k1, lambda_q2, lambda_k2, subln_g, ssm_a_re, ssm_a_im, ssm_log_dt, ssm_b_re, ssm_b_im, ssm_c_re, ssm_c_im, ssm_d, ssm_glu_w, ssm_glu_b, w_branch_attn, w_branch_ssm, w_out, norm2_g, ffn_w_up, ffn_conv_w, ffn_conv_b, ffn_w_down):
    batch, seq, _ = x.shape
    depth = w_in.shape[0]
    xf = x.reshape(batch * seq, D_MODEL)
    for l in range(depth):
        lam_init = 0.8 - 0.6 * math.exp(-0.3 * l)
        q, k, v, u, gates = _in_proj(xf, norm1_g[l], w_in[l].astype(BF16),
                                     q_norm_g[l], k_norm_g[l], batch, seq)
        o_att = _attention(q, k, v, lambda_q1[l], lambda_k1[l], lambda_q2[l], lambda_k2[l],
                           subln_g[l], lam_init, batch, seq)
        bblk, are, aim, cblk = _ssm_params(ssm_a_re[l], ssm_a_im[l], ssm_log_dt[l],
                                           ssm_b_re[l], ssm_b_im[l], ssm_c_re[l], ssm_c_im[l])
        y = _ssm(u.reshape(batch, seq, D_SSM), bblk, are, aim, cblk, ssm_d[l])
        xf, h_ffn = _mix(y.reshape(batch * seq, D_SSM), o_att, gates, xf,
                         ssm_glu_w[l].astype(BF16), ssm_glu_b[l],
                         w_branch_attn[l].astype(BF16), w_branch_ssm[l].astype(BF16),
                         w_out[l].astype(BF16), norm2_g[l], batch, seq)
        wu = ffn_w_up[l]
        cw = ffn_conv_w[l]
        cb = ffn_conv_b[l].reshape(1, 2 * D_FF)
        w_down = jnp.pad(ffn_w_down[l], ((0, D_FF_PAD - D_FF), (0, 0))).astype(BF16)
        act = _ffn_up(h_ffn,
                      _pad_ff_cols(wu[:, :D_FF]).astype(BF16), _pad_ff_cols(wu[:, D_FF:]).astype(BF16),
                      _pad_ff_cols(cw[:, :D_FF]), _pad_ff_cols(cw[:, D_FF:]),
                      _pad_ff_cols(cb[:, :D_FF]), _pad_ff_cols(cb[:, D_FF:]), seq)
        xf = _ffn_down(act, w_down, xf)
    return xf.reshape(batch, seq, D_MODEL)
```

```python
import functools
import math

import jax
import jax.numpy as jnp
from jax import lax
from jax.experimental import pallas as pl
from jax.experimental.pallas import tpu as pltpu

F32 = jnp.float32
BF16 = jnp.bfloat16

D_MODEL = 2048
N_HEADS = 8
HEAD_DIM = 64
V_HEAD_DIM = 2 * HEAD_DIM
D_QK = N_HEADS * 2 * HEAD_DIM
D_V = N_HEADS * V_HEAD_DIM
D_SSM = D_MODEL // 2
SSM_GROUP = 16
N_GROUPS = D_SSM // SSM_GROUP
SSM_STATE = 64
D_FF = 5504
CONV_W = 3
CHUNK = 64
EPS = 1e-6
IN_COLS = 2 * D_QK + D_V + D_SSM + 2 * D_MODEL

MXU_DIM = 256
FF_TILE = 512
ROW_CHUNK = 256
D_FF_PAD = ((D_FF + FF_TILE - 1) // FF_TILE) * FF_TILE
GROUPS_PER_TILE = 8
SSM_NC = GROUPS_PER_TILE * SSM_STATE
NEG_BIG = -1e30
Q_SCALE = HEAD_DIM ** -0.5 * math.log2(math.e)
VMEM_LIMIT = 56 * 1024 * 1024


def _gelu_tanh(x):
    c = math.sqrt(2.0 / math.pi)
    return x * (0.5 * (1.0 + jnp.tanh(c * (x + 0.044715 * (x * x * x)))))


def _rms_rows(xf, g):
    ms = jnp.mean(xf * xf, axis=-1, keepdims=True)
    return xf * lax.rsqrt(ms + EPS) * g


def _in_proj_body(x_ref, g_ref, w_ref, qg_ref, kg_ref, ones_ref,
                  q_ref, k_ref, v_ref, u_ref, gate_ref, h_scr, *, tn):
    j = pl.program_id(1)
    nq = D_QK // tn

    @pl.when(j == 0)
    def _():
        h_scr[...] = _rms_rows(x_ref[...], g_ref[...]).astype(BF16)

    tm = h_scr.shape[0]
    rc = min(ROW_CHUNK, tm)

    def tile(out_ref, epilogue):
        nchunk = tm // rc
        rows = lambda q: slice(q * rc, (q + 1) * rc)
        dot_q = lambda q: jnp.dot(h_scr[rows(q), :], w_ref[...], preferred_element_type=F32)
        acc = dot_q(0)
        for q in range(1, nchunk):
            nxt = dot_q(q)
            out_ref[rows(q - 1), :] = epilogue(acc).astype(BF16)
            acc = nxt
        out_ref[rows(nchunk - 1), :] = epilogue(acc).astype(BF16)

    def head_norm(gain_ref, scale):
        def epilogue(acc):
            outs = []
            for c in range(tn // MXU_DIM):
                a = acc[:, c * MXU_DIM:(c + 1) * MXU_DIM]
                ss = jnp.dot((a * a).astype(BF16), ones_ref[...], preferred_element_type=F32)
                outs.append(a * lax.rsqrt(ss * (1.0 / HEAD_DIM) + EPS) * (gain_ref[...] * scale))
            return jnp.concatenate(outs, axis=1)
        return epilogue

    plain = lambda acc: acc

    @pl.when(j < nq)
    def _():
        tile(q_ref, head_norm(qg_ref, Q_SCALE))

    @pl.when((j >= nq) & (j < 2 * nq))
    def _():
        tile(k_ref, head_norm(kg_ref, 1.0))

    @pl.when((j >= 2 * nq) & (j < 3 * nq))
    def _():
        tile(v_ref, plain)

    @pl.when((j >= 3 * nq) & (j < 4 * nq))
    def _():
        tile(u_ref, plain)

    @pl.when(j >= 4 * nq)
    def _():
        tile(gate_ref, plain)


def _in_proj(xf, g, w_bf, qg, kg, batch, seq, *, tm=1024, tn=1024):
    t = xf.shape[0]
    tm = min(tm, seq)
    nq = D_QK // tn
    ng = 2 * D_MODEL // tn
    qg_t = jnp.tile(qg.astype(F32), MXU_DIM // HEAD_DIM).reshape(1, MXU_DIM)
    kg_t = jnp.tile(kg.astype(F32), MXU_DIM // HEAD_DIM).reshape(1, MXU_DIM)
    seg = jnp.arange(MXU_DIM) // HEAD_DIM
    ones_bd = (seg[:, None] == seg[None, :]).astype(BF16)

    def cl(j, lo, n):
        return jnp.clip(j - lo, 0, n - 1)

    const = lambda i, j: (0, 0)
    return pl.pallas_call(
        functools.partial(_in_proj_body, tn=tn),
        grid=(t // tm, IN_COLS // tn),
        in_specs=[
            pl.BlockSpec((tm, D_MODEL), lambda i, j: (i, 0)),
            pl.BlockSpec((1, D_MODEL), const),
            pl.BlockSpec((D_MODEL, tn), lambda i, j: (0, j)),
            pl.BlockSpec((1, MXU_DIM), const),
            pl.BlockSpec((1, MXU_DIM), const),
            pl.BlockSpec((MXU_DIM, MXU_DIM), const),
        ],
        out_specs=[
            pl.BlockSpec((tm, tn), lambda i, j: (i, cl(j, 0, nq))),
            pl.BlockSpec((tm, tn), lambda i, j: (i, cl(j, nq, nq))),
            pl.BlockSpec((tm, tn), lambda i, j: (i, cl(j, 2 * nq, nq))),
            pl.BlockSpec((tm, tn), lambda i, j: (i, cl(j, 3 * nq, nq))),
            pl.BlockSpec((tm, tn), lambda i, j: (i, cl(j, 4 * nq, ng))),
        ],
        out_shape=[
            jax.ShapeDtypeStruct((t, D_QK), BF16),
            jax.ShapeDtypeStruct((t, D_QK), BF16),
            jax.ShapeDtypeStruct((t, D_V), BF16),
            jax.ShapeDtypeStruct((t, D_SSM), BF16),
            jax.ShapeDtypeStruct((t, 2 * D_MODEL), BF16),
        ],
        scratch_shapes=[pltpu.VMEM((tm, D_MODEL), BF16)],
        compiler_params=pltpu.CompilerParams(
            dimension_semantics=("arbitrary", "arbitrary"),
            vmem_limit_bytes=VMEM_LIMIT),
        name="in_proj",
    )(xf, g.reshape(1, D_MODEL), w_bf, qg_t, kg_t, ones_bd)


N_BIAS_ROWS = 3
OFFSET_RADIX = 256
BIAS_LANE0 = 0


def _attn_body(sl_ref, laminit_ref, q_ref, k_ref, v_ref, lq1_ref, lk1_ref, lq2_ref, lk2_ref,
               subg_ref, o_ref, kaug_scr, vt_scr, dmat_scr, *, seq, tq, hpb):
    tk = tq
    nblk = seq // tk
    hd2 = 2 * HEAD_DIM
    nch = 2 * hpb
    hg = pl.program_id(1)
    lam_init = laminit_ref[0]
    lam = (jnp.exp(jnp.sum(lq1_ref[...] * lk1_ref[...], axis=-1, keepdims=True))
           - jnp.exp(jnp.sum(lq2_ref[...] * lk2_ref[...], axis=-1, keepdims=True))
           + lam_init)
    subg = subg_ref[...] * (1.0 - lam_init)

    koff = lax.broadcasted_iota(jnp.int32, (tk, hd2), 0)
    klane = lax.broadcasted_iota(jnp.int32, (tk, hd2), 1)
    bias_lo = BIAS_LANE0
    bias_hi = HEAD_DIM + BIAS_LANE0
    kbias = [jnp.where((klane >= lo) & (klane < lo + N_BIAS_ROWS), koff % OFFSET_RADIX,
                       jnp.where((klane >= lo + N_BIAS_ROWS) & (klane < lo + 2 * N_BIAS_ROWS),
                                 koff // OFFSET_RADIX, 0)).astype(F32).astype(BF16)
             for lo in (bias_hi, bias_lo)]
    c = lax.broadcasted_iota(jnp.int32, (tk, tq), 0)
    r = lax.broadcasted_iota(jnp.int32, (tk, tq), 1)
    visible = (c // CHUNK) <= (r // CHUNK)
    ahead = jnp.maximum(c - r, 0).astype(F32)
    brow_i = lax.broadcasted_iota(jnp.int32, (hd2, tq), 0)
    sl2 = []
    brows = []
    for hh in range(hpb):
        h = hg * hpb + hh
        s_hi, s_mid, s_lo = sl_ref[h, 0], sl_ref[h, 1], sl_ref[h, 2]
        sl2.append(s_hi + s_mid + s_lo)
        radix = float(OFFSET_RADIX)
        pieces = (s_hi, s_mid, s_lo, radix * s_hi, radix * s_mid, radix * s_lo)

        def bias_rows(lo):
            rows = jnp.zeros((hd2, tq), F32)
            for n, piece in enumerate(pieces):
                rows = jnp.where(brow_i == lo + n, piece, rows)
            return rows

        brows.append([bias_rows(bias_hi), bias_rows(bias_lo)])
        for blk in range(nblk):
            blk_rows = slice(blk * tk, (blk + 1) * tk)
            k_blk = k_ref[blk_rows, hh * hd2:(hh + 1) * hd2]
            kaug_scr[hh, 0, blk_rows, :] = jnp.where(klane < HEAD_DIM, k_blk, kbias[0])
            kaug_scr[hh, 1, blk_rows, :] = jnp.where(klane >= HEAD_DIM, k_blk, kbias[1])
            vt_scr[hh, blk] = v_ref[blk_rows, hh * hd2:(hh + 1) * hd2].astype(F32).T.astype(BF16)
        dmat_scr[hh] = jnp.where(visible, (-2.0 * sl2[hh]) * ahead, NEG_BIG)

    def q_block(iq, _):
        q0 = pl.multiple_of(iq * tq, tq)
        qas = []
        for hh in range(hpb):
            qt = q_ref[pl.ds(q0, tq), hh * hd2:(hh + 1) * hd2].astype(F32).T
            for comp in range(2):
                keep = (brow_i < HEAD_DIM) if comp == 0 else (brow_i >= HEAD_DIM)
                qas.append(jnp.where(keep, qt, brows[hh][comp]).astype(BF16))
        hq = tq // 2
        ss_a = [jnp.dot(kaug_scr[idx // 2, idx % 2, pl.ds(q0, hq), :], qas[idx][:, :hq],
                        preferred_element_type=F32) + dmat_scr[idx // 2, :hq, :hq]
                for idx in range(nch)]
        ss_b = [jnp.dot(kaug_scr[idx // 2, idx % 2, pl.ds(q0, tk), :], qas[idx][:, hq:],
                        preferred_element_type=F32) + dmat_scr[idx // 2, :, hq:]
                for idx in range(nch)]
        ps, stats = [], []
        for idx in range(nch):
            m_a = jnp.max(ss_a[idx], axis=0, keepdims=True)
            m_b = jnp.max(ss_b[idx], axis=0, keepdims=True)
            p_a = jnp.exp2(ss_a[idx] - m_a)
            p_b = jnp.exp2(ss_b[idx] - m_b)
            stats.append((jnp.concatenate([m_a, m_b], axis=1),
                          jnp.concatenate([jnp.sum(p_a, axis=0, keepdims=True),
                                           jnp.sum(p_b, axis=0, keepdims=True)], axis=1)))
            ps.append((p_a.astype(BF16), p_b.astype(BF16)))
        state = []
        for idx in range(nch):
            vt = vt_scr[idx // 2, iq]
            acc = jnp.concatenate(
                [jnp.dot(vt[:, :hq], ps[idx][0], preferred_element_type=F32),
                 jnp.dot(vt, ps[idx][1], preferred_element_type=F32)], axis=1)
            state.extend([stats[idx][0], stats[idx][1], acc])

        def kv_block(j, carry):
            k0 = pl.multiple_of(j * tk, tk)
            boff = lax.convert_element_type((j - iq) * tk, F32)
            ss = [jnp.dot(kaug_scr[idx // 2, idx % 2, pl.ds(k0, tk), :], qas[idx],
                          preferred_element_type=F32) for idx in range(nch)]
            ps, stats = [], []
            for idx in range(nch):
                m, l, _ = carry[3 * idx:3 * idx + 3]
                bc = sl2[idx // 2] * boff
                m_new = jnp.maximum(m, jnp.max(ss[idx], axis=0, keepdims=True) + bc)
                alpha = jnp.exp2(m - m_new)
                p = jnp.exp2(ss[idx] - (m_new - bc))
                stats.append((m_new, alpha * l + jnp.sum(p, axis=0, keepdims=True), alpha))
                ps.append(p.astype(BF16))
            out = []
            for idx in range(nch):
                m_new, l_new, alpha = stats[idx]
                acc = alpha * carry[3 * idx + 2] + jnp.dot(vt_scr[idx // 2, j], ps[idx],
                                                           preferred_element_type=F32)
                out.extend([m_new, l_new, acc])
            return tuple(out)

        state = lax.fori_loop(0, iq, kv_block, tuple(state))
        for hh in range(hpb):
            _, l0, a0, _, l1, a1 = state[6 * hh:6 * hh + 6]
            ot = a0 * (1.0 / l0) - a1 * (lam / l1)
            ot = ot * lax.rsqrt(jnp.mean(ot * ot, axis=0, keepdims=True) + EPS)
            o_ref[pl.ds(q0, tq), hh * hd2:(hh + 1) * hd2] = (ot.T * subg).astype(BF16)
        return 0

    lax.fori_loop(0, seq // tq, q_block, 0)


def _attention(q, k, v, lq1, lk1, lq2, lk2, subg, lam_init, batch, seq, *, tq=512, hpb=4):
    slopes = 2.0 ** (-8.0 * jnp.arange(1, N_HEADS + 1, dtype=F32) / N_HEADS)
    sl = slopes * math.log2(math.e)
    s_hi = sl.astype(BF16).astype(F32)
    s_mid = (sl - s_hi).astype(BF16).astype(F32)
    s_lo = (sl - s_hi - s_mid).astype(BF16).astype(F32)
    sl3 = jnp.stack([s_hi, s_mid, s_lo], axis=1)
    lam_arr = jnp.full((1,), lam_init, F32)
    smem = pl.BlockSpec(memory_space=pltpu.SMEM)
    hw = hpb * V_HEAD_DIM
    head_blk = pl.BlockSpec((seq, hw), lambda b, h: (b, h))
    vec = lambda n: pl.BlockSpec((1, n), lambda b, h: (0, 0))
    row = lambda a: a.astype(F32).reshape(1, -1)
    return pl.pallas_call(
        functools.partial(_attn_body, seq=seq, tq=tq, hpb=hpb),
        grid=(batch, N_HEADS // hpb),
        in_specs=[smem, smem, head_blk, head_blk, head_blk,
                  vec(HEAD_DIM), vec(HEAD_DIM), vec(HEAD_DIM), vec(HEAD_DIM), vec(V_HEAD_DIM)],
        out_specs=head_blk,
        out_shape=jax.ShapeDtypeStruct((batch * seq, D_V), BF16),
        scratch_shapes=[pltpu.VMEM((hpb, 2, seq, V_HEAD_DIM), BF16),
                        pltpu.VMEM((hpb, seq // tq, V_HEAD_DIM, tq), BF16),
                        pltpu.VMEM((hpb, tq, tq), F32)],
        compiler_params=pltpu.CompilerParams(
            dimension_semantics=("parallel", "parallel"),
            vmem_limit_bytes=VMEM_LIMIT),
        name="diff_attention",
    )(sl3, lam_arr, q, k, v, row(lq1), row(lk1), row(lq2), row(lk2), row(subg))


def _ssm_body(u_ref, bblk_ref, are_ref, aim_ref, cblk_ref, d_ref, y_ref,
              x_scr, h_scr, st_scr, *, batch, tt):
    nc = st_scr.shape[-1] // 2
    gcols = u_ref.shape[-1]

    @pl.when(pl.program_id(1) == 0)
    def _():
        st_scr[...] = jnp.zeros_like(st_scr)

    u_f = pltpu.einshape("btn->tbn", u_ref[...].astype(F32)).reshape(tt * batch, gcols)
    x_scr[...] = jnp.dot(u_f.astype(BF16), bblk_ref[...], preferred_element_type=F32)
    ar = jnp.broadcast_to(are_ref[...], (batch, nc))
    ai = jnp.broadcast_to(aim_ref[...], (batch, nc))

    def step(t, carry):
        hr, hi = carry
        r0 = pl.multiple_of(t * batch, batch)
        xr = x_scr[pl.ds(r0, batch), :nc]
        xi = x_scr[pl.ds(r0, batch), nc:]
        nr = ar * hr - ai * hi + xr
        ni = ar * hi + ai * hr + xi
        h_scr[pl.ds(r0, batch), :nc] = nr.astype(BF16)
        h_scr[pl.ds(r0, batch), nc:] = ni.astype(BF16)
        return nr, ni

    hr, hi = lax.fori_loop(0, tt, step, (st_scr[:, :nc], st_scr[:, nc:]), unroll=4)
    st_scr[:, :nc] = hr
    st_scr[:, nc:] = hi
    y = jnp.dot(h_scr[...], cblk_ref[...], preferred_element_type=F32)
    y = _gelu_tanh(y + d_ref[...] * u_f)
    y_ref[...] = pltpu.einshape("tbn->btn", y.reshape(tt, batch, gcols)).astype(BF16)


def _ssm(u3, bblk, are, aim, cblk, d_skip, *, tt=256):
    batch, seq, _ = u3.shape
    gcols = GROUPS_PER_TILE * SSM_GROUP
    ngt = N_GROUPS // GROUPS_PER_TILE
    tt = min(tt, seq)
    rows = tt * batch
    return pl.pallas_call(
        functools.partial(_ssm_body, batch=batch, tt=tt),
        grid=(ngt, seq // tt),
        in_specs=[
            pl.BlockSpec((batch, tt, gcols), lambda g, t: (0, t, g)),
            pl.BlockSpec((None, gcols, 2 * SSM_NC), lambda g, t: (g, 0, 0)),
            pl.BlockSpec((None, 1, SSM_NC), lambda g, t: (g, 0, 0)),
            pl.BlockSpec((None, 1, SSM_NC), lambda g, t: (g, 0, 0)),
            pl.BlockSpec((None, 2 * SSM_NC, gcols), lambda g, t: (g, 0, 0)),
            pl.BlockSpec((1, gcols), lambda g, t: (0, g)),
        ],
        out_specs=pl.BlockSpec((batch, tt, gcols), lambda g, t: (0, t, g)),
        out_shape=jax.ShapeDtypeStruct((batch, seq, D_SSM), BF16),
        scratch_shapes=[pltpu.VMEM((rows, 2 * SSM_NC), F32),
                        pltpu.VMEM((rows, 2 * SSM_NC), BF16),
                        pltpu.VMEM((batch, 2 * SSM_NC), F32)],
        compiler_params=pltpu.CompilerParams(
            dimension_semantics=("parallel", "arbitrary"),
            vmem_limit_bytes=VMEM_LIMIT),
        name="s5_scan",
    )(u3, bblk, are, aim, cblk, d_skip.astype(F32).reshape(1, D_SSM))


def _ssm_params(a_re, a_im, log_dt, b_re, b_im, c_re, c_im):
    gpt = GROUPS_PER_TILE
    ngt = N_GROUPS // gpt
    dt = jnp.exp(log_dt)[:, None]
    mag = jnp.exp(dt * a_re)
    ab_re = mag * jnp.cos(dt * a_im)
    ab_im = mag * jnp.sin(dt * a_im)
    den = a_re * a_re + a_im * a_im
    zr = ab_re - 1.0
    zi = ab_im
    f_re = (zr * a_re + zi * a_im) / den
    f_im = (zi * a_re - zr * a_im) / den
    bb_re = f_re[..., None] * b_re - f_im[..., None] * b_im
    bb_im = f_re[..., None] * b_im + f_im[..., None] * b_re
    eye = jnp.eye(gpt, dtype=F32)

    def bdiag_in(bb):
        t = bb.reshape(ngt, gpt, SSM_STATE, SSM_GROUP)
        return jnp.einsum('tgpi,gh->tgihp', t, eye).reshape(ngt, gpt * SSM_GROUP, SSM_NC)

    def bdiag_out(cc):
        t = cc.reshape(ngt, gpt, SSM_GROUP, SSM_STATE)
        return jnp.einsum('tgop,gh->tgpho', t, eye).reshape(ngt, SSM_NC, gpt * SSM_GROUP)

    bblk = jnp.concatenate([bdiag_in(bb_re), bdiag_in(bb_im)], axis=-1).astype(BF16)
    cblk = jnp.concatenate([bdiag_out(c_re), -bdiag_out(c_im)], axis=1).astype(BF16)
    return bblk, ab_re.reshape(ngt, 1, SSM_NC), ab_im.reshape(ngt, 1, SSM_NC), cblk


def _mix_body(y_ref, oa_ref, gate_ref, x_ref, gw_ref, gb_ref, wba_ref, wbs_ref, wo_ref, g2_ref,
              out_ref, h_ref):
    y = y_ref[...]
    z = jnp.dot(y, gw_ref[...], preferred_element_type=F32) + gb_ref[...]
    y2 = (y.astype(F32) * jax.nn.sigmoid(z)).astype(BF16)
    o_ssm = jnp.dot(y2, wbs_ref[...], preferred_element_type=F32)
    o_att = jnp.dot(oa_ref[...], wba_ref[...], preferred_element_type=F32)
    mixed = (jax.nn.sigmoid(gate_ref[:, :D_MODEL].astype(F32)) * o_att
             + jax.nn.sigmoid(gate_ref[:, D_MODEL:].astype(F32)) * o_ssm)
    x_new = x_ref[...] + jnp.dot(mixed.astype(BF16), wo_ref[...], preferred_element_type=F32)
    out_ref[...] = x_new
    h_ref[...] = _rms_rows(x_new, g2_ref[...]).astype(BF16)


def _mix(y, o_att, gates, xf, glu_w, glu_b, w_ba, w_bs, w_out, g2, batch, seq, *, tm=256):
    t = xf.shape[0]
    tm = min(tm, seq)
    full = lambda shape: pl.BlockSpec(shape, lambda i: (0, 0), pipeline_mode=pl.Buffered(1))
    return pl.pallas_call(
        _mix_body,
        grid=(t // tm,),
        in_specs=[
            pl.BlockSpec((tm, D_SSM), lambda i: (i, 0)),
            pl.BlockSpec((tm, D_V), lambda i: (i, 0)),
            pl.BlockSpec((tm, 2 * D_MODEL), lambda i: (i, 0)),
            pl.BlockSpec((tm, D_MODEL), lambda i: (i, 0)),
            full((D_SSM, D_SSM)), full((1, D_SSM)),
            full((D_V, D_MODEL)), full((D_SSM, D_MODEL)), full((D_MODEL, D_MODEL)),
            full((1, D_MODEL)),
        ],
        out_specs=[pl.BlockSpec((tm, D_MODEL), lambda i: (i, 0)),
                   pl.BlockSpec((tm, D_MODEL), lambda i: (i, 0))],
        out_shape=[jax.ShapeDtypeStruct((t, D_MODEL), F32),
                   jax.ShapeDtypeStruct((t, D_MODEL), BF16)],
        compiler_params=pltpu.CompilerParams(
            dimension_semantics=("parallel",),
            vmem_limit_bytes=VMEM_LIMIT),
        name="gated_merge",
    )(y, o_att, gates, xf, glu_w, glu_b.astype(F32).reshape(1, D_SSM), w_ba, w_bs, w_out,
      g2.astype(F32).reshape(1, D_MODEL))


def _shift_rows(up, prev, k):
    body = pltpu.roll(up, k, axis=0)
    top = pltpu.roll(jnp.concatenate([prev, up[:8]], axis=0), k, axis=0)[8:]
    return jnp.concatenate([top, body[8:]], axis=0)


def _ffn_up_body(h_ref, wa_ref, wv_ref, cwa_ref, cwv_ref, cba_ref, cbv_ref,
                 act_ref, carry_a, carry_v, *, nsb):
    i = pl.program_id(0)
    j = pl.program_id(1)
    h = h_ref[...]

    @pl.when((i == 0) & (j == 0))
    def _():
        carry_a[...] = jnp.zeros_like(carry_a)
        carry_v[...] = jnp.zeros_like(carry_v)
    seq_start = i % nsb == 0

    def conv(w_ref, cw_ref, cb_ref, carry):
        up = jnp.dot(h, w_ref[...], preferred_element_type=F32)
        prev = jnp.where(seq_start, 0.0, carry[j])
        carry[j] = up[up.shape[0] - 8:]
        cw = cw_ref[...]
        return (cb_ref[...] + cw[0:1] * _shift_rows(up, prev, 2)
                + cw[1:2] * _shift_rows(up, prev, 1) + cw[2:3] * up)

    a = conv(wa_ref, cwa_ref, cba_ref, carry_a)
    val = conv(wv_ref, cwv_ref, cbv_ref, carry_v)
    act_ref[...] = (_gelu_tanh(a) * val).astype(BF16)


def _ffn_up(h, wa, wv, cwa, cwv, cba, cbv, seq, *, tm=1024, tn=FF_TILE):
    t = h.shape[0]
    tm = min(tm, seq)
    ncol = D_FF_PAD // tn
    col = lambda i, j: (0, j)
    return pl.pallas_call(
        functools.partial(_ffn_up_body, nsb=seq // tm),
        grid=(t // tm, ncol),
        in_specs=[
            pl.BlockSpec((tm, D_MODEL), lambda i, j: (i, 0)),
            pl.BlockSpec((D_MODEL, tn), col),
            pl.BlockSpec((D_MODEL, tn), col),
            pl.BlockSpec((CONV_W, tn), col),
            pl.BlockSpec((CONV_W, tn), col),
            pl.BlockSpec((1, tn), col),
            pl.BlockSpec((1, tn), col),
        ],
        out_specs=pl.BlockSpec((tm, tn), lambda i, j: (i, j)),
        out_shape=jax.ShapeDtypeStruct((t, D_FF_PAD), BF16),
        scratch_shapes=[pltpu.VMEM((ncol, 8, tn), F32),
                        pltpu.VMEM((ncol, 8, tn), F32)],
        compiler_params=pltpu.CompilerParams(
            dimension_semantics=("arbitrary", "arbitrary"),
            vmem_limit_bytes=VMEM_LIMIT),
        name="ffn_up_conv_gate",
    )(h, wa, wv, cwa, cwv, cba, cbv)


def _ffn_down_body(act_ref, w_ref, x_ref, out_ref):
    out_ref[...] = x_ref[...] + jnp.dot(act_ref[...], w_ref[...], preferred_element_type=F32)


def _ffn_down(act, w_down, xf, *, tm=1024, tn=512):
    t = xf.shape[0]
    tm = min(tm, t)
    return pl.pallas_call(
        _ffn_down_body,
        grid=(t // tm, D_MODEL // tn),
        in_specs=[
            pl.BlockSpec((tm, D_FF_PAD), lambda i, j: (i, 0)),
            pl.BlockSpec((D_FF_PAD, tn), lambda i, j: (0, j)),
            pl.BlockSpec((tm, tn), lambda i, j: (i, j)),
        ],
        out_specs=pl.BlockSpec((tm, tn), lambda i, j: (i, j)),
        out_shape=jax.ShapeDtypeStruct((t, D_MODEL), F32),
        compiler_params=pltpu.CompilerParams(
            dimension_semantics=("parallel", "parallel"),
            vmem_limit_bytes=VMEM_LIMIT),
        name="ffn_down",
    )(act, w_down, xf)


def _pad_ff_cols(a):
    return jnp.pad(a, ((0, 0), (0, D_FF_PAD - D_FF)))


def kernel(x, norm1_g, w_in, q_norm_g, k_norm_g, lambda_q1, lambda_k1, lambda_q2, lambda_k2, subln_g, ssm_a_re, ssm_a_im, ssm_log_dt, ssm_b_re, ssm_b_im, ssm_c_re, ssm_c_im, ssm_d, ssm_glu_w, ssm_glu_b, w_branch_attn, w_branch_ssm, w_out, norm2_g, ffn_w_up, ffn_conv_w, ffn_conv_b, ffn_w_down):
    batch, seq, _ = x.shape
    depth = w_in.shape[0]
    xf = x.reshape(batch * seq, D_MODEL)
    for l in range(depth):
        lam_init = 0.8 - 0.6 * math.exp(-0.3 * l)
        q, k, v, u, gates = _in_proj(xf, norm1_g[l], w_in[l].astype(BF16),
                                     q_norm_g[l], k_norm_g[l], batch, seq)
        o_att = _attention(q, k, v, lambda_q1[l], lambda_k1[l], lambda_q2[l], lambda_k2[l],
                           subln_g[l], lam_init, batch, seq)
        bblk, are, aim, cblk = _ssm_params(ssm_a_re[l], ssm_a_im[l], ssm_log_dt[l],
                                           ssm_b_re[l], ssm_b_im[l], ssm_c_re[l], ssm_c_im[l])
        y = _ssm(u.reshape(batch, seq, D_SSM), bblk, are, aim, cblk, ssm_d[l])
        xf, h_ffn = _mix(y.reshape(batch * seq, D_SSM), o_att, gates, xf,
                         ssm_glu_w[l].astype(BF16), ssm_glu_b[l],
                         w_branch_attn[l].astype(BF16), w_branch_ssm[l].astype(BF16),
                         w_out[l].astype(BF16), norm2_g[l], batch, seq)
        wu = ffn_w_up[l]
        cw = ffn_conv_w[l]
        cb = ffn_conv_b[l].reshape(1, 2 * D_FF)
        w_down = jnp.pad(ffn_w_down[l].astype(BF16), ((0, D_FF_PAD - D_FF), (0, 0)))
        act = _ffn_up(h_ffn,
                      _pad_ff_cols(wu[:, :D_FF].astype(BF16)), _pad_ff_cols(wu[:, D_FF:].astype(BF16)),
                      _pad_ff_cols(cw[:, :D_FF]), _pad_ff_cols(cw[:, D_FF:]),
                      _pad_ff_cols(cb[:, :D_FF]), _pad_ff_cols(cb[:, D_FF:]), seq)
        xf = _ffn_down(act, w_down, xf)
    return xf.reshape(batch, seq, D_MODEL)
```

```python
import functools
import math

import jax
import jax.numpy as jnp
from jax import lax
from jax.experimental import pallas as pl
from jax.experimental.pallas import tpu as pltpu

F32 = jnp.float32
BF16 = jnp.bfloat16

D_MODEL = 2048
N_HEADS = 8
HEAD_DIM = 64
V_HEAD_DIM = 2 * HEAD_DIM
D_QK = N_HEADS * 2 * HEAD_DIM
D_V = N_HEADS * V_HEAD_DIM
D_SSM = D_MODEL // 2
SSM_GROUP = 16
N_GROUPS = D_SSM // SSM_GROUP
SSM_STATE = 64
D_FF = 5504
CONV_W = 3
CHUNK = 64
EPS = 1e-6
IN_COLS = 2 * D_QK + D_V + D_SSM + 2 * D_MODEL

MXU_DIM = 256
FF_TILE = 512
ROW_CHUNK = 256
D_FF_PAD = ((D_FF + FF_TILE - 1) // FF_TILE) * FF_TILE
GROUPS_PER_TILE = 8
SSM_NC = GROUPS_PER_TILE * SSM_STATE
NEG_BIG = -1e30
Q_SCALE = HEAD_DIM ** -0.5 * math.log2(math.e)
VMEM_LIMIT = 56 * 1024 * 1024


def _gelu_tanh(x):
    c = math.sqrt(2.0 / math.pi)
    return x * (0.5 * (1.0 + jnp.tanh(c * (x + 0.044715 * (x * x * x)))))


def _rms_rows(xf, g):
    ms = jnp.mean(xf * xf, axis=-1, keepdims=True)
    return xf * lax.rsqrt(ms + EPS) * g


def _in_proj_body(x_ref, g_ref, w_ref, qg_ref, kg_ref, ones_ref,
                  q_ref, k_ref, v_ref, u_ref, gate_ref, h_scr, *, tn):
    j = pl.program_id(1)
    nq = D_QK // tn

    @pl.when(j == 0)
    def _():
        h_scr[...] = _rms_rows(x_ref[...], g_ref[...]).astype(BF16)

    tm = h_scr.shape[0]
    rc = min(ROW_CHUNK, tm)

    def tile(out_ref, epilogue):
        nchunk = tm // rc
        rows = lambda q: slice(q * rc, (q + 1) * rc)
        dot_q = lambda q: jnp.dot(h_scr[rows(q), :], w_ref[...], preferred_element_type=F32)
        acc = dot_q(0)
        for q in range(1, nchunk):
            nxt = dot_q(q)
            out_ref[rows(q - 1), :] = epilogue(acc).astype(BF16)
            acc = nxt
        out_ref[rows(nchunk - 1), :] = epilogue(acc).astype(BF16)

    def head_norm(gain_ref, scale):
        def epilogue(acc):
            outs = []
            for c in range(tn // MXU_DIM):
                a = acc[:, c * MXU_DIM:(c + 1) * MXU_DIM]
                ss = jnp.dot((a * a).astype(BF16), ones_ref[...], preferred_element_type=F32)
                outs.append(a * lax.rsqrt(ss * (1.0 / HEAD_DIM) + EPS) * (gain_ref[...] * scale))
            return jnp.concatenate(outs, axis=1)
        return epilogue

    plain = lambda acc: acc

    @pl.when(j < nq)
    def _():
        tile(q_ref, head_norm(qg_ref, Q_SCALE))

    @pl.when((j >= nq) & (j < 2 * nq))
    def _():
        tile(k_ref, head_norm(kg_ref, 1.0))

    @pl.when((j >= 2 * nq) & (j < 3 * nq))
    def _():
        tile(v_ref, plain)

    @pl.when((j >= 3 * nq) & (j < 4 * nq))
    def _():
        tile(u_ref, plain)

    @pl.when(j >= 4 * nq)
    def _():
        tile(gate_ref, plain)


def _in_proj(xf, g, w_bf, qg, kg, batch, seq, *, tm=1024, tn=1024):
    t = xf.shape[0]
    tm = min(tm, seq)
    nq = D_QK // tn
    ng = 2 * D_MODEL // tn
    qg_t = jnp.tile(qg.astype(F32), MXU_DIM // HEAD_DIM).reshape(1, MXU_DIM)
    kg_t = jnp.tile(kg.astype(F32), MXU_DIM // HEAD_DIM).reshape(1, MXU_DIM)
    seg = jnp.arange(MXU_DIM) // HEAD_DIM
    ones_bd = (seg[:, None] == seg[None, :]).astype(BF16)

    def cl(j, lo, n):
        return jnp.clip(j - lo, 0, n - 1)

    const = lambda i, j: (0, 0)
    return pl.pallas_call(
        functools.partial(_in_proj_body, tn=tn),
        grid=(t // tm, IN_COLS // tn),
        in_specs=[
            pl.BlockSpec((tm, D_MODEL), lambda i, j: (i, 0)),
            pl.BlockSpec((1, D_MODEL), const),
            pl.BlockSpec((D_MODEL, tn), lambda i, j: (0, j)),
            pl.BlockSpec((1, MXU_DIM), const),
            pl.BlockSpec((1, MXU_DIM), const),
            pl.BlockSpec((MXU_DIM, MXU_DIM), const),
        ],
        out_specs=[
            pl.BlockSpec((tm, tn), lambda i, j: (i, cl(j, 0, nq))),
            pl.BlockSpec((tm, tn), lambda i, j: (i, cl(j, nq, nq))),
            pl.BlockSpec((tm, tn), lambda i, j: (i, cl(j, 2 * nq, nq))),
            pl.BlockSpec((tm, tn), lambda i, j: (i, cl(j, 3 * nq, nq))),
            pl.BlockSpec((tm, tn), lambda i, j: (i, cl(j, 4 * nq, ng))),
        ],
        out_shape=[
            jax.ShapeDtypeStruct((t, D_QK), BF16),
            jax.ShapeDtypeStruct((t, D_QK), BF16),
            jax.ShapeDtypeStruct((t, D_V), BF16),
            jax.ShapeDtypeStruct((t, D_SSM), BF16),
            jax.ShapeDtypeStruct((t, 2 * D_MODEL), BF16),
        ],
        scratch_shapes=[pltpu.VMEM((tm, D_MODEL), BF16)],
        compiler_params=pltpu.CompilerParams(
            dimension_semantics=("arbitrary", "arbitrary"),
            vmem_limit_bytes=VMEM_LIMIT),
        name="in_proj",
    )(xf, g.reshape(1, D_MODEL), w_bf, qg_t, kg_t, ones_bd)


N_BIAS_ROWS = 3
OFFSET_RADIX = 256
BIAS_LANE0 = 0


def _attn_body(sl_ref, laminit_ref, q_ref, k_ref, v_ref, lq1_ref, lk1_ref, lq2_ref, lk2_ref,
               subg_ref, o_ref, kaug_scr, vt_scr, dmat_scr, *, seq, tq, hpb):
    tk = tq
    nblk = seq // tk
    hd2 = 2 * HEAD_DIM
    nch = 2 * hpb
    hg = pl.program_id(1)
    lam_init = laminit_ref[0]
    lam = (jnp.exp(jnp.sum(lq1_ref[...] * lk1_ref[...], axis=-1, keepdims=True))
           - jnp.exp(jnp.sum(lq2_ref[...] * lk2_ref[...], axis=-1, keepdims=True))
           + lam_init)
    subg = subg_ref[...] * (1.0 - lam_init)

    koff = lax.broadcasted_iota(jnp.int32, (tk, hd2), 0)
    klane = lax.broadcasted_iota(jnp.int32, (tk, hd2), 1)
    bias_lo = BIAS_LANE0
    bias_hi = HEAD_DIM + BIAS_LANE0
    kbias = [jnp.where((klane >= lo) & (klane < lo + N_BIAS_ROWS), koff % OFFSET_RADIX,
                       jnp.where((klane >= lo + N_BIAS_ROWS) & (klane < lo + 2 * N_BIAS_ROWS),
                                 koff // OFFSET_RADIX, 0)).astype(F32).astype(BF16)
             for lo in (bias_hi, bias_lo)]
    c = lax.broadcasted_iota(jnp.int32, (tk, tq), 0)
    r = lax.broadcasted_iota(jnp.int32, (tk, tq), 1)
    visible = (c // CHUNK) <= (r // CHUNK)
    ahead = jnp.maximum(c - r, 0).astype(F32)
    brow_i = lax.broadcasted_iota(jnp.int32, (hd2, tq), 0)
    sl2 = []
    brows = []
    for hh in range(hpb):
        h = hg * hpb + hh
        s_hi, s_mid, s_lo = sl_ref[h, 0], sl_ref[h, 1], sl_ref[h, 2]
        sl2.append(s_hi + s_mid + s_lo)
        radix = float(OFFSET_RADIX)
        pieces = (s_hi, s_mid, s_lo, radix * s_hi, radix * s_mid, radix * s_lo)

        def bias_rows(lo):
            rows = jnp.zeros((hd2, tq), F32)
            for n, piece in enumerate(pieces):
                rows = jnp.where(brow_i == lo + n, piece, rows)
            return rows

        brows.append([bias_rows(bias_hi), bias_rows(bias_lo)])
        for blk in range(nblk):
            blk_rows = slice(blk * tk, (blk + 1) * tk)
            k_blk = k_ref[blk_rows, hh * hd2:(hh + 1) * hd2]
            kaug_scr[hh, 0, blk_rows, :] = jnp.where(klane < HEAD_DIM, k_blk, kbias[0])
            kaug_scr[hh, 1, blk_rows, :] = jnp.where(klane >= HEAD_DIM, k_blk, kbias[1])
            vt_scr[hh, blk] = v_ref[blk_rows, hh * hd2:(hh + 1) * hd2].astype(F32).T.astype(BF16)
        dmat_scr[hh] = jnp.where(visible, (-2.0 * sl2[hh]) * ahead, NEG_BIG)

    def q_block(iq, _):
        q0 = pl.multiple_of(iq * tq, tq)
        qas = []
        for hh in range(hpb):
            qt = q_ref[pl.ds(q0, tq), hh * hd2:(hh + 1) * hd2].astype(F32).T
            for comp in range(2):
                keep = (brow_i < HEAD_DIM) if comp == 0 else (brow_i >= HEAD_DIM)
                qas.append(jnp.where(keep, qt, brows[hh][comp]).astype(BF16))
        hq = tq // 2
        ss_a = [jnp.dot(kaug_scr[idx // 2, idx % 2, pl.ds(q0, hq), :], qas[idx][:, :hq],
                        preferred_element_type=F32) + dmat_scr[idx // 2, :hq, :hq]
                for idx in range(nch)]
        ss_b = [jnp.dot(kaug_scr[idx // 2, idx % 2, pl.ds(q0, tk), :], qas[idx][:, hq:],
                        preferred_element_type=F32) + dmat_scr[idx // 2, :, hq:]
                for idx in range(nch)]
        ps, stats = [], []
        for idx in range(nch):
            m_a = jnp.max(ss_a[idx], axis=0, keepdims=True)
            m_b = jnp.max(ss_b[idx], axis=0, keepdims=True)
            p_a = jnp.exp2(ss_a[idx] - m_a)
            p_b = jnp.exp2(ss_b[idx] - m_b)
            stats.append((jnp.concatenate([m_a, m_b], axis=1),
                          jnp.concatenate([jnp.sum(p_a, axis=0, keepdims=True),
                                           jnp.sum(p_b, axis=0, keepdims=True)], axis=1)))
            ps.append((p_a.astype(BF16), p_b.astype(BF16)))
        state = []
        for idx in range(nch):
            vt = vt_scr[idx // 2, iq]
            acc = jnp.concatenate(
                [jnp.dot(vt[:, :hq], ps[idx][0], preferred_element_type=F32),
                 jnp.dot(vt, ps[idx][1], preferred_element_type=F32)], axis=1)
            state.extend([stats[idx][0], stats[idx][1], acc])

        def kv_block(j, carry):
            k0 = pl.multiple_of(j * tk, tk)
            boff = lax.convert_element_type((j - iq) * tk, F32)
            ss = [jnp.dot(kaug_scr[idx // 2, idx % 2, pl.ds(k0, tk), :], qas[idx],
                          preferred_element_type=F32) for idx in range(nch)]
            ps, stats = [], []
            for idx in range(nch):
                m, l, _ = carry[3 * idx:3 * idx + 3]
                bc = sl2[idx // 2] * boff
                m_new = jnp.maximum(m, jnp.max(ss[idx], axis=0, keepdims=True) + bc)
                alpha = jnp.exp2(m - m_new)
                p = jnp.exp2(ss[idx] - (m_new - bc))
                stats.append((m_new, alpha * l + jnp.sum(p, axis=0, keepdims=True), alpha))
                ps.append(p.astype(BF16))
            out = []
            for idx in range(nch):
                m_new, l_new, alpha = stats[idx]
                acc = alpha * carry[3 * idx + 2] + jnp.dot(vt_scr[idx // 2, j], ps[idx],
                                                           preferred_element_type=F32)
                out.extend([m_new, l_new, acc])
            return tuple(out)

        state = lax.fori_loop(0, iq, kv_block, tuple(state))
        for hh in range(hpb):
            _, l0, a0, _, l1, a1 = state[6 * hh:6 * hh + 6]
            ot = a0 * (1.0 / l0) - a1 * (lam / l1)
            ot = ot * lax.rsqrt(jnp.mean(ot * ot, axis=0, keepdims=True) + EPS)
            o_ref[pl.ds(q0, tq), hh * hd2:(hh + 1) * hd2] = (ot.T * subg).astype(BF16)
        return 0

    lax.fori_loop(0, seq // tq, q_block, 0)


def _attention(q, k, v, lq1, lk1, lq2, lk2, subg, lam_init, batch, seq, *, tq=512, hpb=4):
    slopes = 2.0 ** (-8.0 * jnp.arange(1, N_HEADS + 1, dtype=F32) / N_HEADS)
    sl = slopes * math.log2(math.e)
    s_hi = sl.astype(BF16).astype(F32)
    s_mid = (sl - s_hi).astype(BF16).astype(F32)
    s_lo = (sl - s_hi - s_mid).astype(BF16).astype(F32)
    sl3 = jnp.stack([s_hi, s_mid, s_lo], axis=1)
    lam_arr = jnp.full((1,), lam_init, F32)
    smem = pl.BlockSpec(memory_space=pltpu.SMEM)
    hw = hpb * V_HEAD_DIM
    head_blk = pl.BlockSpec((seq, hw), lambda b, h: (b, h))
    vec = lambda n: pl.BlockSpec((1, n), lambda b, h: (0, 0))
    row = lambda a: a.astype(F32).reshape(1, -1)
    return pl.pallas_call(
        functools.partial(_attn_body, seq=seq, tq=tq, hpb=hpb),
        grid=(batch, N_HEADS // hpb),
        in_specs=[smem, smem, head_blk, head_blk, head_blk,
                  vec(HEAD_DIM), vec(HEAD_DIM), vec(HEAD_DIM), vec(HEAD_DIM), vec(V_HEAD_DIM)],
        out_specs=head_blk,
        out_shape=jax.ShapeDtypeStruct((batch * seq, D_V), BF16),
        scratch_shapes=[pltpu.VMEM((hpb, 2, seq, V_HEAD_DIM), BF16),
                        pltpu.VMEM((hpb, seq // tq, V_HEAD_DIM, tq), BF16),
                        pltpu.VMEM((hpb, tq, tq), F32)],
        compiler_params=pltpu.CompilerParams(
            dimension_semantics=("parallel", "parallel"),
            vmem_limit_bytes=VMEM_LIMIT),
        name="diff_attention",
    )(sl3, lam_arr, q, k, v, row(lq1), row(lk1), row(lq2), row(lk2), row(subg))


def _ssm_body(u_ref, bblk_ref, are_ref, aim_ref, cblk_ref, d_ref, y_ref,
              x_scr, h_scr, st_scr, *, batch, tt):
    nc = st_scr.shape[-1] // 2
    gcols = u_ref.shape[-1]

    @pl.when(pl.program_id(1) == 0)
    def _():
        st_scr[...] = jnp.zeros_like(st_scr)

    u_f = pltpu.einshape("btn->tbn", u_ref[...].astype(F32)).reshape(tt * batch, gcols)
    x_scr[...] = jnp.dot(u_f.astype(BF16), bblk_ref[...], preferred_element_type=F32)
    ar = jnp.broadcast_to(are_ref[...], (batch, nc))
    ai = jnp.broadcast_to(aim_ref[...], (batch, nc))

    def step(t, carry):
        hr, hi = carry
        r0 = pl.multiple_of(t * batch, batch)
        xr = x_scr[pl.ds(r0, batch), :nc]
        xi = x_scr[pl.ds(r0, batch), nc:]
        nr = ar * hr - ai * hi + xr
        ni = ar * hi + ai * hr + xi
        h_scr[pl.ds(r0, batch), :nc] = nr.astype(BF16)
        h_scr[pl.ds(r0, batch), nc:] = ni.astype(BF16)
        return nr, ni

    hr, hi = lax.fori_loop(0, tt, step, (st_scr[:, :nc], st_scr[:, nc:]), unroll=4)
    st_scr[:, :nc] = hr
    st_scr[:, nc:] = hi
    y = jnp.dot(h_scr[...], cblk_ref[...], preferred_element_type=F32)
    y = _gelu_tanh(y + d_ref[...] * u_f)
    y_ref[...] = pltpu.einshape("tbn->btn", y.reshape(tt, batch, gcols)).astype(BF16)


def _ssm(u3, bblk, are, aim, cblk, d_skip, *, tt=256):
    batch, seq, _ = u3.shape
    gcols = GROUPS_PER_TILE * SSM_GROUP
    ngt = N_GROUPS // GROUPS_PER_TILE
    tt = min(tt, seq)
    rows = tt * batch
    return pl.pallas_call(
        functools.partial(_ssm_body, batch=batch, tt=tt),
        grid=(ngt, seq // tt),
        in_specs=[
            pl.BlockSpec((batch, tt, gcols), lambda g, t: (0, t, g)),
            pl.BlockSpec((None, gcols, 2 * SSM_NC), lambda g, t: (g, 0, 0)),
            pl.BlockSpec((None, 1, SSM_NC), lambda g, t: (g, 0, 0)),
            pl.BlockSpec((None, 1, SSM_NC), lambda g, t: (g, 0, 0)),
            pl.BlockSpec((None, 2 * SSM_NC, gcols), lambda g, t: (g, 0, 0)),
            pl.BlockSpec((1, gcols), lambda g, t: (0, g)),
        ],
        out_specs=pl.BlockSpec((batch, tt, gcols), lambda g, t: (0, t, g)),
        out_shape=jax.ShapeDtypeStruct((batch, seq, D_SSM), BF16),
        scratch_shapes=[pltpu.VMEM((rows, 2 * SSM_NC), F32),
                        pltpu.VMEM((rows, 2 * SSM_NC), BF16),
                        pltpu.VMEM((batch, 2 * SSM_NC), F32)],
        compiler_params=pltpu.CompilerParams(
            dimension_semantics=("parallel", "arbitrary"),
            vmem_limit_bytes=VMEM_LIMIT),
        name="s5_scan",
    )(u3, bblk, are, aim, cblk, d_skip.astype(F32).reshape(1, D_SSM))


def _ssm_params(a_re, a_im, log_dt, b_re, b_im, c_re, c_im):
    gpt = GROUPS_PER_TILE
    ngt = N_GROUPS // gpt
    dt = jnp.exp(log_dt)[:, None]
    mag = jnp.exp(dt * a_re)
    ab_re = mag * jnp.cos(dt * a_im)
    ab_im = mag * jnp.sin(dt * a_im)
    den = a_re * a_re + a_im * a_im
    zr = ab_re - 1.0
    zi = ab_im
    f_re = (zr * a_re + zi * a_im) / den
    f_im = (zi * a_re - zr * a_im) / den
    bb_re = f_re[..., None] * b_re - f_im[..., None] * b_im
    bb_im = f_re[..., None] * b_im + f_im[..., None] * b_re
    eye = jnp.eye(gpt, dtype=F32)

    def bdiag_in(bb):
        t = bb.reshape(ngt, gpt, SSM_STATE, SSM_GROUP)
        return jnp.einsum('tgpi,gh->tgihp', t, eye).reshape(ngt, gpt * SSM_GROUP, SSM_NC)

    def bdiag_out(cc):
        t = cc.reshape(ngt, gpt, SSM_GROUP, SSM_STATE)
        return jnp.einsum('tgop,gh->tgpho', t, eye).reshape(ngt, SSM_NC, gpt * SSM_GROUP)

    bblk = jnp.concatenate([bdiag_in(bb_re), bdiag_in(bb_im)], axis=-1).astype(BF16)
    cblk = jnp.concatenate([bdiag_out(c_re), -bdiag_out(c_im)], axis=1).astype(BF16)
    return bblk, ab_re.reshape(ngt, 1, SSM_NC), ab_im.reshape(ngt, 1, SSM_NC), cblk


def _mix_body(y_ref, oa_ref, gate_ref, x_ref, gw_ref, gb_ref, wba_ref, wbs_ref, wo_ref, out_ref):
    y = y_ref[...]
    z = jnp.dot(y, gw_ref[...], preferred_element_type=F32) + gb_ref[...]
    y2 = (y.astype(F32) * jax.nn.sigmoid(z)).astype(BF16)
    o_ssm = jnp.dot(y2, wbs_ref[...], preferred_element_type=F32)
    o_att = jnp.dot(oa_ref[...], wba_ref[...], preferred_element_type=F32)
    mixed = (jax.nn.sigmoid(gate_ref[:, :D_MODEL].astype(F32)) * o_att
             + jax.nn.sigmoid(gate_ref[:, D_MODEL:].astype(F32)) * o_ssm)
    out_ref[...] = x_ref[...] + jnp.dot(mixed.astype(BF16), wo_ref[...],
                                        preferred_element_type=F32)


def _mix(y, o_att, gates, xf, glu_w, glu_b, w_ba, w_bs, w_out, batch, seq, *, tm=512):
    t = xf.shape[0]
    tm = min(tm, seq)
    full = lambda shape: pl.BlockSpec(shape, lambda i: (0, 0), pipeline_mode=pl.Buffered(1))
    return pl.pallas_call(
        _mix_body,
        grid=(t // tm,),
        in_specs=[
            pl.BlockSpec((tm, D_SSM), lambda i: (i, 0)),
            pl.BlockSpec((tm, D_V), lambda i: (i, 0)),
            pl.BlockSpec((tm, 2 * D_MODEL), lambda i: (i, 0)),
            pl.BlockSpec((tm, D_MODEL), lambda i: (i, 0)),
            full((D_SSM, D_SSM)), full((1, D_SSM)),
            full((D_V, D_MODEL)), full((D_SSM, D_MODEL)), full((D_MODEL, D_MODEL)),
        ],
        out_specs=pl.BlockSpec((tm, D_MODEL), lambda i: (i, 0)),
        out_shape=jax.ShapeDtypeStruct((t, D_MODEL), F32),
        compiler_params=pltpu.CompilerParams(
            dimension_semantics=("parallel",),
            vmem_limit_bytes=VMEM_LIMIT),
        name="gated_merge",
    )(y, o_att, gates, xf, glu_w, glu_b.astype(F32).reshape(1, D_SSM), w_ba, w_bs, w_out)


def _shift_rows(up, prev, k):
    body = pltpu.roll(up, k, axis=0)
    top = pltpu.roll(jnp.concatenate([prev, up[:8]], axis=0), k, axis=0)[8:]
    return jnp.concatenate([top, body[8:]], axis=0)


def _ffn_up_body(x_ref, g_ref, wa_ref, wv_ref, cwa_ref, cwv_ref, cba_ref, cbv_ref,
                 act_ref, h_scr, carry_a, carry_v, *, nsb):
    i = pl.program_id(0)
    j = pl.program_id(1)

    @pl.when(j == 0)
    def _():
        h_scr[...] = _rms_rows(x_ref[...], g_ref[...]).astype(BF16)

        @pl.when(i == 0)
        def _():
            carry_a[...] = jnp.zeros_like(carry_a)
            carry_v[...] = jnp.zeros_like(carry_v)

    h = h_scr[...]
    seq_start = i % nsb == 0

    def conv(w_ref, cw_ref, cb_ref, carry):
        up = jnp.dot(h, w_ref[...], preferred_element_type=F32)
        prev = jnp.where(seq_start, 0.0, carry[j])
        carry[j] = up[up.shape[0] - 8:]
        cw = cw_ref[...]
        return (cb_ref[...] + cw[0:1] * _shift_rows(up, prev, 2)
                + cw[1:2] * _shift_rows(up, prev, 1) + cw[2:3] * up)

    a = conv(wa_ref, cwa_ref, cba_ref, carry_a)
    val = conv(wv_ref, cwv_ref, cbv_ref, carry_v)
    act_ref[...] = (_gelu_tanh(a) * val).astype(BF16)


def _ffn_up(xf, g, wa, wv, cwa, cwv, cba, cbv, seq, *, tm=1024, tn=FF_TILE):
    t = xf.shape[0]
    tm = min(tm, seq)
    ncol = D_FF_PAD // tn
    const = lambda i, j: (0, 0)
    col = lambda i, j: (0, j)
    return pl.pallas_call(
        functools.partial(_ffn_up_body, nsb=seq // tm),
        grid=(t // tm, ncol),
        in_specs=[
            pl.BlockSpec((tm, D_MODEL), lambda i, j: (i, 0)),
            pl.BlockSpec((1, D_MODEL), const),
            pl.BlockSpec((D_MODEL, tn), col),
            pl.BlockSpec((D_MODEL, tn), col),
            pl.BlockSpec((CONV_W, tn), col),
            pl.BlockSpec((CONV_W, tn), col),
            pl.BlockSpec((1, tn), col),
            pl.BlockSpec((1, tn), col),
        ],
        out_specs=pl.BlockSpec((tm, tn), lambda i, j: (i, j)),
        out_shape=jax.ShapeDtypeStruct((t, D_FF_PAD), BF16),
        scratch_shapes=[pltpu.VMEM((tm, D_MODEL), BF16),
                        pltpu.VMEM((ncol, 8, tn), F32),
                        pltpu.VMEM((ncol, 8, tn), F32)],
        compiler_params=pltpu.CompilerParams(
            dimension_semantics=("arbitrary", "arbitrary"),
            vmem_limit_bytes=VMEM_LIMIT),
        name="ffn_up_conv_gate",
    )(xf, g.reshape(1, D_MODEL), wa, wv, cwa, cwv, cba, cbv)


def _ffn_down_body(act_ref, w_ref, x_ref, out_ref):
    out_ref[...] = x_ref[...] + jnp.dot(act_ref[...], w_ref[...], preferred_element_type=F32)


def _ffn_down(act, w_down, xf, *, tm=1024, tn=512):
    t = xf.shape[0]
    tm = min(tm, t)
    return pl.pallas_call(
        _ffn_down_body,
        grid=(t // tm, D_MODEL // tn),
        in_specs=[
            pl.BlockSpec((tm, D_FF_PAD), lambda i, j: (i, 0)),
            pl.BlockSpec((D_FF_PAD, tn), lambda i, j: (0, j)),
            pl.BlockSpec((tm, tn), lambda i, j: (i, j)),
        ],
        out_specs=pl.BlockSpec((tm, tn), lambda i, j: (i, j)),
        out_shape=jax.ShapeDtypeStruct((t, D_MODEL), F32),
        compiler_params=pltpu.CompilerParams(
            dimension_semantics=("parallel", "parallel"),
            vmem_limit_bytes=VMEM_LIMIT),
        name="ffn_down",
    )(act, w_down, xf)


def _pad_ff_cols(a):
    return jnp.pad(a, ((0, 0), (0, D_FF_PAD - D_FF)))


def kernel(x, norm1_g, w_in, q_norm_g, k_norm_g, lambda_q1, lambda_k1, lambda_q2, lambda_k2, subln_g, ssm_a_re, ssm_a_im, ssm_log_dt, ssm_b_re, ssm_b_im, ssm_c_re, ssm_c_im, ssm_d, ssm_glu_w, ssm_glu_b, w_branch_attn, w_branch_ssm, w_out, norm2_g, ffn_w_up, ffn_conv_w, ffn_conv_b, ffn_w_down):
    batch, seq, _ = x.shape
    depth = w_in.shape[0]
    xf = x.reshape(batch * seq, D_MODEL)
    for l in range(depth):
        lam_init = 0.8 - 0.6 * math.exp(-0.3 * l)
        q, k, v, u, gates = _in_proj(xf, norm1_g[l], w_in[l].astype(BF16),
                                     q_norm_g[l], k_norm_g[l], batch, seq)
        o_att = _attention(q, k, v, lambda_q1[l], lambda_k1[l], lambda_q2[l], lambda_k2[l],
                           subln_g[l], lam_init, batch, seq)
        bblk, are, aim, cblk = _ssm_params(ssm_a_re[l], ssm_a_im[l], ssm_log_dt[l],
                                           ssm_b_re[l], ssm_b_im[l], ssm_c_re[l], ssm_c_im[l])
        y = _ssm(u.reshape(batch, seq, D_SSM), bblk, are, aim, cblk, ssm_d[l])
        xf = _mix(y.reshape(batch * seq, D_SSM), o_att, gates, xf,
                  ssm_glu_w[l].astype(BF16), ssm_glu_b[l],
                  w_branch_attn[l].astype(BF16), w_branch_ssm[l].astype(BF16),
                  w_out[l].astype(BF16), batch, seq)
        wu = ffn_w_up[l]
        cw = ffn_conv_w[l]
        cb = ffn_conv_b[l].reshape(1, 2 * D_FF)
        w_down = jnp.pad(ffn_w_down[l].astype(BF16), ((0, D_FF_PAD - D_FF), (0, 0)))
        act = _ffn_up(xf, norm2_g[l],
                      _pad_ff_cols(wu[:, :D_FF].astype(BF16)), _pad_ff_cols(wu[:, D_FF:].astype(BF16)),
                      _pad_ff_cols(cw[:, :D_FF]), _pad_ff_cols(cw[:, D_FF:]),
                      _pad_ff_cols(cb[:, :D_FF]), _pad_ff_cols(cb[:, D_FF:]), seq)
        xf = _ffn_down(act, w_down, xf)
    return xf.reshape(batch, seq, D_MODEL)
```

```python
import functools
import math

import jax
import jax.numpy as jnp
from jax import lax
from jax.experimental import pallas as pl
from jax.experimental.pallas import tpu as pltpu

F32 = jnp.float32
BF16 = jnp.bfloat16

D_MODEL = 2048
N_HEADS = 8
HEAD_DIM = 64
V_HEAD_DIM = 2 * HEAD_DIM
D_QK = N_HEADS * 2 * HEAD_DIM
D_V = N_HEADS * V_HEAD_DIM
D_SSM = D_MODEL // 2
SSM_GROUP = 16
N_GROUPS = D_SSM // SSM_GROUP
SSM_STATE = 64
D_FF = 5504
CONV_W = 3
CHUNK = 64
EPS = 1e-6
IN_COLS = 2 * D_QK + D_V + D_SSM + 2 * D_MODEL

MXU_DIM = 256
FF_TILE = 512
ROW_CHUNK = 256
D_FF_PAD = ((D_FF + FF_TILE - 1) // FF_TILE) * FF_TILE
GROUPS_PER_TILE = 8
SSM_NC = GROUPS_PER_TILE * SSM_STATE
NEG_BIG = -1e30
Q_SCALE = HEAD_DIM ** -0.5 * math.log2(math.e)
VMEM_LIMIT = 56 * 1024 * 1024


def _gelu_tanh(x):
    c = math.sqrt(2.0 / math.pi)
    return x * (0.5 * (1.0 + jnp.tanh(c * (x + 0.044715 * (x * x * x)))))


def _rms_rows(xf, g):
    ms = jnp.mean(xf * xf, axis=-1, keepdims=True)
    return xf * lax.rsqrt(ms + EPS) * g


def _in_proj_body(x_ref, g_ref, w_ref, qg_ref, kg_ref, ones_ref,
                  q_ref, k_ref, v_ref, u_ref, gate_ref, h_scr, *, tn):
    j = pl.program_id(1)
    nq = D_QK // tn

    @pl.when(j == 0)
    def _():
        h_scr[...] = _rms_rows(x_ref[...], g_ref[...]).astype(BF16)

    tm = h_scr.shape[0]
    rc = min(ROW_CHUNK, tm)

    def tile(out_ref, epilogue):
        nchunk = tm // rc
        rows = lambda q: slice(q * rc, (q + 1) * rc)
        dot_q = lambda q: jnp.dot(h_scr[rows(q), :], w_ref[...], preferred_element_type=F32)
        acc = dot_q(0)
        for q in range(1, nchunk):
            nxt = dot_q(q)
            out_ref[rows(q - 1), :] = epilogue(acc).astype(BF16)
            acc = nxt
        out_ref[rows(nchunk - 1), :] = epilogue(acc).astype(BF16)

    def head_norm(gain_ref, scale):
        def epilogue(acc):
            outs = []
            for c in range(tn // MXU_DIM):
                a = acc[:, c * MXU_DIM:(c + 1) * MXU_DIM]
                ss = jnp.dot((a * a).astype(BF16), ones_ref[...], preferred_element_type=F32)
                outs.append(a * lax.rsqrt(ss * (1.0 / HEAD_DIM) + EPS) * (gain_ref[...] * scale))
            return jnp.concatenate(outs, axis=1)
        return epilogue

    plain = lambda acc: acc

    @pl.when(j < nq)
    def _():
        tile(q_ref, head_norm(qg_ref, Q_SCALE))

    @pl.when((j >= nq) & (j < 2 * nq))
    def _():
        tile(k_ref, head_norm(kg_ref, 1.0))

    @pl.when((j >= 2 * nq) & (j < 3 * nq))
    def _():
        tile(v_ref, plain)

    @pl.when((j >= 3 * nq) & (j < 4 * nq))
    def _():
        tile(u_ref, plain)

    @pl.when(j >= 4 * nq)
    def _():
        tile(gate_ref, plain)


def _in_proj(xf, g, w_bf, qg, kg, batch, seq, *, tm=1024, tn=1024):
    t = xf.shape[0]
    tm = min(tm, seq)
    nq = D_QK // tn
    ng = 2 * D_MODEL // tn
    qg_t = jnp.tile(qg.astype(F32), MXU_DIM // HEAD_DIM).reshape(1, MXU_DIM)
    kg_t = jnp.tile(kg.astype(F32), MXU_DIM // HEAD_DIM).reshape(1, MXU_DIM)
    seg = jnp.arange(MXU_DIM) // HEAD_DIM
    ones_bd = (seg[:, None] == seg[None, :]).astype(BF16)

    def cl(j, lo, n):
        return jnp.clip(j - lo, 0, n - 1)

    const = lambda i, j: (0, 0)
    return pl.pallas_call(
        functools.partial(_in_proj_body, tn=tn),
        grid=(t // tm, IN_COLS // tn),
        in_specs=[
            pl.BlockSpec((tm, D_MODEL), lambda i, j: (i, 0)),
            pl.BlockSpec((1, D_MODEL), const),
            pl.BlockSpec((D_MODEL, tn), lambda i, j: (0, j)),
            pl.BlockSpec((1, MXU_DIM), const),
            pl.BlockSpec((1, MXU_DIM), const),
            pl.BlockSpec((MXU_DIM, MXU_DIM), const),
        ],
        out_specs=[
            pl.BlockSpec((tm, tn), lambda i, j: (i, cl(j, 0, nq))),
            pl.BlockSpec((tm, tn), lambda i, j: (i, cl(j, nq, nq))),
            pl.BlockSpec((tm, tn), lambda i, j: (i, cl(j, 2 * nq, nq))),
            pl.BlockSpec((tm, tn), lambda i, j: (i, cl(j, 3 * nq, nq))),
            pl.BlockSpec((tm, tn), lambda i, j: (i, cl(j, 4 * nq, ng))),
        ],
        out_shape=[
            jax.ShapeDtypeStruct((t, D_QK), BF16),
            jax.ShapeDtypeStruct((t, D_QK), BF16),
            jax.ShapeDtypeStruct((t, D_V), BF16),
            jax.ShapeDtypeStruct((t, D_SSM), BF16),
            jax.ShapeDtypeStruct((t, 2 * D_MODEL), BF16),
        ],
        scratch_shapes=[pltpu.VMEM((tm, D_MODEL), BF16)],
        compiler_params=pltpu.CompilerParams(
            dimension_semantics=("arbitrary", "arbitrary"),
            vmem_limit_bytes=VMEM_LIMIT),
        name="in_proj",
    )(xf, g.reshape(1, D_MODEL), w_bf, qg_t, kg_t, ones_bd)


N_BIAS_ROWS = 3
OFFSET_RADIX = 256
BIAS_LANE0 = 0


def _attn_body(sl_ref, laminit_ref, q_ref, k_ref, v_ref, lq1_ref, lk1_ref, lq2_ref, lk2_ref,
               subg_ref, o_ref, kaug_scr, vt_scr, dmat_scr, *, seq, tq, hpb):
    tk = tq
    nblk = seq // tk
    hd2 = 2 * HEAD_DIM
    nch = 2 * hpb
    hg = pl.program_id(1)
    lam_init = laminit_ref[0]
    lam = (jnp.exp(jnp.sum(lq1_ref[...] * lk1_ref[...], axis=-1, keepdims=True))
           - jnp.exp(jnp.sum(lq2_ref[...] * lk2_ref[...], axis=-1, keepdims=True))
           + lam_init)
    subg = subg_ref[...] * (1.0 - lam_init)

    koff = lax.broadcasted_iota(jnp.int32, (tk, hd2), 0)
    klane = lax.broadcasted_iota(jnp.int32, (tk, hd2), 1)
    bias_lo = BIAS_LANE0
    bias_hi = HEAD_DIM + BIAS_LANE0
    kbias = [jnp.where((klane >= lo) & (klane < lo + N_BIAS_ROWS), koff % OFFSET_RADIX,
                       jnp.where((klane >= lo + N_BIAS_ROWS) & (klane < lo + 2 * N_BIAS_ROWS),
                                 koff // OFFSET_RADIX, 0)).astype(F32).astype(BF16)
             for lo in (bias_hi, bias_lo)]
    c = lax.broadcasted_iota(jnp.int32, (tk, tq), 0)
    r = lax.broadcasted_iota(jnp.int32, (tk, tq), 1)
    visible = (c // CHUNK) <= (r // CHUNK)
    ahead = jnp.maximum(c - r, 0).astype(F32)
    brow_i = lax.broadcasted_iota(jnp.int32, (hd2, tq), 0)
    sl2 = []
    brows = []
    for hh in range(hpb):
        h = hg * hpb + hh
        s_hi, s_mid, s_lo = sl_ref[h, 0], sl_ref[h, 1], sl_ref[h, 2]
        sl2.append(s_hi + s_mid + s_lo)
        radix = float(OFFSET_RADIX)
        pieces = (s_hi, s_mid, s_lo, radix * s_hi, radix * s_mid, radix * s_lo)

        def bias_rows(lo):
            rows = jnp.zeros((hd2, tq), F32)
            for n, piece in enumerate(pieces):
                rows = jnp.where(brow_i == lo + n, piece, rows)
            return rows

        brows.append([bias_rows(bias_hi), bias_rows(bias_lo)])
        for blk in range(nblk):
            blk_rows = slice(blk * tk, (blk + 1) * tk)
            k_blk = k_ref[blk_rows, hh * hd2:(hh + 1) * hd2]
            kaug_scr[hh, 0, blk_rows, :] = jnp.where(klane < HEAD_DIM, k_blk, kbias[0])
            kaug_scr[hh, 1, blk_rows, :] = jnp.where(klane >= HEAD_DIM, k_blk, kbias[1])
            vt_scr[hh, blk] = v_ref[blk_rows, hh * hd2:(hh + 1) * hd2].astype(F32).T.astype(BF16)
        dmat_scr[hh] = jnp.where(visible, (-2.0 * sl2[hh]) * ahead, NEG_BIG)

    def q_block(iq, _):
        q0 = pl.multiple_of(iq * tq, tq)
        qas = []
        for hh in range(hpb):
            qt = q_ref[pl.ds(q0, tq), hh * hd2:(hh + 1) * hd2].astype(F32).T
            for comp in range(2):
                keep = (brow_i < HEAD_DIM) if comp == 0 else (brow_i >= HEAD_DIM)
                qas.append(jnp.where(keep, qt, brows[hh][comp]).astype(BF16))
        hq = tq // 2
        ss_a = [jnp.dot(kaug_scr[idx // 2, idx % 2, pl.ds(q0, hq), :], qas[idx][:, :hq],
                        preferred_element_type=F32) + dmat_scr[idx // 2, :hq, :hq]
                for idx in range(nch)]
        ss_b = [jnp.dot(kaug_scr[idx // 2, idx % 2, pl.ds(q0, tk), :], qas[idx][:, hq:],
                        preferred_element_type=F32) + dmat_scr[idx // 2, :, hq:]
                for idx in range(nch)]
        ps, stats = [], []
        for idx in range(nch):
            m_a = jnp.max(ss_a[idx], axis=0, keepdims=True)
            m_b = jnp.max(ss_b[idx], axis=0, keepdims=True)
            p_a = jnp.exp2(ss_a[idx] - m_a)
            p_b = jnp.exp2(ss_b[idx] - m_b)
            stats.append((jnp.concatenate([m_a, m_b], axis=1),
                          jnp.concatenate([jnp.sum(p_a, axis=0, keepdims=True),
                                           jnp.sum(p_b, axis=0, keepdims=True)], axis=1)))
            ps.append((p_a.astype(BF16), p_b.astype(BF16)))
        state = []
        for idx in range(nch):
            vt = vt_scr[idx // 2, iq]
            acc = jnp.concatenate(
                [jnp.dot(vt[:, :hq], ps[idx][0], preferred_element_type=F32),
                 jnp.dot(vt, ps[idx][1], preferred_element_type=F32)], axis=1)
            state.extend([stats[idx][0], stats[idx][1], acc])

        def kv_block(j, carry):
            k0 = pl.multiple_of(j * tk, tk)
            boff = lax.convert_element_type((j - iq) * tk, F32)
            ss = [jnp.dot(kaug_scr[idx // 2, idx % 2, pl.ds(k0, tk), :], qas[idx],
                          preferred_element_type=F32) for idx in range(nch)]
            ps, stats = [], []
            for idx in range(nch):
                m, l, _ = carry[3 * idx:3 * idx + 3]
                bc = sl2[idx // 2] * boff
                m_new = jnp.maximum(m, jnp.max(ss[idx], axis=0, keepdims=True) + bc)
                alpha = jnp.exp2(m - m_new)
                p = jnp.exp2(ss[idx] - (m_new - bc))
                stats.append((m_new, alpha * l + jnp.sum(p, axis=0, keepdims=True), alpha))
                ps.append(p.astype(BF16))
            out = []
            for idx in range(nch):
                m_new, l_new, alpha = stats[idx]
                acc = alpha * carry[3 * idx + 2] + jnp.dot(vt_scr[idx // 2, j], ps[idx],
                                                           preferred_element_type=F32)
                out.extend([m_new, l_new, acc])
            return tuple(out)

        state = lax.fori_loop(0, iq, kv_block, tuple(state))
        for hh in range(hpb):
            _, l0, a0, _, l1, a1 = state[6 * hh:6 * hh + 6]
            ot = a0 * (1.0 / l0) - a1 * (lam / l1)
            ot = ot * lax.rsqrt(jnp.mean(ot * ot, axis=0, keepdims=True) + EPS)
            o_ref[pl.ds(q0, tq), hh * hd2:(hh + 1) * hd2] = (ot.T * subg).astype(BF16)
        return 0

    lax.fori_loop(0, seq // tq, q_block, 0)


def _attention(q, k, v, lq1, lk1, lq2, lk2, subg, lam_init, batch, seq, *, tq=512, hpb=4):
    slopes = 2.0 ** (-8.0 * jnp.arange(1, N_HEADS + 1, dtype=F32) / N_HEADS)
    sl = slopes * math.log2(math.e)
    s_hi = sl.astype(BF16).astype(F32)
    s_mid = (sl - s_hi).astype(BF16).astype(F32)
    s_lo = (sl - s_hi - s_mid).astype(BF16).astype(F32)
    sl3 = jnp.stack([s_hi, s_mid, s_lo], axis=1)
    lam_arr = jnp.full((1,), lam_init, F32)
    smem = pl.BlockSpec(memory_space=pltpu.SMEM)
    hw = hpb * V_HEAD_DIM
    head_blk = pl.BlockSpec((seq, hw), lambda b, h: (b, h))
    vec = lambda n: pl.BlockSpec((1, n), lambda b, h: (0, 0))
    row = lambda a: a.astype(F32).reshape(1, -1)
    return pl.pallas_call(
        functools.partial(_attn_body, seq=seq, tq=tq, hpb=hpb),
        grid=(batch, N_HEADS // hpb),
        in_specs=[smem, smem, head_blk, head_blk, head_blk,
                  vec(HEAD_DIM), vec(HEAD_DIM), vec(HEAD_DIM), vec(HEAD_DIM), vec(V_HEAD_DIM)],
        out_specs=head_blk,
        out_shape=jax.ShapeDtypeStruct((batch * seq, D_V), BF16),
        scratch_shapes=[pltpu.VMEM((hpb, 2, seq, V_HEAD_DIM), BF16),
                        pltpu.VMEM((hpb, seq // tq, V_HEAD_DIM, tq), BF16),
                        pltpu.VMEM((hpb, tq, tq), F32)],
        compiler_params=pltpu.CompilerParams(
            dimension_semantics=("parallel", "parallel"),
            vmem_limit_bytes=VMEM_LIMIT),
        name="diff_attention",
    )(sl3, lam_arr, q, k, v, row(lq1), row(lk1), row(lq2), row(lk2), row(subg))


def _ssm_body(u_ref, bblk_ref, are_ref, aim_ref, cblk_ref, d_ref, y_ref,
              x_scr, h_scr, st_scr, *, batch, tt):
    nc = st_scr.shape[-1] // 2
    gcols = u_ref.shape[-1]

    @pl.when(pl.program_id(1) == 0)
    def _():
        st_scr[...] = jnp.zeros_like(st_scr)

    u_f = pltpu.einshape("btn->tbn", u_ref[...].astype(F32)).reshape(tt * batch, gcols)
    x_scr[...] = jnp.dot(u_f.astype(BF16), bblk_ref[...], preferred_element_type=F32)
    ar = jnp.broadcast_to(are_ref[...], (batch, nc))
    ai = jnp.broadcast_to(aim_ref[...], (batch, nc))

    def step(t, carry):
        hr, hi = carry
        r0 = pl.multiple_of(t * batch, batch)
        xr = x_scr[pl.ds(r0, batch), :nc]
        xi = x_scr[pl.ds(r0, batch), nc:]
        nr = ar * hr - ai * hi + xr
        ni = ar * hi + ai * hr + xi
        h_scr[pl.ds(r0, batch), :nc] = nr.astype(BF16)
        h_scr[pl.ds(r0, batch), nc:] = ni.astype(BF16)
        return nr, ni

    hr, hi = lax.fori_loop(0, tt, step, (st_scr[:, :nc], st_scr[:, nc:]), unroll=4)
    st_scr[:, :nc] = hr
    st_scr[:, nc:] = hi
    y = jnp.dot(h_scr[...], cblk_ref[...], preferred_element_type=F32)
    y = _gelu_tanh(y + d_ref[...] * u_f)
    y_ref[...] = pltpu.einshape("tbn->btn", y.reshape(tt, batch, gcols)).astype(BF16)


def _ssm(u3, bblk, are, aim, cblk, d_skip, *, tt=256):
    batch, seq, _ = u3.shape
    gcols = GROUPS_PER_TILE * SSM_GROUP
    ngt = N_GROUPS // GROUPS_PER_TILE
    tt = min(tt, seq)
    rows = tt * batch
    return pl.pallas_call(
        functools.partial(_ssm_body, batch=batch, tt=tt),
        grid=(ngt, seq // tt),
        in_specs=[
            pl.BlockSpec((batch, tt, gcols), lambda g, t: (0, t, g)),
            pl.BlockSpec((None, gcols, 2 * SSM_NC), lambda g, t: (g, 0, 0)),
            pl.BlockSpec((None, 1, SSM_NC), lambda g, t: (g, 0, 0)),
            pl.BlockSpec((None, 1, SSM_NC), lambda g, t: (g, 0, 0)),
            pl.BlockSpec((None, 2 * SSM_NC, gcols), lambda g, t: (g, 0, 0)),
            pl.BlockSpec((1, gcols), lambda g, t: (0, g)),
        ],
        out_specs=pl.BlockSpec((batch, tt, gcols), lambda g, t: (0, t, g)),
        out_shape=jax.ShapeDtypeStruct((batch, seq, D_SSM), BF16),
        scratch_shapes=[pltpu.VMEM((rows, 2 * SSM_NC), F32),
                        pltpu.VMEM((rows, 2 * SSM_NC), BF16),
                        pltpu.VMEM((batch, 2 * SSM_NC), F32)],
        compiler_params=pltpu.CompilerParams(
            dimension_semantics=("parallel", "arbitrary"),
            vmem_limit_bytes=VMEM_LIMIT),
        name="s5_scan",
    )(u3, bblk, are, aim, cblk, d_skip.astype(F32).reshape(1, D_SSM))


def _ssm_params(a_re, a_im, log_dt, b_re, b_im, c_re, c_im):
    gpt = GROUPS_PER_TILE
    ngt = N_GROUPS // gpt
    dt = jnp.exp(log_dt)[:, None]
    mag = jnp.exp(dt * a_re)
    ab_re = mag * jnp.cos(dt * a_im)
    ab_im = mag * jnp.sin(dt * a_im)
    den = a_re * a_re + a_im * a_im
    zr = ab_re - 1.0
    zi = ab_im
    f_re = (zr * a_re + zi * a_im) / den
    f_im = (zi * a_re - zr * a_im) / den
    bb_re = f_re[..., None] * b_re - f_im[..., None] * b_im
    bb_im = f_re[..., None] * b_im + f_im[..., None] * b_re
    eye = jnp.eye(gpt, dtype=F32)

    def bdiag_in(bb):
        t = bb.reshape(ngt, gpt, SSM_STATE, SSM_GROUP)
        return jnp.einsum('tgpi,gh->tgihp', t, eye).reshape(ngt, gpt * SSM_GROUP, SSM_NC)

    def bdiag_out(cc):
        t = cc.reshape(ngt, gpt, SSM_GROUP, SSM_STATE)
        return jnp.einsum('tgop,gh->tgpho', t, eye).reshape(ngt, SSM_NC, gpt * SSM_GROUP)

    bblk = jnp.concatenate([bdiag_in(bb_re), bdiag_in(bb_im)], axis=-1).astype(BF16)
    cblk = jnp.concatenate([bdiag_out(c_re), -bdiag_out(c_im)], axis=1).astype(BF16)
    return bblk, ab_re.reshape(ngt, 1, SSM_NC), ab_im.reshape(ngt, 1, SSM_NC), cblk


def _mix_body(y_ref, oa_ref, gate_ref, x_ref, gw_ref, gb_ref, wba_ref, wbs_ref, wo_ref, out_ref):
    y = y_ref[...]
    z = jnp.dot(y, gw_ref[...], preferred_element_type=F32) + gb_ref[...]
    y2 = (y.astype(F32) * jax.nn.sigmoid(z)).astype(BF16)
    o_ssm = jnp.dot(y2, wbs_ref[...], preferred_element_type=F32)
    o_att = jnp.dot(oa_ref[...], wba_ref[...], preferred_element_type=F32)
    mixed = (jax.nn.sigmoid(gate_ref[:, :D_MODEL].astype(F32)) * o_att
             + jax.nn.sigmoid(gate_ref[:, D_MODEL:].astype(F32)) * o_ssm)
    out_ref[...] = x_ref[...] + jnp.dot(mixed.astype(BF16), wo_ref[...],
                                        preferred_element_type=F32)


def _mix(y, o_att, gates, xf, glu_w, glu_b, w_ba, w_bs, w_out, batch, seq, *, tm=512):
    t = xf.shape[0]
    tm = min(tm, seq)
    full = lambda shape: pl.BlockSpec(shape, lambda i: (0, 0), pipeline_mode=pl.Buffered(1))
    return pl.pallas_call(
        _mix_body,
        grid=(t // tm,),
        in_specs=[
            pl.BlockSpec((tm, D_SSM), lambda i: (i, 0)),
            pl.BlockSpec((tm, D_V), lambda i: (i, 0)),
            pl.BlockSpec((tm, 2 * D_MODEL), lambda i: (i, 0)),
            pl.BlockSpec((tm, D_MODEL), lambda i: (i, 0)),
            full((D_SSM, D_SSM)), full((1, D_SSM)),
            full((D_V, D_MODEL)), full((D_SSM, D_MODEL)), full((D_MODEL, D_MODEL)),
        ],
        out_specs=pl.BlockSpec((tm, D_MODEL), lambda i: (i, 0)),
        out_shape=jax.ShapeDtypeStruct((t, D_MODEL), F32),
        compiler_params=pltpu.CompilerParams(
            dimension_semantics=("parallel",),
            vmem_limit_bytes=VMEM_LIMIT),
        name="gated_merge",
    )(y, o_att, gates, xf, glu_w, glu_b.astype(F32).reshape(1, D_SSM), w_ba, w_bs, w_out)


def _shift_rows(up, prev, k):
    body = pltpu.roll(up, k, axis=0)
    top = pltpu.roll(jnp.concatenate([prev, up[:8]], axis=0), k, axis=0)[8:]
    return jnp.concatenate([top, body[8:]], axis=0)


def _ffn_up_body(x_ref, g_ref, wa_ref, wv_ref, cwa_ref, cwv_ref, cba_ref, cbv_ref,
                 act_ref, h_scr, carry_a, carry_v, *, nsb):
    i = pl.program_id(0)
    j = pl.program_id(1)

    @pl.when(j == 0)
    def _():
        h_scr[...] = _rms_rows(x_ref[...], g_ref[...]).astype(BF16)

        @pl.when(i == 0)
        def _():
            carry_a[...] = jnp.zeros_like(carry_a)
            carry_v[...] = jnp.zeros_like(carry_v)

    h = h_scr[...]
    seq_start = i % nsb == 0

    def conv(w_ref, cw_ref, cb_ref, carry):
        up = jnp.dot(h, w_ref[...], preferred_element_type=F32)
        prev = jnp.where(seq_start, 0.0, carry[j])
        carry[j] = up[up.shape[0] - 8:]
        cw = cw_ref[...]
        return (cb_ref[...] + cw[0:1] * _shift_rows(up, prev, 2)
                + cw[1:2] * _shift_rows(up, prev, 1) + cw[2:3] * up)

    a = conv(wa_ref, cwa_ref, cba_ref, carry_a)
    val = conv(wv_ref, cwv_ref, cbv_ref, carry_v)
    act_ref[...] = (_gelu_tanh(a) * val).astype(BF16)


def _ffn_up(xf, g, wa, wv, cwa, cwv, cba, cbv, seq, *, tm=1024, tn=FF_TILE):
    t = xf.shape[0]
    tm = min(tm, seq)
    ncol = D_FF_PAD // tn
    const = lambda i, j: (0, 0)
    col = lambda i, j: (0, j)
    return pl.pallas_call(
        functools.partial(_ffn_up_body, nsb=seq // tm),
        grid=(t // tm, ncol),
        in_specs=[
            pl.BlockSpec((tm, D_MODEL), lambda i, j: (i, 0)),
            pl.BlockSpec((1, D_MODEL), const),
            pl.BlockSpec((D_MODEL, tn), col),
            pl.BlockSpec((D_MODEL, tn), col),
            pl.BlockSpec((CONV_W, tn), col),
            pl.BlockSpec((CONV_W, tn), col),
            pl.BlockSpec((1, tn), col),
            pl.BlockSpec((1, tn), col),
        ],
        out_specs=pl.BlockSpec((tm, tn), lambda i, j: (i, j)),
        out_shape=jax.ShapeDtypeStruct((t, D_FF_PAD), BF16),
        scratch_shapes=[pltpu.VMEM((tm, D_MODEL), BF16),
                        pltpu.VMEM((ncol, 8, tn), F32),
                        pltpu.VMEM((ncol, 8, tn), F32)],
        compiler_params=pltpu.CompilerParams(
            dimension_semantics=("arbitrary", "arbitrary"),
            vmem_limit_bytes=VMEM_LIMIT),
        name="ffn_up_conv_gate",
    )(xf, g.reshape(1, D_MODEL), wa, wv, cwa, cwv, cba, cbv)


def _ffn_down_body(act_ref, w_ref, x_ref, out_ref):
    out_ref[...] = x_ref[...] + jnp.dot(act_ref[...], w_ref[...], preferred_element_type=F32)


def _ffn_down(act, w_down, xf, *, tm=1024, tn=1024):
    t = xf.shape[0]
    tm = min(tm, t)
    return pl.pallas_call(
        _ffn_down_body,
        grid=(t // tm, D_MODEL // tn),
        in_specs=[
            pl.BlockSpec((tm, D_FF_PAD), lambda i, j: (i, 0), pipeline_mode=pl.Buffered(1)),
            pl.BlockSpec((D_FF_PAD, tn), lambda i, j: (0, j)),
            pl.BlockSpec((tm, tn), lambda i, j: (i, j)),
        ],
        out_specs=pl.BlockSpec((tm, tn), lambda i, j: (i, j)),
        out_shape=jax.ShapeDtypeStruct((t, D_MODEL), F32),
        compiler_params=pltpu.CompilerParams(
            dimension_semantics=("parallel", "parallel"),
            vmem_limit_bytes=VMEM_LIMIT),
        name="ffn_down",
    )(act, w_down, xf)


def _pad_ff_cols(a):
    return jnp.pad(a, ((0, 0), (0, D_FF_PAD - D_FF)))


def kernel(x, norm1_g, w_in, q_norm_g, k_norm_g, lambda_q1, lambda_k1, lambda_q2, lambda_k2, subln_g, ssm_a_re, ssm_a_im, ssm_log_dt, ssm_b_re, ssm_b_im, ssm_c_re, ssm_c_im, ssm_d, ssm_glu_w, ssm_glu_b, w_branch_attn, w_branch_ssm, w_out, norm2_g, ffn_w_up, ffn_conv_w, ffn_conv_b, ffn_w_down):
    batch, seq, _ = x.shape
    depth = w_in.shape[0]
    xf = x.reshape(batch * seq, D_MODEL)
    for l in range(depth):
        lam_init = 0.8 - 0.6 * math.exp(-0.3 * l)
        q, k, v, u, gates = _in_proj(xf, norm1_g[l], w_in[l].astype(BF16),
                                     q_norm_g[l], k_norm_g[l], batch, seq)
        o_att = _attention(q, k, v, lambda_q1[l], lambda_k1[l], lambda_q2[l], lambda_k2[l],
                           subln_g[l], lam_init, batch, seq)
        bblk, are, aim, cblk = _ssm_params(ssm_a_re[l], ssm_a_im[l], ssm_log_dt[l],
                                           ssm_b_re[l], ssm_b_im[l], ssm_c_re[l], ssm_c_im[l])
        y = _ssm(u.reshape(batch, seq, D_SSM), bblk, are, aim, cblk, ssm_d[l])
        xf = _mix(y.reshape(batch * seq, D_SSM), o_att, gates, xf,
                  ssm_glu_w[l].astype(BF16), ssm_glu_b[l],
                  w_branch_attn[l].astype(BF16), w_branch_ssm[l].astype(BF16),
                  w_out[l].astype(BF16), batch, seq)
        wu = ffn_w_up[l]
        cw = ffn_conv_w[l]
        cb = ffn_conv_b[l].reshape(1, 2 * D_FF)
        w_down = jnp.pad(ffn_w_down[l].astype(BF16), ((0, D_FF_PAD - D_FF), (0, 0)))
        act = _ffn_up(xf, norm2_g[l],
                      _pad_ff_cols(wu[:, :D_FF].astype(BF16)), _pad_ff_cols(wu[:, D_FF:].astype(BF16)),
                      _pad_ff_cols(cw[:, :D_FF]), _pad_ff_cols(cw[:, D_FF:]),
                      _pad_ff_cols(cb[:, :D_FF]), _pad_ff_cols(cb[:, D_FF:]), seq)
        xf = _ffn_down(act, w_down, xf)
    return xf.reshape(batch, seq, D_MODEL)
```
